```python
import math
import jax, jax.numpy as jnp
from jax import lax
import numpy as np

D_MODEL = 4096
BATCH = 4
SEQ = 4096
DEPTH = 2

N_MIXERS = 2
EPS = 1e-6
NEG_INF = -1e30

REL_BUCKETS = 32
REL_MAX_DISTANCE = 128

NSA_HEADS = 64
NSA_KV_GROUPS = 4
NSA_HEADS_PER_GROUP = NSA_HEADS // NSA_KV_GROUPS
NSA_QK_DIM = 192
NSA_V_DIM = 128
CMP_BLOCK = 32
CMP_STRIDE = 16
SEL_BLOCK = 64
SEL_TOPK = 16
WINDOW = 512
NSA_Q_BLOCK = 64
NSA_Q_WIDTH = NSA_HEADS * NSA_QK_DIM
NSA_KV_WIDTH = NSA_KV_GROUPS * (NSA_QK_DIM + NSA_V_DIM)
NSA_Z_WIDTH = NSA_HEADS * NSA_V_DIM
NSA_GATE_WIDTH = 3 * NSA_HEADS
NSA_IN_DIM = NSA_Q_WIDTH + 3 * NSA_KV_WIDTH + 3 * NSA_Z_WIDTH + NSA_GATE_WIDTH

MLA_HEADS = 64
MLA_Q_LORA = 1536
MLA_KV_LORA = 512
MLA_NOPE_DIM = 128
MLA_ROPE_DIM = 64
MLA_V_DIM = 128
MLA_QK_DIM = MLA_NOPE_DIM + MLA_ROPE_DIM
MLA_Z_WIDTH = MLA_HEADS * MLA_V_DIM
MLA_IN_DIM = MLA_Q_LORA + MLA_KV_LORA + MLA_ROPE_DIM + MLA_Z_WIDTH
MLA_Q_BLOCK = 128
ROPE_BASE = 10000.0

kernel_name = "nsa_mla_interleaved_hybrid"


def rms_norm(x, gain):
    xf = x.astype(jnp.float32)
    y = xf * lax.rsqrt(jnp.mean(xf * xf, axis=-1, keepdims=True) + EPS)
    return (y * gain.astype(jnp.float32)).astype(x.dtype)


def t5_bucket(dist):
    n = jnp.maximum(dist, 0)
    max_exact = REL_BUCKETS // 2
    scaled = jnp.log(jnp.maximum(n, max_exact).astype(jnp.float32) / max_exact) / math.log(REL_MAX_DISTANCE / max_exact)
    large = jnp.minimum(max_exact + (scaled * (REL_BUCKETS - max_exact)).astype(jnp.int32), REL_BUCKETS - 1)
    return jnp.where(n < max_exact, n, large)


def masked_softmax(logits, mask):
    p = jax.nn.softmax(jnp.where(mask, logits, NEG_INF), axis=-1)
    return p * jnp.any(mask, axis=-1, keepdims=True)


def rope(x, positions):
    half = MLA_ROPE_DIM // 2
    inv_freq = ROPE_BASE ** (-jnp.arange(half, dtype=jnp.float32) / half)
    ang = positions.astype(jnp.float32)[:, :, None, None] * inv_freq
    cos, sin = jnp.cos(ang), jnp.sin(ang)
    xf = x.astype(jnp.float32)
    x1, x2 = xf[..., :half], xf[..., half:]
    return jnp.concatenate([x1 * cos - x2 * sin, x2 * cos + x1 * sin], axis=-1).astype(x.dtype)


def compress_blocks(x_raw, pe, w1, w2):
    B, S, G, d = x_raw.shape
    n_cmp = (S - CMP_BLOCK) // CMP_STRIDE + 1
    win = CMP_STRIDE * jnp.arange(n_cmp)[:, None] + jnp.arange(CMP_BLOCK)[None, :]
    blocks = x_raw[:, win] + pe[None, None, :, None, :]
    flat = blocks.transpose(0, 1, 3, 2, 4).reshape(B, n_cmp, G, CMP_BLOCK * d)
    return jax.nn.silu(flat @ w1) @ w2


def nsa_mixer(h, w_in, q_norm, k_norm, pe_k, w1_k, w2_k, pe_v, w1_v, w2_v, rel_bias, w_out):
    B, S, _ = h.shape
    H, G, R, dk, dv = NSA_HEADS, NSA_KV_GROUPS, NSA_HEADS_PER_GROUP, NSA_QK_DIM, NSA_V_DIM
    QB = NSA_Q_BLOCK
    sizes = [NSA_Q_WIDTH] + [NSA_KV_WIDTH] * 3 + [NSA_Z_WIDTH] * 3 + [NSA_GATE_WIDTH]
    cuts = [int(c) for c in np.cumsum(sizes)[:-1]]
    q, kv_c, kv_s, kv_w, z_c, z_s, z_w, g = jnp.split(h @ w_in, cuts, axis=-1)

    q = rms_norm(q.reshape(B, S, G, R, dk), q_norm)

    def split_kv(t):
        return t[..., :G * dk].reshape(B, S, G, dk), t[..., G * dk:].reshape(B, S, G, dv)

    kc_raw, vc_raw = split_kv(kv_c)
    ks_raw, vs = split_kv(kv_s)
    kw_raw, vw = split_kv(kv_w)
    kc = rms_norm(compress_blocks(kc_raw, pe_k, w1_k, w2_k), k_norm[0])
    vc = compress_blocks(vc_raw, pe_v, w1_v, w2_v)
    ks = rms_norm(ks_raw, k_norm[1])
    kw = rms_norm(kw_raw, k_norm[2])

    n_cmp = kc.shape[1]
    n_sel = S // SEL_BLOCK
    k_sel = min(SEL_TOPK, n_sel)
    ks_b = ks.reshape(B, n_sel, SEL_BLOCK, G, dk).transpose(0, 3, 1, 2, 4)
    vs_b = vs.reshape(B, n_sel, SEL_BLOCK, G, dv).transpose(0, 3, 1, 2, 4)
    kw_p = jnp.pad(kw, ((0, 0), (WINDOW, 0), (0, 0), (0, 0)))
    vw_p = jnp.pad(vw, ((0, 0), (WINDOW, 0), (0, 0), (0, 0)))

    gates = jax.nn.sigmoid(g).reshape(B, S, 3, G, R)
    a_c = gates[:, :, 0, :, :, None] * jax.nn.silu(z_c).reshape(B, S, G, R, dv)
    a_s = gates[:, :, 1, :, :, None] * jax.nn.silu(z_s).reshape(B, S, G, R, dv)
    a_w = gates[:, :, 2, :, :, None] * jax.nn.silu(z_w).reshape(B, S, G, R, dv)

    tbl = rel_bias.reshape(REL_BUCKETS, G, R)
    cmp_start = CMP_STRIDE * jnp.arange(n_cmp)
    cmp_end = cmp_start + CMP_BLOCK - 1
    sel_start = SEL_BLOCK * jnp.arange(n_sel)
    overlap = ((cmp_start[:, None] < (sel_start + SEL_BLOCK)[None, :])
               & ((cmp_end + 1)[:, None] > sel_start[None, :])).astype(jnp.float32)
    scale = dk ** -0.5
    bi = jnp.arange(B)[:, None, None, None]
    gi = jnp.arange(G)[None, None, :, None]
    blk = jnp.arange(n_sel)
    span = QB + WINDOW

    def block(c):
        t0 = c * QB
        tpos = t0 + jnp.arange(QB)
        qc = lax.dynamic_slice_in_dim(q, t0, QB, axis=1)

        d_c = tpos[:, None] - cmp_end[None, :]
        lg = jnp.einsum('bqgrd,bngd->bqgrn', qc, kc, preferred_element_type=jnp.float32) * scale
        lg = lg + tbl[t5_bucket(d_c)].transpose(0, 2, 3, 1)
        p_c = masked_softmax(lg, (d_c >= 0)[:, None, None, :])
        o_c = jnp.einsum('bqgrn,bngd->bqgrd', p_c.astype(vc.dtype), vc)

        imp = jnp.einsum('bqgrn,nj->bqgj', p_c, overlap)
        cur = (tpos // SEL_BLOCK)[:, None]
        forced = (blk[None, :] == 0) | (blk[None, :] == cur) | (blk[None, :] == cur - 1)
        future = sel_start[None, :] > tpos[:, None]
        imp = jnp.where(forced[:, None, :], jnp.inf, jnp.where(future[:, None, :], -jnp.inf, imp))
        _, idx = lax.top_k(imp, k_sel)
        k_g = ks_b[bi, gi, idx].reshape(B, QB, G, k_sel * SEL_BLOCK, dk)
        v_g = vs_b[bi, gi, idx].reshape(B, QB, G, k_sel * SEL_BLOCK, dv)
        tok = idx[..., None] * SEL_BLOCK + jnp.arange(SEL_BLOCK)
        d_s = tpos[None, :, None, None, None] - tok
        b_s = tbl[t5_bucket(d_s), gi[..., None]]
        b_s = b_s.transpose(0, 1, 2, 5, 3, 4).reshape(B, QB, G, R, k_sel * SEL_BLOCK)
        lg = jnp.einsum('bqgrd,bqgmd->bqgrm', qc, k_g, preferred_element_type=jnp.float32) * scale + b_s
        p_s = masked_softmax(lg, (d_s >= 0).reshape(B, QB, G, 1, k_sel * SEL_BLOCK))
        o_s = jnp.einsum('bqgrm,bqgmd->bqgrd', p_s.astype(v_g.dtype), v_g)

        k_w = lax.dynamic_slice_in_dim(kw_p, t0, span, axis=1)
        v_w = lax.dynamic_slice_in_dim(vw_p, t0, span, axis=1)
        kpos = t0 - WINDOW + jnp.arange(span)
        d_w = tpos[:, None] - kpos[None, :]
        m_w = (d_w >= 0) & (d_w < WINDOW) & (kpos >= 0)[None, :]
        lg = jnp.einsum('bqgrd,blgd->bqgrl', qc, k_w, preferred_element_type=jnp.float32) * scale
        lg = lg + tbl[t5_bucket(d_w)].transpose(0, 2, 3, 1)
        p_w = masked_softmax(lg, m_w[:, None, None, :])
        o_w = jnp.einsum('bqgrl,blgd->bqgrd', p_w.astype(v_w.dtype), v_w)

        o = (o_c * lax.dynamic_slice_in_dim(a_c, t0, QB, axis=1)
             + o_s * lax.dynamic_slice_in_dim(a_s, t0, QB, axis=1)
             + o_w * lax.dynamic_slice_in_dim(a_w, t0, QB, axis=1))
        return o.reshape(B, QB, H * dv)

    out = lax.map(block, jnp.arange(S // QB))
    out = out.transpose(1, 0, 2, 3).reshape(B, S, H * dv)
    return out @ w_out


def mla_mixer(h, positions, w_in, q_a_norm, w_q_b, kv_a_norm, w_kv_b, q_norm, k_norm, w_out):
    B, S, _ = h.shape
    H, QB = MLA_HEADS, MLA_Q_BLOCK
    c_q, c_kv, k_pe, z = jnp.split(h @ w_in, [MLA_Q_LORA, MLA_Q_LORA + MLA_KV_LORA,
                                             MLA_Q_LORA + MLA_KV_LORA + MLA_ROPE_DIM], axis=-1)
    q = (rms_norm(c_q, q_a_norm) @ w_q_b).reshape(B, S, H, MLA_QK_DIM)
    kv = (rms_norm(c_kv, kv_a_norm) @ w_kv_b).reshape(B, S, H, MLA_NOPE_DIM + MLA_V_DIM)
    k_nope, v = kv[..., :MLA_NOPE_DIM], kv[..., MLA_NOPE_DIM:]
    k = jnp.concatenate([k_nope, jnp.broadcast_to(k_pe[:, :, None, :], (B, S, H, MLA_ROPE_DIM))], axis=-1)
    q = rms_norm(q, q_norm)
    k = rms_norm(k, k_norm)
    q = jnp.concatenate([q[..., :MLA_NOPE_DIM], rope(q[..., MLA_NOPE_DIM:], positions)], axis=-1)
    k = jnp.concatenate([k[..., :MLA_NOPE_DIM], rope(k[..., MLA_NOPE_DIM:], positions)], axis=-1)
    scale = MLA_QK_DIM ** -0.5
    outs = []
    for j in range(S // QB):
        lo, hi = j * QB, (j + 1) * QB
        lg = jnp.einsum('bqhd,bkhd->bhqk', q[:, lo:hi], k[:, :hi], preferred_element_type=jnp.float32) * scale
        mask = (lo + jnp.arange(QB))[:, None] >= jnp.arange(hi)[None, :]
        p = jax.nn.softmax(jnp.where(mask, lg, NEG_INF), axis=-1)
        outs.append(jnp.einsum('bhqk,bkhd->bqhd', p.astype(v.dtype), v[:, :hi]))
    o = jnp.concatenate(outs, axis=1) * jax.nn.silu(z).reshape(B, S, H, MLA_V_DIM)
    return o.reshape(B, S, MLA_Z_WIDTH) @ w_out


def setup_inputs(seed: int = 0) -> dict:
    key = jax.random.key(seed)
    ks = iter(jax.random.split(key, 32))
    n_nsa = (DEPTH + N_MIXERS - 1) // N_MIXERS
    n_mla = DEPTH // N_MIXERS
    f32 = jnp.float32

    def nrm(shape, scale):
        return jax.random.normal(next(ks), shape, f32) * scale

    def gain(shape):
        return 1.0 + 0.01 * jax.random.normal(next(ks), shape, f32)

    x = nrm((BATCH, SEQ, D_MODEL), 1.0)
    offset = jax.random.randint(next(ks), (BATCH, 1), 0, 1024, dtype=jnp.int32)
    positions = (offset + jnp.arange(SEQ, dtype=jnp.int32)[None, :]).astype(jnp.int32)
    return {
        "x": x,
        "positions": positions,
        "norm_w": gain((DEPTH, D_MODEL)),
        "rel_bias": nrm((REL_BUCKETS, NSA_HEADS), 0.2),
        "nsa_w_in": nrm((n_nsa, D_MODEL, NSA_IN_DIM), D_MODEL ** -0.5),
        "nsa_q_norm": gain((n_nsa, NSA_QK_DIM)),
        "nsa_k_norm": gain((n_nsa, 3, NSA_QK_DIM)),
        "nsa_cmp_pe_k": nrm((n_nsa, CMP_BLOCK, NSA_QK_DIM), 0.1),
        "nsa_cmp_w1_k": nrm((n_nsa, CMP_BLOCK * NSA_QK_DIM, NSA_QK_DIM), (CMP_BLOCK * NSA_QK_DIM) ** -0.5),
        "nsa_cmp_w2_k": nrm((n_nsa, NSA_QK_DIM, NSA_QK_DIM), NSA_QK_DIM ** -0.5),
        "nsa_cmp_pe_v": nrm((n_nsa, CMP_BLOCK, NSA_V_DIM), 0.1),
        "nsa_cmp_w1_v": nrm((n_nsa, CMP_BLOCK * NSA_V_DIM, NSA_V_DIM), (CMP_BLOCK * NSA_V_DIM) ** -0.5),
        "nsa_cmp_w2_v": nrm((n_nsa, NSA_V_DIM, NSA_V_DIM), NSA_V_DIM ** -0.5),
        "nsa_w_out": nrm((n_nsa, NSA_Z_WIDTH, D_MODEL), NSA_Z_WIDTH ** -0.5),
        "mla_w_in": nrm((n_mla, D_MODEL, MLA_IN_DIM), D_MODEL ** -0.5),
        "mla_q_a_norm": gain((n_mla, MLA_Q_LORA)),
        "mla_w_q_b": nrm((n_mla, MLA_Q_LORA, MLA_HEADS * MLA_QK_DIM), MLA_Q_LORA ** -0.5),
        "mla_kv_a_norm": gain((n_mla, MLA_KV_LORA)),
        "mla_w_kv_b": nrm((n_mla, MLA_KV_LORA, MLA_HEADS * (MLA_NOPE_DIM + MLA_V_DIM)), MLA_KV_LORA ** -0.5),
        "mla_q_norm": gain((n_mla, MLA_QK_DIM)),
        "mla_k_norm": gain((n_mla, MLA_QK_DIM)),
        "mla_w_out": nrm((n_mla, MLA_Z_WIDTH, D_MODEL), MLA_Z_WIDTH ** -0.5),
    }


def reference(x, positions, norm_w, rel_bias, nsa_w_in, nsa_q_norm, nsa_k_norm, nsa_cmp_pe_k, nsa_cmp_w1_k,
              nsa_cmp_w2_k, nsa_cmp_pe_v, nsa_cmp_w1_v, nsa_cmp_w2_v, nsa_w_out, mla_w_in, mla_q_a_norm,
              mla_w_q_b, mla_kv_a_norm, mla_w_kv_b, mla_q_norm, mla_k_norm, mla_w_out):
    for i in range(DEPTH):
        h = rms_norm(x, norm_w[i])
        j = i // N_MIXERS
        if i % N_MIXERS == 0:
            y = nsa_mixer(h, nsa_w_in[j], nsa_q_norm[j], nsa_k_norm[j], nsa_cmp_pe_k[j], nsa_cmp_w1_k[j],
                          nsa_cmp_w2_k[j], nsa_cmp_pe_v[j], nsa_cmp_w1_v[j], nsa_cmp_w2_v[j], rel_bias, nsa_w_out[j])
        else:
            y = mla_mixer(h, positions, mla_w_in[j], mla_q_a_norm[j], mla_w_q_b[j], mla_kv_a_norm[j],
                          mla_w_kv_b[j], mla_q_norm[j], mla_k_norm[j], mla_w_out[j])
        x = x + y
    return x
```

```python
import functools
import math

import numpy as np
import jax
import jax.numpy as jnp
from jax import lax
from jax.experimental import pallas as pl
from jax.experimental.pallas import tpu as pltpu

F32 = jnp.float32
BF16 = jnp.bfloat16

EPS = 1e-6
REL_BUCKETS = 32
REL_MAX_DISTANCE = 128
NSA_HEADS = 64
NSA_KV_GROUPS = 4
NSA_QK_DIM = 192
NSA_V_DIM = 128
CMP_BLOCK = 32
CMP_STRIDE = 16
SEL_BLOCK = 64
SEL_TOPK = 16
WINDOW = 512
MLA_HEADS = 64
MLA_Q_LORA = 1536
MLA_KV_LORA = 512
MLA_NOPE_DIM = 128
MLA_ROPE_DIM = 64
MLA_V_DIM = 128
ROPE_BASE = 10000.0

V7X_LANES = 128
V7X_VMEM_BYTES = 64 * 1024 * 1024
VMEM_LIMIT_BYTES = V7X_VMEM_BYTES - 8 * 1024 * 1024

DKP = 2 * V7X_LANES
DV = V7X_LANES
TQ = 256
NEG = -1e30
M_INIT = -1e20
MLA_HEADS_PER_STEP = 2


def _cparams(sem):
    return pltpu.CompilerParams(dimension_semantics=sem, vmem_limit_bytes=VMEM_LIMIT_BYTES)


def _dot(a, b):
    return jnp.dot(a, b, preferred_element_type=F32)


def _dot_nt(a, b):
    return lax.dot_general(a, b, (((1,), (1,)), ((), ())), preferred_element_type=F32)


def _sigmoid(x):
    return 1.0 / (1.0 + jnp.exp(-x))


def _silu(x):
    return x * _sigmoid(x)


def _pick(n, target, unit=1):
    best = unit
    c = unit
    while c <= min(n, max(target, unit)):
        if n % c == 0:
            best = c
        c += unit
    return best


def _rmsnorm_kernel(x_ref, g_ref, o_ref, *, inv_n):
    x = x_ref[...].astype(F32)
    ms = jnp.sum(x * x, axis=-1, keepdims=True) * inv_n
    o_ref[...] = (x * lax.rsqrt(ms + EPS) * g_ref[...]).astype(o_ref.dtype)


def rmsnorm_rows(x, gain, n_true, out_dtype, name):
    m, d = x.shape
    tm = _pick(m, max(8, (2 * 1024 * 1024) // (4 * d)), 8)
    return pl.pallas_call(
        functools.partial(_rmsnorm_kernel, inv_n=1.0 / n_true),
        grid=(m // tm,),
        in_specs=[pl.BlockSpec((tm, d), lambda i: (i, 0)), pl.BlockSpec((1, d), lambda i: (0, 0))],
        out_specs=pl.BlockSpec((tm, d), lambda i: (i, 0)),
        out_shape=jax.ShapeDtypeStruct((m, d), out_dtype),
        compiler_params=_cparams(("parallel",)),
        name=name,
    )(x, gain.reshape(1, d).astype(F32))


def _matmul_kernel(*refs, epilogue, n_in):
    a_ref, w_ref = refs[0], refs[1]
    acc = _dot(a_ref[...], w_ref[...])
    epilogue(acc, refs[2:n_in], refs[n_in:])


def matmul(a, w, epilogue, *, tm, tn, out_shapes, out_specs, extra=(), extra_specs=(), name):
    m, k = a.shape
    n = w.shape[1]
    assert m % tm == 0 and n % tn == 0, (m, tm, n, tn)
    in_specs = [pl.BlockSpec((tm, k), lambda i, j: (i, 0)), pl.BlockSpec((k, tn), lambda i, j: (0, j))]
    in_specs += list(extra_specs)
    return pl.pallas_call(
        functools.partial(_matmul_kernel, epilogue=epilogue, n_in=2 + len(extra)),
        grid=(m // tm, n // tn),
        in_specs=in_specs,
        out_specs=out_specs,
        out_shape=out_shapes,
        compiler_params=_cparams(("parallel", "arbitrary")),
        name=name,
    )(a, w, *extra)


def _ep_store(acc, extra, outs):
    outs[0][...] = acc.astype(outs[0].dtype)


def _ep_sigmoid(acc, extra, outs):
    outs[0][...] = _sigmoid(acc)


def _ep_residual(acc, extra, outs):
    outs[0][...] = extra[0][...] + acc


def _rope_lanes(x, cos2, sin2):
    half = MLA_ROPE_DIM // 2
    lane = lax.broadcasted_iota(jnp.int32, x.shape, 1)
    up = pltpu.roll(x, V7X_LANES - half, axis=1)
    down = pltpu.roll(x, half, axis=1)
    rot = jnp.where(lane < half, up, down)
    return x * cos2 + rot * sin2


def _ep_q_heads(acc, extra, outs, *, hb, n_true, scale, rope):
    gain = extra[0][...]
    for r in range(hb):
        y = acc[:, r * DKP:(r + 1) * DKP]
        ms = jnp.sum(y * y, axis=-1, keepdims=True) * (1.0 / n_true)
        yn = y * lax.rsqrt(ms + EPS) * gain
        if rope:
            yr = _rope_lanes(yn[:, V7X_LANES:], extra[1][...], extra[2][...])
            yn = jnp.concatenate([yn[:, :V7X_LANES], yr], axis=1)
        outs[0][0, r] = (yn * scale).astype(outs[0].dtype)


def _ep_gate_heads(acc, extra, outs, *, hb, gated):
    for r in range(hb):
        z = acc[:, r * DV:(r + 1) * DV]
        a = _silu(z)
        if gated:
            a = a * extra[0][0][:, r:r + 1]
        outs[0][r] = a.astype(outs[0].dtype)


def _ep_mla_kv(acc, extra, outs, *, hb):
    kpe = extra[0][...]
    gain_n = extra[1][...]
    gain_r = extra[2][...]
    ss_pe = jnp.sum(kpe * kpe, axis=-1, keepdims=True)
    kr = _rope_lanes(kpe * gain_r, extra[3][...], extra[4][...])
    inv_n = 1.0 / (MLA_NOPE_DIM + MLA_ROPE_DIM)
    for r in range(hb):
        kn = acc[:, r * DKP:r * DKP + V7X_LANES]
        v = acc[:, r * DKP + V7X_LANES:(r + 1) * DKP]
        rs = lax.rsqrt((jnp.sum(kn * kn, axis=-1, keepdims=True) + ss_pe) * inv_n + EPS)
        outs[0][0, r] = jnp.concatenate([kn * rs * gain_n, kr * rs], axis=1).astype(outs[0].dtype)
        outs[1][0, r] = v.astype(outs[1].dtype)


def _compress_kernel(xa_ref, xb_ref, pea_ref, peb_ref, w1a_ref, w1b_ref, w2_ref, g_ref, o_ref, *, norm_n):
    xa = (xa_ref[0, 0] + pea_ref[...]).astype(BF16)
    xb = (xb_ref[0, 0] + peb_ref[...]).astype(BF16)
    pre = _dot(xa, w1a_ref[...]) + _dot(xb, w1b_ref[...])
    y = _dot(_silu(pre).astype(BF16), w2_ref[...])
    if norm_n:
        ms = jnp.sum(y * y, axis=-1, keepdims=True) * (1.0 / norm_n)
        y = y * lax.rsqrt(ms + EPS) * g_ref[...]
    o_ref[0, 0] = y.astype(o_ref.dtype)


def compress(x_raw, pe, w1, w2, gain, norm_n, name):
    b, g, s, dp = x_raw.shape
    d = pe.shape[1]
    nch = s // CMP_STRIDE
    half = CMP_STRIDE * dp
    chunks = x_raw.reshape(b, g, nch, half)
    xb = jnp.concatenate([chunks[:, :, 1:], jnp.zeros((b, g, 1, half), F32)], axis=2)
    pe_p = jnp.pad(pe, ((0, 0), (0, dp - d)))
    pea = pe_p[:CMP_STRIDE].reshape(1, half)
    peb = pe_p[CMP_STRIDE:].reshape(1, half)
    w1p = jnp.pad(w1.reshape(CMP_BLOCK, d, d), ((0, 0), (0, dp - d), (0, dp - d))).astype(BF16)
    w1a = w1p[:CMP_STRIDE].reshape(half, dp)
    w1b = w1p[CMP_STRIDE:].reshape(half, dp)
    w2p = jnp.pad(w2, ((0, dp - d), (0, dp - d))).astype(BF16)
    gp = jnp.pad(gain, (0, dp - d)).reshape(1, dp).astype(F32)
    full = lambda shape: pl.BlockSpec(shape, lambda bi, gi: (0,) * len(shape))
    return pl.pallas_call(
        functools.partial(_compress_kernel, norm_n=norm_n),
        grid=(b, g),
        in_specs=[
            pl.BlockSpec((1, 1, nch, half), lambda bi, gi: (bi, gi, 0, 0)),
            pl.BlockSpec((1, 1, nch, half), lambda bi, gi: (bi, gi, 0, 0)),
            full((1, half)), full((1, half)), full((half, dp)), full((half, dp)), full((dp, dp)), full((1, dp)),
        ],
        out_specs=pl.BlockSpec((1, 1, nch, dp), lambda bi, gi: (bi, gi, 0, 0)),
        out_shape=jax.ShapeDtypeStruct((b, g, nch, dp), BF16),
        compiler_params=_cparams(("parallel", "arbitrary")),
        name=name,
    )(chunks, xb, pea, peb, w1a, w1b, w2p, gp)


def _attn_init(m_scr, l_scr, acc_scr):
    m_scr[...] = jnp.full(m_scr.shape, M_INIT, F32)
    l_scr[...] = jnp.zeros(l_scr.shape, F32)
    acc_scr[...] = jnp.zeros(acc_scr.shape, F32)


def _attn_step(q, k, v, add, m_scr, l_scr, acc_scr):
    s = _dot_nt(q, k)
    if add is not None:
        s = s + add
    m_prev = m_scr[...]
    m_new = jnp.maximum(m_prev, jnp.max(s, axis=1, keepdims=True))
    alpha = jnp.exp(m_prev - m_new)
    p = jnp.exp(s - m_new)
    l_scr[...] = alpha * l_scr[...] + jnp.sum(p, axis=1, keepdims=True)
    acc_scr[...] = alpha * acc_scr[...] + _dot(p.astype(v.dtype), v)
    m_scr[...] = m_new


def _tile_ds(idx, size):
    return pl.ds(pl.multiple_of(idx * size, size), size)


def _nsa_attn_kernel(q_ref, kc_ref, vc_ref, ks_ref, vs_ref, kw_ref, vw_ref, ac_ref, as_ref, aw_ref,
                     bc_ref, t0_ref, t1_ref, cf_ref, ovl_ref, e_ref, o_ref,
                     m_scr, l_scr, acc_scr, oc_scr, mask_scr, *, heads, n_cmp_pad, n_sel, k_sel):
    i = pl.program_id(2)
    row = i * TQ + lax.broadcasted_iota(jnp.int32, (TQ, 1), 0)
    state = (m_scr, l_scr, acc_scr)

    valid_c = (row >= CMP_BLOCK - 1).astype(F32)
    kc = kc_ref[0, 0]
    vc = vc_ref[0, 0]

    def cmp_head(r, psum):
        s = _dot_nt(q_ref[0, r], kc) + bc_ref[r]
        p = jnp.exp(s - jnp.max(s, axis=1, keepdims=True))
        pn = p * (valid_c / jnp.sum(p, axis=1, keepdims=True))
        oc_scr[r] = _dot(pn.astype(BF16), vc) * ac_ref[r].astype(F32)
        return psum + pn

    psum = lax.fori_loop(0, heads, cmp_head, jnp.zeros((TQ, n_cmp_pad), F32))

    ovl = ovl_ref[...]
    p_hi = psum.astype(BF16)
    rem = psum - p_hi.astype(F32)
    p_mid = rem.astype(BF16)
    p_lo = (rem - p_mid.astype(F32)).astype(BF16)
    imp = _dot(p_hi, ovl) + _dot(p_mid, ovl) + _dot(p_lo, ovl)

    lane = lax.broadcasted_iota(jnp.int32, (TQ, V7X_LANES), 1)
    lane_f = lane.astype(F32)
    cur = jnp.right_shift(row, int(math.log2(SEL_BLOCK)))
    forced = (lane == 0) | (lane == cur) | (lane == cur - 1)
    future = lane * SEL_BLOCK > row
    val = jnp.where(forced, 1e30, jnp.where(future, -1.0, imp))
    val = jnp.where(lane < n_sel, val, -2.0)

    def pick(_, c):
        v, sel = c
        top = jnp.max(v, axis=1, keepdims=True)
        first = jnp.min(jnp.where(v == top, lane_f, float(V7X_LANES)), axis=1, keepdims=True)
        hit = lane_f == first
        return jnp.where(hit, -3.0, v), jnp.where(hit, 1.0, sel)

    _, sel = lax.fori_loop(0, k_sel, pick, (val, jnp.zeros((TQ, V7X_LANES), F32)))
    sel_b = sel.astype(BF16)

    def build_mask(c, _):
        mask_scr[c] = (_dot(sel_b, e_ref[c]) - 1.0) * (-NEG)
        return 0

    lax.fori_loop(0, i + 1, build_mask, 0)

    n_far = jnp.maximum(i - 1, 0) // 2
    has_odd_far = (i >= 2) & (i % 2 == 0)
    ri = lax.broadcasted_iota(jnp.int32, (TQ, TQ), 0)
    ci = lax.broadcasted_iota(jnp.int32, (TQ, TQ), 1)
    half_t = TQ // 2

    def head(r, _):
        q = q_ref[0, r]
        cf = cf_ref[r][:, :1]
        t0 = t0_ref[r]
        t1 = t1_ref[r]
        cfb = jnp.broadcast_to(cf, (half_t, half_t))
        negb = jnp.full((half_t, half_t), NEG, F32)
        prev_bias = jnp.concatenate([jnp.concatenate([cfb, t1], axis=1),
                                     jnp.concatenate([cfb, cfb], axis=1)], axis=0)
        diag_bias = jnp.concatenate([jnp.concatenate([t0, negb], axis=1),
                                     jnp.concatenate([t1, t0], axis=1)], axis=0)

        _attn_init(*state)

        def far(c, _):
            ds = _tile_ds(c, 2 * TQ)
            add = cf + jnp.concatenate([mask_scr[2 * c], mask_scr[2 * c + 1]], axis=1)
            _attn_step(q, ks_ref[0, 0, ds, :], vs_ref[0, 0, ds, :], add, *state)
            return 0

        lax.fori_loop(0, n_far, far, 0)

        @pl.when(has_odd_far)
        def _():
            ds = _tile_ds(i - 2, TQ)
            _attn_step(q, ks_ref[0, 0, ds, :], vs_ref[0, 0, ds, :], cf + mask_scr[i - 2], *state)

        @pl.when(i >= 1)
        def _():
            ds = _tile_ds(i - 1, TQ)
            _attn_step(q, ks_ref[0, 0, ds, :], vs_ref[0, 0, ds, :], prev_bias + mask_scr[i - 1], *state)

        ds_diag = _tile_ds(i, TQ)
        _attn_step(q, ks_ref[0, 0, ds_diag, :], vs_ref[0, 0, ds_diag, :], diag_bias + mask_scr[i], *state)
        oc_scr[r] = oc_scr[r] + (acc_scr[...] / l_scr[...]) * as_ref[r].astype(F32)

        _attn_init(*state)

        @pl.when(i >= 2)
        def _():
            ds = _tile_ds(i - 2, TQ)
            _attn_step(q, kw_ref[0, 0, ds, :], vw_ref[0, 0, ds, :], jnp.where(ci > ri, cf, NEG), *state)

        @pl.when(i >= 1)
        def _():
            ds = _tile_ds(i - 1, TQ)
            _attn_step(q, kw_ref[0, 0, ds, :], vw_ref[0, 0, ds, :], prev_bias, *state)

        _attn_step(q, kw_ref[0, 0, ds_diag, :], vw_ref[0, 0, ds_diag, :], diag_bias, *state)
        o = oc_scr[r] + (acc_scr[...] / l_scr[...]) * aw_ref[r].astype(F32)
        o_ref[r] = o.astype(o_ref.dtype)
        return 0

    lax.fori_loop(0, heads, head, 0)


def _t5_bucket(dist):
    n = jnp.maximum(dist, 0)
    max_exact = REL_BUCKETS // 2
    scaled = (jnp.log(jnp.maximum(n, max_exact).astype(F32) / max_exact)
              / math.log(REL_MAX_DISTANCE / max_exact))
    large = jnp.minimum(max_exact + (scaled * (REL_BUCKETS - max_exact)).astype(jnp.int32), REL_BUCKETS - 1)
    return jnp.where(n < max_exact, n, large)


def _bias_table(rel_bias, dist):
    vals = jnp.moveaxis(rel_bias[_t5_bucket(dist)], -1, 0)
    return jnp.where(dist >= 0, vals, NEG).astype(F32)


def nsa_attention(q, kc, vc, ks, vs, kw, vw, gates, rel_bias):
    b, h, s, _ = q.shape
    g = kc.shape[1]
    heads = h // g
    n_cmp_pad = kc.shape[2]
    n_sel = s // SEL_BLOCK
    k_sel = min(SEL_TOPK, n_sel)
    nq = s // TQ
    half_t = TQ // 2
    assert s % (2 * TQ) == 0 and WINDOW == 2 * TQ and n_sel <= V7X_LANES and TQ % SEL_BLOCK == 0
    assert half_t >= REL_MAX_DISTANCE
    assert n_cmp_pad == s // CMP_STRIDE and n_cmp_pad % V7X_LANES == 0

    tok = jnp.arange(s)
    cmp_end = CMP_STRIDE * jnp.arange(n_cmp_pad) + CMP_BLOCK - 1
    bias_c = _bias_table(rel_bias, tok[:, None] - cmp_end[None, :])
    ij = jnp.arange(half_t)
    t0 = _bias_table(rel_bias, ij[:, None] - ij[None, :])
    t1 = _bias_table(rel_bias, half_t + ij[:, None] - ij[None, :])
    cf = jnp.broadcast_to(rel_bias[REL_BUCKETS - 1][:, None, None], (h, 1, V7X_LANES)).astype(F32)

    n_cmp = (s - CMP_BLOCK) // CMP_STRIDE + 1
    cmp_start = CMP_STRIDE * np.arange(n_cmp_pad)
    sel_start = SEL_BLOCK * np.arange(V7X_LANES)
    ovl = ((cmp_start[:, None] < (sel_start + SEL_BLOCK)[None, :])
           & ((cmp_start + CMP_BLOCK)[:, None] > sel_start[None, :])
           & (np.arange(n_cmp_pad) < n_cmp)[:, None] & (np.arange(V7X_LANES) < n_sel)[None, :])
    ovl = jnp.asarray(ovl, BF16)
    key_blk = (np.arange(s) // SEL_BLOCK).reshape(nq, 1, TQ)
    expand = jnp.asarray(key_blk == np.arange(V7X_LANES)[None, :, None], BF16)

    kv_spec = lambda n, d: pl.BlockSpec((1, 1, n, d), lambda bi, gi, i: (bi, gi, 0, 0))
    gate_spec = lambda br: pl.BlockSpec((heads, TQ, DV), lambda bi, gi, i, br=br: (br * g + gi, bi * nq + i, 0))
    tbl_spec = lambda shape: pl.BlockSpec((heads,) + shape, lambda bi, gi, i: (gi, 0, 0))
    kern = functools.partial(_nsa_attn_kernel, heads=heads, n_cmp_pad=n_cmp_pad, n_sel=n_sel, k_sel=k_sel)
    return pl.pallas_call(
        kern,
        grid=(b, g, nq),
        in_specs=[
            pl.BlockSpec((1, heads, TQ, DKP), lambda bi, gi, i: (bi, gi, i, 0)),
            kv_spec(n_cmp_pad, DKP), kv_spec(n_cmp_pad, DV),
            kv_spec(s, DKP), kv_spec(s, DV), kv_spec(s, DKP), kv_spec(s, DV),
            gate_spec(0), gate_spec(1), gate_spec(2),
            pl.BlockSpec((heads, TQ, n_cmp_pad), lambda bi, gi, i: (gi, i, 0)),
            tbl_spec((half_t, half_t)), tbl_spec((half_t, half_t)), tbl_spec((1, V7X_LANES)),
            pl.BlockSpec((n_cmp_pad, V7X_LANES), lambda bi, gi, i: (0, 0)),
            pl.BlockSpec((nq, V7X_LANES, TQ), lambda bi, gi, i: (0, 0, 0)),
        ],
        out_specs=pl.BlockSpec((heads, TQ, DV), lambda bi, gi, i: (gi, bi * nq + i, 0)),
        out_shape=jax.ShapeDtypeStruct((h, b * s, DV), BF16),
        scratch_shapes=[
            pltpu.VMEM((TQ, 1), F32), pltpu.VMEM((TQ, 1), F32), pltpu.VMEM((TQ, DV), F32),
            pltpu.VMEM((heads, TQ, DV), F32), pltpu.VMEM((nq, TQ, TQ), F32),
        ],
        compiler_params=_cparams(("parallel", "parallel", "arbitrary")),
        name="nsa_attention",
    )(q, kc, vc, ks, vs, kw, vw, gates, gates, gates, bias_c, t0, t1, cf, ovl, expand)


def _mla_attn_kernel(q_ref, k_ref, v_ref, z_ref, o_ref, m_scr, l_scr, acc_scr, *, heads):
    i = pl.program_id(2)
    state = (m_scr, l_scr, acc_scr)
    ri = lax.broadcasted_iota(jnp.int32, (TQ, TQ), 0)
    ci = lax.broadcasted_iota(jnp.int32, (TQ, TQ), 1)
    causal = jnp.where(ci <= ri, 0.0, NEG)

    def head(r, _):
        q = q_ref[0, r]
        _attn_init(*state)

        def far(c, _):
            ds = _tile_ds(c, 2 * TQ)
            _attn_step(q, k_ref[0, r, ds, :], v_ref[0, r, ds, :], None, *state)
            return 0

        lax.fori_loop(0, i // 2, far, 0)

        @pl.when(i % 2 == 1)
        def _():
            ds = _tile_ds(i - 1, TQ)
            _attn_step(q, k_ref[0, r, ds, :], v_ref[0, r, ds, :], None, *state)

        ds = _tile_ds(i, TQ)
        _attn_step(q, k_ref[0, r, ds, :], v_ref[0, r, ds, :], causal, *state)
        o_ref[r] = ((acc_scr[...] / l_scr[...]) * z_ref[r].astype(F32)).astype(o_ref.dtype)
        return 0

    lax.fori_loop(0, heads, head, 0)


def mla_attention(q, k, v, zg):
    b, h, s, _ = q.shape
    hb = MLA_HEADS_PER_STEP
    nq = s // TQ
    assert s % (2 * TQ) == 0 and h % hb == 0
    return pl.pallas_call(
        functools.partial(_mla_attn_kernel, heads=hb),
        grid=(b, h // hb, nq),
        in_specs=[
            pl.BlockSpec((1, hb, TQ, DKP), lambda bi, hi, i: (bi, hi, i, 0)),
            pl.BlockSpec((1, hb, s, DKP), lambda bi, hi, i: (bi, hi, 0, 0)),
            pl.BlockSpec((1, hb, s, DV), lambda bi, hi, i: (bi, hi, 0, 0)),
            pl.BlockSpec((hb, TQ, DV), lambda bi, hi, i: (hi, bi * nq + i, 0)),
        ],
        out_specs=pl.BlockSpec((hb, TQ, DV), lambda bi, hi, i: (hi, bi * nq + i, 0)),
        out_shape=jax.ShapeDtypeStruct((h, b * s, DV), BF16),
        scratch_shapes=[pltpu.VMEM((TQ, 1), F32), pltpu.VMEM((TQ, 1), F32), pltpu.VMEM((TQ, DV), F32)],
        compiler_params=_cparams(("parallel", "parallel", "arbitrary")),
        name="mla_attention",
    )(q, k, v, zg)


def _pad_heads(w, n_heads, d):
    k = w.shape[0]
    return jnp.pad(w.reshape(k, n_heads, d), ((0, 0), (0, 0), (0, DKP - d))).reshape(k, n_heads * DKP).astype(BF16)


def _head_major_out(b, s, h, tm, hb, width, dtype):
    nt = s // tm
    spec = pl.BlockSpec((1, hb, tm, width), lambda i, j: (i // nt, j, i % nt, 0))
    return jax.ShapeDtypeStruct((b, h, s, width), dtype), spec


def _heads_to_rows(o):
    h, t, d = o.shape
    return o.transpose(1, 0, 2).reshape(t, h * d)


def nsa_layer(x2, b, s, norm_w, rel_bias, w_in, q_norm, k_norm, pe_k, w1_k, w2_k, pe_v, w1_v, w2_v, w_out):
    t, d_model = x2.shape
    h, g, dk, dv = NSA_HEADS, NSA_KV_GROUPS, NSA_QK_DIM, NSA_V_DIM
    qw, kvw, zw = h * dk, g * (dk + dv), h * dv
    tm = _pick(s, 1024, 8)
    hn = rmsnorm_rows(x2, norm_w, d_model, BF16, "nsa_prenorm")

    w_q = _pad_heads(w_in[:, :qw], h, dk)
    hb = _pick(h, 4)
    q_shape, q_spec = _head_major_out(b, s, h, tm, hb, DKP, BF16)
    gain_q = jnp.pad(q_norm, (0, DKP - dk)).reshape(1, DKP)
    q = matmul(hn, w_q, functools.partial(_ep_q_heads, hb=hb, n_true=dk, scale=dk ** -0.5, rope=False),
               tm=tm, tn=hb * DKP, out_shapes=q_shape, out_specs=q_spec,
               extra=(gain_q,), extra_specs=(pl.BlockSpec((1, DKP), lambda i, j: (0, 0)),), name="nsa_q_proj")

    kv_cols = []
    for br in range(3):
        w_br = w_in[:, qw + br * kvw: qw + (br + 1) * kvw]
        kv_cols += [_pad_heads(w_br[:, :g * dk], g, dk), w_br[:, g * dk:].astype(BF16)]
    w_kv = jnp.concatenate(kv_cols, axis=1)
    kv_width = g * (DKP + dv)
    tn_kv = _pick(3 * kv_width, 768, V7X_LANES)
    kv = matmul(hn, w_kv, _ep_store, tm=tm, tn=tn_kv,
                out_shapes=jax.ShapeDtypeStruct((t, 3 * kv_width), F32),
                out_specs=pl.BlockSpec((tm, tn_kv), lambda i, j: (i, j)), name="nsa_kv_proj")

    def kv_branch(br):
        blk = kv[:, br * kv_width:(br + 1) * kv_width]
        k = blk[:, :g * DKP].reshape(b, s, g, DKP).transpose(0, 2, 1, 3)
        v = blk[:, g * DKP:].reshape(b, s, g, dv).transpose(0, 2, 1, 3)
        return k, v

    kc_raw, vc_raw = kv_branch(0)
    ks_raw, vs_raw = kv_branch(1)
    kw_raw, vw_raw = kv_branch(2)
    kc = compress(kc_raw, pe_k, w1_k, w2_k, k_norm[0], dk, "nsa_compress_k")
    vc = compress(vc_raw, pe_v, w1_v, w2_v, jnp.ones((dv,), F32), 0, "nsa_compress_v")
    pad_gain = lambda gn: jnp.pad(gn, (0, DKP - dk))
    ks = rmsnorm_rows(ks_raw.reshape(-1, DKP), pad_gain(k_norm[1]), dk, BF16, "nsa_knorm_sel").reshape(b, g, s, DKP)
    kw = rmsnorm_rows(kw_raw.reshape(-1, DKP), pad_gain(k_norm[2]), dk, BF16, "nsa_knorm_win").reshape(b, g, s, DKP)
    vs = vs_raw.astype(BF16)
    vw = vw_raw.astype(BF16)

    z0 = qw + 3 * kvw
    w_g = w_in[:, z0 + 3 * zw:].astype(BF16)
    sg = matmul(hn, w_g, _ep_sigmoid, tm=tm, tn=3 * h,
                out_shapes=jax.ShapeDtypeStruct((t, 3 * h), F32),
                out_specs=pl.BlockSpec((tm, 3 * h), lambda i, j: (i, j)), name="nsa_gate_proj")
    hbz = _pick(h, 8)
    n_zt = 3 * h // hbz
    sg_tiles = jnp.pad(sg.reshape(t, n_zt, hbz).transpose(1, 0, 2), ((0, 0), (0, 0), (0, V7X_LANES - hbz)))
    w_z = w_in[:, z0:z0 + 3 * zw].astype(BF16)
    gates = matmul(hn, w_z, functools.partial(_ep_gate_heads, hb=hbz, gated=True), tm=tm, tn=hbz * dv,
                   out_shapes=jax.ShapeDtypeStruct((3 * h, t, dv), BF16),
                   out_specs=pl.BlockSpec((hbz, tm, dv), lambda i, j: (j, i, 0)),
                   extra=(sg_tiles,), extra_specs=(pl.BlockSpec((1, tm, V7X_LANES), lambda i, j: (j, i, 0)),),
                   name="nsa_z_proj")

    o = nsa_attention(q, kc, vc, ks, vs, kw, vw, gates, rel_bias)
    tmo = _pick(t, 512, 8)
    tno = _pick(d_model, 512, V7X_LANES)
    return matmul(_heads_to_rows(o), w_out.astype(BF16), _ep_residual, tm=tmo, tn=tno,
                  out_shapes=jax.ShapeDtypeStruct((t, d_model), F32),
                  out_specs=pl.BlockSpec((tmo, tno), lambda i, j: (i, j)),
                  extra=(x2,), extra_specs=(pl.BlockSpec((tmo, tno), lambda i, j: (i, j)),), name="nsa_out_proj")


def mla_layer(x2, b, s, positions, norm_w, w_in, q_a_norm, w_q_b, kv_a_norm, w_kv_b, q_norm, k_norm, w_out):
    t, d_model = x2.shape
    h = MLA_HEADS
    dqk = MLA_NOPE_DIM + MLA_ROPE_DIM
    tm = _pick(s, 1024, 8)
    hn = rmsnorm_rows(x2, norm_w, d_model, BF16, "mla_prenorm")

    half = MLA_ROPE_DIM // 2
    inv_freq = ROPE_BASE ** (-jnp.arange(half, dtype=F32) / half)
    ang = positions.astype(F32).reshape(t, 1) * inv_freq[None, :]
    zeros = jnp.zeros((t, V7X_LANES - MLA_ROPE_DIM), F32)
    cos2 = jnp.concatenate([jnp.cos(ang), jnp.cos(ang), zeros], axis=1)
    sin2 = jnp.concatenate([-jnp.sin(ang), jnp.sin(ang), zeros], axis=1)
    rope_spec = pl.BlockSpec((tm, V7X_LANES), lambda i, j: (i, 0))
    row_spec = lambda w: pl.BlockSpec((1, w), lambda i, j: (0, 0))

    c0, c1, c2 = MLA_Q_LORA, MLA_Q_LORA + MLA_KV_LORA, MLA_Q_LORA + MLA_KV_LORA + MLA_ROPE_DIM
    w_lat = jnp.concatenate([w_in[:, :c1], jnp.pad(w_in[:, c1:c2], ((0, 0), (0, V7X_LANES - MLA_ROPE_DIM)))],
                            axis=1).astype(BF16)
    lat_w = c1 + V7X_LANES
    tn_lat = _pick(lat_w, 1024, V7X_LANES)
    lat = matmul(hn, w_lat, _ep_store, tm=tm, tn=tn_lat,
                 out_shapes=jax.ShapeDtypeStruct((t, lat_w), F32),
                 out_specs=pl.BlockSpec((tm, tn_lat), lambda i, j: (i, j)), name="mla_latent_proj")
    cq = rmsnorm_rows(lat[:, :c0], q_a_norm, MLA_Q_LORA, BF16, "mla_q_a_norm")
    ckv = rmsnorm_rows(lat[:, c0:c1], kv_a_norm, MLA_KV_LORA, BF16, "mla_kv_a_norm")
    kpe = lat[:, c1:]

    hbz = _pick(h, 8)
    zg = matmul(hn, w_in[:, c2:].astype(BF16), functools.partial(_ep_gate_heads, hb=hbz, gated=False),
                tm=tm, tn=hbz * DV, out_shapes=jax.ShapeDtypeStruct((h, t, DV), BF16),
                out_specs=pl.BlockSpec((hbz, tm, DV), lambda i, j: (j, i, 0)), name="mla_z_proj")

    hb = _pick(h, 4)
    q_shape, q_spec = _head_major_out(b, s, h, tm, hb, DKP, BF16)
    gain_q = jnp.pad(q_norm, (0, DKP - dqk)).reshape(1, DKP)
    q = matmul(cq, _pad_heads(w_q_b, h, dqk),
               functools.partial(_ep_q_heads, hb=hb, n_true=dqk, scale=dqk ** -0.5, rope=True),
               tm=tm, tn=hb * DKP, out_shapes=q_shape, out_specs=q_spec,
               extra=(gain_q, cos2, sin2), extra_specs=(row_spec(DKP), rope_spec, rope_spec), name="mla_q_proj")

    k_shape, k_spec = _head_major_out(b, s, h, tm, hb, DKP, BF16)
    v_shape, v_spec = _head_major_out(b, s, h, tm, hb, DV, BF16)
    gain_n = k_norm[:MLA_NOPE_DIM].reshape(1, MLA_NOPE_DIM)
    gain_r = jnp.pad(k_norm[MLA_NOPE_DIM:], (0, V7X_LANES - MLA_ROPE_DIM)).reshape(1, V7X_LANES)
    k, v = matmul(ckv, w_kv_b.astype(BF16), functools.partial(_ep_mla_kv, hb=hb),
                  tm=tm, tn=hb * DKP, out_shapes=(k_shape, v_shape), out_specs=(k_spec, v_spec),
                  extra=(kpe, gain_n, gain_r, cos2, sin2),
                  extra_specs=(rope_spec, row_spec(MLA_NOPE_DIM), row_spec(V7X_LANES), rope_spec, rope_spec),
                  name="mla_kv_proj")

    o = mla_attention(q, k, v, zg)
    tmo = _pick(t, 512, 8)
    tno = _pick(d_model, 512, V7X_LANES)
    return matmul(_heads_to_rows(o), w_out.astype(BF16), _ep_residual, tm=tmo, tn=tno,
                  out_shapes=jax.ShapeDtypeStruct((t, d_model), F32),
                  out_specs=pl.BlockSpec((tmo, tno), lambda i, j: (i, j)),
                  extra=(x2,), extra_specs=(pl.BlockSpec((tmo, tno), lambda i, j: (i, j)),), name="mla_out_proj")


def kernel(x, positions, norm_w, rel_bias, nsa_w_in, nsa_q_norm, nsa_k_norm, nsa_cmp_pe_k, nsa_cmp_w1_k,
           nsa_cmp_w2_k, nsa_cmp_pe_v, nsa_cmp_w1_v, nsa_cmp_w2_v, nsa_w_out, mla_w_in, mla_q_a_norm,
           mla_w_q_b, mla_kv_a_norm, mla_w_kv_b, mla_q_norm, mla_k_norm, mla_w_out):
    b, s, d_model = x.shape
    x2 = x.reshape(b * s, d_model)
    depth = norm_w.shape[0]
    for layer in range(depth):
        j = layer // 2
        if layer % 2 == 0:
            x2 = nsa_layer(x2, b, s, norm_w[layer], rel_bias, nsa_w_in[j], nsa_q_norm[j], nsa_k_norm[j],
                           nsa_cmp_pe_k[j], nsa_cmp_w1_k[j], nsa_cmp_w2_k[j], nsa_cmp_pe_v[j], nsa_cmp_w1_v[j],
                           nsa_cmp_w2_v[j], nsa_w_out[j])
        else:
            x2 = mla_layer(x2, b, s, positions, norm_w[layer], mla_w_in[j], mla_q_a_norm[j], mla_w_q_b[j],
                           mla_kv_a_norm[j], mla_w_kv_b[j], mla_q_norm[j], mla_k_norm[j], mla_w_out[j])
    return x2.reshape(b, s, d_model)
```

```python
import functools
import math

import numpy as np
import jax
import jax.numpy as jnp
from jax import lax
from jax.experimental import pallas as pl
from jax.experimental.pallas import tpu as pltpu

F32 = jnp.float32
BF16 = jnp.bfloat16

EPS = 1e-6
REL_BUCKETS = 32
REL_MAX_DISTANCE = 128
NSA_HEADS = 64
NSA_KV_GROUPS = 4
NSA_QK_DIM = 192
NSA_V_DIM = 128
CMP_BLOCK = 32
CMP_STRIDE = 16
SEL_BLOCK = 64
SEL_TOPK = 16
WINDOW = 512
MLA_HEADS = 64
MLA_Q_LORA = 1536
MLA_KV_LORA = 512
MLA_NOPE_DIM = 128
MLA_ROPE_DIM = 64
MLA_V_DIM = 128
ROPE_BASE = 10000.0

V7X_LANES = 128
V7X_VMEM_BYTES = 64 * 1024 * 1024
VMEM_LIMIT_BYTES = V7X_VMEM_BYTES - 8 * 1024 * 1024

DKP = 2 * V7X_LANES
DV = V7X_LANES
TQ = 256
NEG = -1e30
M_INIT = -1e20
MLA_HEADS_PER_STEP = 4
NSA_HEADS_PER_ITER = 4


def _cparams(sem):
    return pltpu.CompilerParams(dimension_semantics=sem, vmem_limit_bytes=VMEM_LIMIT_BYTES)


def _dot(a, b):
    return jnp.dot(a, b, preferred_element_type=F32)


def _sigmoid(x):
    return 1.0 / (1.0 + jnp.exp(-x))


def _silu(x):
    return x * _sigmoid(x)


def _pick(n, target, unit=1):
    best = unit
    c = unit
    while c <= min(n, max(target, unit)):
        if n % c == 0:
            best = c
        c += unit
    return best


def _rmsnorm_kernel(x_ref, g_ref, o_ref, *, inv_n):
    x = x_ref[...].astype(F32)
    ms = jnp.sum(x * x, axis=-1, keepdims=True) * inv_n
    o_ref[...] = (x * lax.rsqrt(ms + EPS) * g_ref[...]).astype(o_ref.dtype)


def rmsnorm_rows(x, gain, n_true, out_dtype, name):
    m, d = x.shape
    tm = _pick(m, max(8, (2 * 1024 * 1024) // (4 * d)), 8)
    return pl.pallas_call(
        functools.partial(_rmsnorm_kernel, inv_n=1.0 / n_true),
        grid=(m // tm,),
        in_specs=[pl.BlockSpec((tm, d), lambda i: (i, 0)), pl.BlockSpec((1, d), lambda i: (0, 0))],
        out_specs=pl.BlockSpec((tm, d), lambda i: (i, 0)),
        out_shape=jax.ShapeDtypeStruct((m, d), out_dtype),
        compiler_params=_cparams(("parallel",)),
        name=name,
    )(x, gain.reshape(1, d).astype(F32))


def _matmul_kernel(*refs, epilogue, n_in):
    a_ref, w_ref = refs[0], refs[1]
    acc = _dot(a_ref[...], w_ref[...])
    epilogue(acc, refs[2:n_in], refs[n_in:])


def matmul(a, w, epilogue, *, tm, tn, out_shapes, out_specs, extra=(), extra_specs=(), name):
    m, k = a.shape
    n = w.shape[1]
    assert m % tm == 0 and n % tn == 0, (m, tm, n, tn)
    in_specs = [pl.BlockSpec((tm, k), lambda i, j: (i, 0)), pl.BlockSpec((k, tn), lambda i, j: (0, j))]
    in_specs += list(extra_specs)
    return pl.pallas_call(
        functools.partial(_matmul_kernel, epilogue=epilogue, n_in=2 + len(extra)),
        grid=(m // tm, n // tn),
        in_specs=in_specs,
        out_specs=out_specs,
        out_shape=out_shapes,
        compiler_params=_cparams(("parallel", "arbitrary")),
        name=name,
    )(a, w, *extra)


def _ep_store(acc, extra, outs):
    outs[0][...] = acc.astype(outs[0].dtype)


def _ep_sigmoid(acc, extra, outs):
    outs[0][...] = _sigmoid(acc)


def _ep_residual(acc, extra, outs):
    outs[0][...] = extra[0][...] + acc


def _rope_lanes(x, cos2, sin2):
    half = MLA_ROPE_DIM // 2
    lane = lax.broadcasted_iota(jnp.int32, x.shape, 1)
    up = pltpu.roll(x, V7X_LANES - half, axis=1)
    down = pltpu.roll(x, half, axis=1)
    rot = jnp.where(lane < half, up, down)
    return x * cos2 + rot * sin2


def _store_tiles_t(ref, x):
    for u in range(x.shape[0] // TQ):
        ref[u] = x[u * TQ:(u + 1) * TQ, :].T.astype(ref.dtype)


def _ep_q_heads(acc, extra, outs, *, hb, n_true, scale, rope):
    gain = extra[0][...]
    for r in range(hb):
        y = acc[:, r * DKP:(r + 1) * DKP]
        ms = jnp.sum(y * y, axis=-1, keepdims=True) * (1.0 / n_true)
        yn = y * lax.rsqrt(ms + EPS) * gain
        if rope:
            yr = _rope_lanes(yn[:, V7X_LANES:], extra[1][...], extra[2][...])
            yn = jnp.concatenate([yn[:, :V7X_LANES], yr], axis=1)
        _store_tiles_t(outs[0].at[0, r], yn * scale)


def _ep_gate_heads(acc, extra, outs, *, hb, gated):
    for r in range(hb):
        z = acc[:, r * DV:(r + 1) * DV]
        a = _silu(z)
        if gated:
            a = a * extra[0][0][:, r:r + 1]
        _store_tiles_t(outs[0].at[r], a)


def _ep_mla_kv(acc, extra, outs, *, hb):
    kpe = extra[0][...]
    gain_n = extra[1][...]
    gain_r = extra[2][...]
    ss_pe = jnp.sum(kpe * kpe, axis=-1, keepdims=True)
    kr = _rope_lanes(kpe * gain_r, extra[3][...], extra[4][...])
    inv_n = 1.0 / (MLA_NOPE_DIM + MLA_ROPE_DIM)
    for r in range(hb):
        kn = acc[:, r * DKP:r * DKP + V7X_LANES]
        v = acc[:, r * DKP + V7X_LANES:(r + 1) * DKP]
        rs = lax.rsqrt((jnp.sum(kn * kn, axis=-1, keepdims=True) + ss_pe) * inv_n + EPS)
        outs[0][0, r] = jnp.concatenate([kn * rs * gain_n, kr * rs], axis=1).astype(outs[0].dtype)
        _store_tiles_t(outs[1].at[0, r], v)


def _compress_kernel(xa_ref, xb_ref, pea_ref, peb_ref, w1a_ref, w1b_ref, w2_ref, g_ref, o_ref, *, norm_n):
    xa = (xa_ref[0, 0] + pea_ref[...]).astype(BF16)
    xb = (xb_ref[0, 0] + peb_ref[...]).astype(BF16)
    pre = _dot(xa, w1a_ref[...]) + _dot(xb, w1b_ref[...])
    y = _dot(_silu(pre).astype(BF16), w2_ref[...])
    if norm_n:
        ms = jnp.sum(y * y, axis=-1, keepdims=True) * (1.0 / norm_n)
        y = y * lax.rsqrt(ms + EPS) * g_ref[...]
    o_ref[0, 0] = y.astype(o_ref.dtype)


def compress(x_raw, pe, w1, w2, gain, norm_n, name):
    b, g, s, dp = x_raw.shape
    d = pe.shape[1]
    nch = s // CMP_STRIDE
    half = CMP_STRIDE * dp
    chunks = x_raw.reshape(b, g, nch, half)
    xb = jnp.concatenate([chunks[:, :, 1:], jnp.zeros((b, g, 1, half), F32)], axis=2)
    pe_p = jnp.pad(pe, ((0, 0), (0, dp - d)))
    pea = pe_p[:CMP_STRIDE].reshape(1, half)
    peb = pe_p[CMP_STRIDE:].reshape(1, half)
    w1p = jnp.pad(w1.reshape(CMP_BLOCK, d, d), ((0, 0), (0, dp - d), (0, dp - d))).astype(BF16)
    w1a = w1p[:CMP_STRIDE].reshape(half, dp)
    w1b = w1p[CMP_STRIDE:].reshape(half, dp)
    w2p = jnp.pad(w2, ((0, dp - d), (0, dp - d))).astype(BF16)
    gp = jnp.pad(gain, (0, dp - d)).reshape(1, dp).astype(F32)
    full = lambda shape: pl.BlockSpec(shape, lambda bi, gi: (0,) * len(shape))
    return pl.pallas_call(
        functools.partial(_compress_kernel, norm_n=norm_n),
        grid=(b, g),
        in_specs=[
            pl.BlockSpec((1, 1, nch, half), lambda bi, gi: (bi, gi, 0, 0)),
            pl.BlockSpec((1, 1, nch, half), lambda bi, gi: (bi, gi, 0, 0)),
            full((1, half)), full((1, half)), full((half, dp)), full((half, dp)), full((dp, dp)), full((1, dp)),
        ],
        out_specs=pl.BlockSpec((1, 1, nch, dp), lambda bi, gi: (bi, gi, 0, 0)),
        out_shape=jax.ShapeDtypeStruct((b, g, nch, dp), BF16),
        compiler_params=_cparams(("parallel", "arbitrary")),
        name=name,
    )(chunks, xb, pea, peb, w1a, w1b, w2p, gp)


def _attn_init(m_ref, l_ref, acc_ref):
    m_ref[...] = jnp.full(m_ref.shape, M_INIT, F32)
    l_ref[...] = jnp.zeros(l_ref.shape, F32)
    acc_ref[...] = jnp.zeros(acc_ref.shape, F32)


def _attn_step(qt, k, vts, add, m_ref, l_ref, acc_ref):
    _attn_update(_dot(k, qt), vts, add, m_ref, l_ref, acc_ref)


def _attn_update(st, vts, add, m_ref, l_ref, acc_ref):
    for j, vt in enumerate(vts):
        sj = st[j * TQ:(j + 1) * TQ]
        if add is not None:
            sj = sj + add[j * TQ:(j + 1) * TQ]
        m_prev = m_ref[...]
        m_new = jnp.maximum(m_prev, jnp.max(sj, axis=0, keepdims=True))
        alpha = jnp.exp(m_prev - m_new)
        p = jnp.exp(sj - m_new)
        l_ref[...] = alpha * l_ref[...] + jnp.sum(p, axis=0, keepdims=True)
        acc_ref[...] = alpha * acc_ref[...] + _dot(vt, p.astype(BF16))
        m_ref[...] = m_new


def _tile_ds(idx, size):
    return pl.ds(pl.multiple_of(idx * size, size), size)


def _nsa_attn_kernel(qt_ref, kc_ref, vct_ref, ks_ref, vst_ref, kw_ref, vwt_ref, ac_ref, as_ref, aw_ref,
                     cb_ref, t0_ref, t1_ref, ovl_ref, e_ref, o_ref,
                     m_scr, l_scr, acc_scr, oc_scr, mask_scr, cmpb_scr, *, heads, n_cmp_pad, n_sel, k_sel):
    nh = NSA_HEADS_PER_ITER
    i = pl.program_id(2)
    tok = i * TQ + lax.broadcasted_iota(jnp.int32, (1, TQ), 1)
    per_tile = TQ // CMP_STRIDE
    band = cb_ref.shape[1]

    valid_c = (tok >= CMP_BLOCK - 1).astype(F32)
    nrow = lax.broadcasted_iota(jnp.int32, (n_cmp_pad, TQ), 0)
    base = jnp.where(nrow >= (i + 1) * per_tile, NEG, 0.0)
    for u in range(nh):
        cmpb_scr[u] = base
    kc = kc_ref[0, 0]
    vct = vct_ref[0, 0]

    def cmp_group(gi, psum):
        rs = [gi * nh + u for u in range(nh)]
        for u, r in enumerate(rs):
            @pl.when(i > 0)
            def _():
                cmpb_scr[u, pl.ds(pl.multiple_of((i - 1) * per_tile, per_tile), band), :] = cb_ref[r]

            @pl.when(i == 0)
            def _():
                cmpb_scr[u, 0:per_tile, :] = cb_ref[r, per_tile:band, :]

        sts = [_dot(kc, qt_ref[0, r, 0]) + cmpb_scr[u] for u, r in enumerate(rs)]
        for u, r in enumerate(rs):
            p = jnp.exp(sts[u] - jnp.max(sts[u], axis=0, keepdims=True))
            pn = p * (valid_c / jnp.sum(p, axis=0, keepdims=True))
            oc_scr[r] = _dot(vct, pn.astype(BF16)) * ac_ref[r, 0].astype(F32)
            psum = psum + pn
        return psum

    psum = lax.fori_loop(0, heads // nh, cmp_group, jnp.zeros((n_cmp_pad, TQ), F32))

    ovl = ovl_ref[...]
    p_hi = psum.astype(BF16)
    rem = psum - p_hi.astype(F32)
    p_mid = rem.astype(BF16)
    p_lo = (rem - p_mid.astype(F32)).astype(BF16)
    imp = _dot(ovl, p_hi) + _dot(ovl, p_mid) + _dot(ovl, p_lo)

    blk = lax.broadcasted_iota(jnp.int32, (V7X_LANES, TQ), 0)
    blk_f = blk.astype(F32)
    cur = jnp.right_shift(tok, int(math.log2(SEL_BLOCK)))
    forced = (blk == 0) | (blk == cur) | (blk == cur - 1)
    future = blk * SEL_BLOCK > tok
    val = jnp.where(forced, 1e30, jnp.where(future, -1.0, imp))
    val = jnp.where(blk < n_sel, val, -2.0)

    def pick(_, c):
        v, sel = c
        top = jnp.max(v, axis=0, keepdims=True)
        first = jnp.min(jnp.where(v == top, blk_f, float(V7X_LANES)), axis=0, keepdims=True)
        hit = blk_f == first
        return jnp.where(hit, -3.0, v), jnp.where(hit, 1.0, sel)

    _, sel = lax.fori_loop(0, k_sel, pick, (val, jnp.zeros((V7X_LANES, TQ), F32)))
    sel_b = sel.astype(BF16)

    def build_mask(c, _):
        mask_scr[c] = (_dot(e_ref[c], sel_b) - 1.0) * (-NEG)
        return 0

    lax.fori_loop(0, i + 1, build_mask, 0)

    n_far = jnp.maximum(i - 1, 0) // 2
    kj = lax.broadcasted_iota(jnp.int32, (TQ, TQ), 0)
    qi = lax.broadcasted_iota(jnp.int32, (TQ, TQ), 1)
    half_t = TQ // 2
    zb = jnp.zeros((half_t, half_t), F32)
    negb = jnp.full((half_t, half_t), NEG, F32)

    def head_group(gi, _):
        rs = [gi * nh + u for u in range(nh)]
        sel_state = (m_scr.at[0], l_scr.at[0], acc_scr.at[0])
        win_state = (m_scr.at[1], l_scr.at[1], acc_scr.at[1])

        def scores(k_rows):
            return _dot(k_rows, jnp.concatenate([qt_ref[0, r, 0] for r in rs], axis=1))

        def update_all(st_all, vts, adds, state):
            for u in range(nh):
                _attn_update(st_all[:, u * TQ:(u + 1) * TQ], vts, adds[u], *(s.at[u] for s in state))

        _attn_init(*sel_state)
        _attn_init(*win_state)

        def far(c, _):
            st_all = scores(ks_ref[0, 0, _tile_ds(c, 2 * TQ), :])
            mask = jnp.concatenate([mask_scr[2 * c], mask_scr[2 * c + 1]], axis=0)
            update_all(st_all, (vst_ref[0, 0, 2 * c], vst_ref[0, 0, 2 * c + 1]), [mask] * nh, sel_state)
            return 0

        lax.fori_loop(0, n_far, far, 0)

        def near(n_s, n_w):
            t0s = [t0_ref[r] for r in rs]
            t1s = [t1_ref[r] for r in rs]
            prev = [jnp.concatenate([jnp.concatenate([zb, zb], axis=1),
                                     jnp.concatenate([t1, zb], axis=1)], axis=0) for t1 in t1s]
            diag = [jnp.concatenate([jnp.concatenate([t0, t1], axis=1),
                                     jnp.concatenate([negb, t0], axis=1)], axis=0) for t0, t1 in zip(t0s, t1s)]
            w0 = jnp.where(kj > qi, 0.0, NEG)
            st_s = scores(ks_ref[0, 0, pl.ds(pl.multiple_of((i - n_s + 1) * TQ, TQ), n_s * TQ), :])
            st_w = scores(kw_ref[0, 0, pl.ds(pl.multiple_of((i - n_w + 1) * TQ, TQ), n_w * TQ), :])
            adds_s, adds_w = [], []
            for u in range(nh):
                parts_s = [diag[u] + mask_scr[i]]
                parts_w = [diag[u]]
                if n_s >= 2:
                    parts_s.insert(0, prev[u] + mask_scr[i - 1])
                if n_s >= 3:
                    parts_s.insert(0, mask_scr[i - 2])
                if n_w >= 2:
                    parts_w.insert(0, prev[u])
                if n_w >= 3:
                    parts_w.insert(0, w0)
                adds_s.append(jnp.concatenate(parts_s, axis=0) if n_s > 1 else parts_s[0])
                adds_w.append(jnp.concatenate(parts_w, axis=0) if n_w > 1 else parts_w[0])
            update_all(st_s, tuple(vst_ref[0, 0, i - n_s + 1 + j] for j in range(n_s)), adds_s, sel_state)
            update_all(st_w, tuple(vwt_ref[0, 0, i - n_w + 1 + j] for j in range(n_w)), adds_w, win_state)

        pl.when(i == 0)(lambda: near(1, 1))
        pl.when(i == 1)(lambda: near(2, 2))
        pl.when((i >= 2) & (i % 2 == 0))(lambda: near(3, 3))
        pl.when((i >= 3) & (i % 2 == 1))(lambda: near(2, 3))

        for u, r in enumerate(rs):
            ot = (oc_scr[r] + (acc_scr[0, u] / l_scr[0, u]) * as_ref[r, 0].astype(F32)
                  + (acc_scr[1, u] / l_scr[1, u]) * aw_ref[r, 0].astype(F32))
            o_ref[r] = ot.T.astype(o_ref.dtype)
        return 0

    lax.fori_loop(0, heads // nh, head_group, 0)


def _t5_bucket(dist):
    n = jnp.maximum(dist, 0)
    max_exact = REL_BUCKETS // 2
    scaled = (jnp.log(jnp.maximum(n, max_exact).astype(F32) / max_exact)
              / math.log(REL_MAX_DISTANCE / max_exact))
    large = jnp.minimum(max_exact + (scaled * (REL_BUCKETS - max_exact)).astype(jnp.int32), REL_BUCKETS - 1)
    return jnp.where(n < max_exact, n, large)


def _bias_table(rel_bias, dist, shift):
    vals = jnp.moveaxis(rel_bias[_t5_bucket(dist)], -1, 0) - shift
    return jnp.where(dist >= 0, vals, NEG).astype(F32)


def nsa_attention(qt, kc, vct, ks, vst, kw, vwt, gates, rel_bias):
    b, h, nq = qt.shape[:3]
    s = nq * TQ
    g = kc.shape[1]
    heads = h // g
    n_cmp_pad = kc.shape[2]
    n_sel = s // SEL_BLOCK
    k_sel = min(SEL_TOPK, n_sel)
    half_t = TQ // 2
    nh = NSA_HEADS_PER_ITER
    assert heads % nh == 0
    assert s % (2 * TQ) == 0 and WINDOW == 2 * TQ and n_sel <= V7X_LANES and TQ % SEL_BLOCK == 0
    assert half_t >= REL_MAX_DISTANCE
    assert n_cmp_pad == s // CMP_STRIDE and n_cmp_pad % V7X_LANES == 0

    far = rel_bias[REL_BUCKETS - 1][:, None, None]
    per_tile = TQ // CMP_STRIDE
    band = 2 * per_tile
    assert CMP_STRIDE * (per_tile + 1) - (CMP_BLOCK - 1) >= REL_MAX_DISTANCE and n_cmp_pad % per_tile == 0
    a_rel = jnp.arange(band) - per_tile
    cb = _bias_table(rel_bias, jnp.arange(TQ)[None, :] - CMP_STRIDE * a_rel[:, None] - (CMP_BLOCK - 1), far)
    ij = jnp.arange(half_t)
    t0 = _bias_table(rel_bias, ij[None, :] - ij[:, None], far)
    t1 = _bias_table(rel_bias, half_t + ij[None, :] - ij[:, None], far)

    n_cmp = (s - CMP_BLOCK) // CMP_STRIDE + 1
    cmp_start = CMP_STRIDE * np.arange(n_cmp_pad)
    sel_start = SEL_BLOCK * np.arange(V7X_LANES)
    ovl = ((cmp_start[None, :] < (sel_start + SEL_BLOCK)[:, None])
           & ((cmp_start + CMP_BLOCK)[None, :] > sel_start[:, None])
           & (np.arange(n_cmp_pad) < n_cmp)[None, :] & (np.arange(V7X_LANES) < n_sel)[:, None])
    ovl = jnp.asarray(ovl, BF16)
    key_blk = (np.arange(s) // SEL_BLOCK).reshape(nq, TQ, 1)
    expand = jnp.asarray(key_blk == np.arange(V7X_LANES)[None, None, :], BF16)

    rows_spec = lambda n, d: pl.BlockSpec((1, 1, n, d), lambda bi, gi, i: (bi, gi, 0, 0))
    vt_spec = pl.BlockSpec((1, 1, nq, DV, TQ), lambda bi, gi, i: (bi, gi, 0, 0, 0))
    gate_spec = lambda br: pl.BlockSpec((heads, 1, DV, TQ),
                                        lambda bi, gi, i, br=br: (br * g + gi, bi * nq + i, 0, 0))
    tbl_spec = lambda shape: pl.BlockSpec((heads,) + shape, lambda bi, gi, i: (gi, 0, 0))
    kern = functools.partial(_nsa_attn_kernel, heads=heads, n_cmp_pad=n_cmp_pad, n_sel=n_sel, k_sel=k_sel)
    return pl.pallas_call(
        kern,
        grid=(b, g, nq),
        in_specs=[
            pl.BlockSpec((1, heads, 1, DKP, TQ), lambda bi, gi, i: (bi, gi, i, 0, 0)),
            rows_spec(n_cmp_pad, DKP), rows_spec(DV, n_cmp_pad),
            rows_spec(s, DKP), vt_spec, rows_spec(s, DKP), vt_spec,
            gate_spec(0), gate_spec(1), gate_spec(2),
            tbl_spec((band, TQ)), tbl_spec((half_t, half_t)), tbl_spec((half_t, half_t)),
            pl.BlockSpec((V7X_LANES, n_cmp_pad), lambda bi, gi, i: (0, 0)),
            pl.BlockSpec((nq, TQ, V7X_LANES), lambda bi, gi, i: (0, 0, 0)),
        ],
        out_specs=pl.BlockSpec((heads, TQ, DV), lambda bi, gi, i: (gi, bi * nq + i, 0)),
        out_shape=jax.ShapeDtypeStruct((h, b * s, DV), BF16),
        scratch_shapes=[
            pltpu.VMEM((2, nh, 1, TQ), F32), pltpu.VMEM((2, nh, 1, TQ), F32), pltpu.VMEM((2, nh, DV, TQ), F32),
            pltpu.VMEM((heads, DV, TQ), F32), pltpu.VMEM((nq, TQ, TQ), F32), pltpu.VMEM((nh, n_cmp_pad, TQ), F32),
        ],
        compiler_params=_cparams(("parallel", "parallel", "arbitrary")),
        name="nsa_attention",
    )(qt, kc, vct, ks, vst, kw, vwt, gates, gates, gates, cb, t0, t1, ovl, expand)


def _mla_attn_kernel(qt_ref, k_ref, vt_ref, z_ref, o_ref, m_scr, l_scr, acc_scr, *, heads):
    i = pl.program_id(2)
    kj = lax.broadcasted_iota(jnp.int32, (TQ, TQ), 0)
    qi = lax.broadcasted_iota(jnp.int32, (TQ, TQ), 1)
    causal = jnp.where(kj <= qi, 0.0, NEG)

    def far_scores(c):
        rows = _tile_ds(c, 2 * TQ)
        return tuple(_dot(k_ref[0, r, rows, :], qt_ref[0, r, 0]) for r in range(heads))

    def update_all(sts, first, n_tiles, add):
        for r in range(heads):
            vts = tuple(vt_ref[0, r, first + j] for j in range(n_tiles))
            _attn_update(sts[r], vts, add, m_scr.at[r], l_scr.at[r], acc_scr.at[r])

    _attn_init(m_scr, l_scr, acc_scr)
    n_far = i // 2

    def far(c, _):
        update_all(far_scores(c), 2 * c, 2, None)
        return 0

    lax.fori_loop(0, n_far, far, 0)

    def near(n_tiles):
        first = i - n_tiles + 1
        rows = pl.ds(pl.multiple_of(first * TQ, TQ), n_tiles * TQ)
        sts = [_dot(k_ref[0, r, rows, :], qt_ref[0, r, 0]) for r in range(heads)]
        add = causal if n_tiles == 1 else jnp.concatenate([jnp.zeros((TQ, TQ), F32), causal], axis=0)
        update_all(sts, first, n_tiles, add)

    pl.when(i % 2 == 0)(lambda: near(1))
    pl.when(i % 2 == 1)(lambda: near(2))
    for r in range(heads):
        ot = (acc_scr[r] / l_scr[r]) * z_ref[r, 0].astype(F32)
        o_ref[r] = ot.T.astype(o_ref.dtype)


def mla_attention(qt, k, vt, zg):
    b, h, nq = qt.shape[:3]
    s = nq * TQ
    hb = MLA_HEADS_PER_STEP
    assert s % (2 * TQ) == 0 and h % hb == 0
    return pl.pallas_call(
        functools.partial(_mla_attn_kernel, heads=hb),
        grid=(b, h // hb, nq),
        in_specs=[
            pl.BlockSpec((1, hb, 1, DKP, TQ), lambda bi, hi, i: (bi, hi, i, 0, 0)),
            pl.BlockSpec((1, hb, s, DKP), lambda bi, hi, i: (bi, hi, 0, 0)),
            pl.BlockSpec((1, hb, nq, DV, TQ), lambda bi, hi, i: (bi, hi, 0, 0, 0)),
            pl.BlockSpec((hb, 1, DV, TQ), lambda bi, hi, i: (hi, bi * nq + i, 0, 0)),
        ],
        out_specs=pl.BlockSpec((hb, TQ, DV), lambda bi, hi, i: (hi, bi * nq + i, 0)),
        out_shape=jax.ShapeDtypeStruct((h, b * s, DV), BF16),
        scratch_shapes=[pltpu.VMEM((hb, 1, TQ), F32), pltpu.VMEM((hb, 1, TQ), F32), pltpu.VMEM((hb, DV, TQ), F32)],
        compiler_params=_cparams(("parallel", "parallel", "arbitrary")),
        name="mla_attention",
    )(qt, k, vt, zg)


def _pad_heads(w, n_heads, d):
    k = w.shape[0]
    return jnp.pad(w.reshape(k, n_heads, d), ((0, 0), (0, 0), (0, DKP - d))).reshape(k, n_heads * DKP).astype(BF16)


def _head_rows_out(b, s, h, tm, hb, width, dtype):
    nt = s // tm
    spec = pl.BlockSpec((1, hb, tm, width), lambda i, j: (i // nt, j, i % nt, 0))
    return jax.ShapeDtypeStruct((b, h, s, width), dtype), spec


def _head_tiles_out(b, s, h, tm, hb, width, dtype):
    nt = s // tm
    spec = pl.BlockSpec((1, hb, tm // TQ, width, TQ), lambda i, j: (i // nt, j, i % nt, 0, 0))
    return jax.ShapeDtypeStruct((b, h, s // TQ, width, TQ), dtype), spec


def _heads_to_rows(o):
    h, t, d = o.shape
    return o.transpose(1, 0, 2).reshape(t, h * d)


def _tiles_t(x):
    *lead, s, w = x.shape
    return jnp.swapaxes(x.reshape(*lead, s // TQ, TQ, w), -1, -2)


def nsa_layer(x2, b, s, norm_w, rel_bias, w_in, q_norm, k_norm, pe_k, w1_k, w2_k, pe_v, w1_v, w2_v, w_out):
    t, d_model = x2.shape
    h, g, dk, dv = NSA_HEADS, NSA_KV_GROUPS, NSA_QK_DIM, NSA_V_DIM
    qw, kvw, zw = h * dk, g * (dk + dv), h * dv
    tm = _pick(s, 1024, TQ)
    hn = rmsnorm_rows(x2, norm_w, d_model, BF16, "nsa_prenorm")

    w_q = _pad_heads(w_in[:, :qw], h, dk)
    hb = _pick(h, 4)
    q_shape, q_spec = _head_tiles_out(b, s, h, tm, hb, DKP, BF16)
    gain_q = jnp.pad(q_norm, (0, DKP - dk)).reshape(1, DKP)
    qt = matmul(hn, w_q, functools.partial(_ep_q_heads, hb=hb, n_true=dk, scale=dk ** -0.5, rope=False),
                tm=tm, tn=hb * DKP, out_shapes=q_shape, out_specs=q_spec,
                extra=(gain_q,), extra_specs=(pl.BlockSpec((1, DKP), lambda i, j: (0, 0)),), name="nsa_q_proj")

    kv_cols = []
    for br in range(3):
        w_br = w_in[:, qw + br * kvw: qw + (br + 1) * kvw]
        kv_cols += [_pad_heads(w_br[:, :g * dk], g, dk), w_br[:, g * dk:].astype(BF16)]
    w_kv = jnp.concatenate(kv_cols, axis=1)
    kv_width = g * (DKP + dv)
    tn_kv = _pick(3 * kv_width, 768, V7X_LANES)
    kv = matmul(hn, w_kv, _ep_store, tm=tm, tn=tn_kv,
                out_shapes=jax.ShapeDtypeStruct((t, 3 * kv_width), F32),
                out_specs=pl.BlockSpec((tm, tn_kv), lambda i, j: (i, j)), name="nsa_kv_proj")

    def kv_branch(br):
        blk = kv[:, br * kv_width:(br + 1) * kv_width]
        k = blk[:, :g * DKP].reshape(b, s, g, DKP).transpose(0, 2, 1, 3)
        v = blk[:, g * DKP:].reshape(b, s, g, dv).transpose(0, 2, 1, 3)
        return k, v

    kc_raw, vc_raw = kv_branch(0)
    ks_raw, vs_raw = kv_branch(1)
    kw_raw, vw_raw = kv_branch(2)
    kc = compress(kc_raw, pe_k, w1_k, w2_k, k_norm[0], dk, "nsa_compress_k")
    vc = compress(vc_raw, pe_v, w1_v, w2_v, jnp.ones((dv,), F32), 0, "nsa_compress_v")
    vct = jnp.swapaxes(vc, -1, -2)
    pad_gain = lambda gn: jnp.pad(gn, (0, DKP - dk))
    ks = rmsnorm_rows(ks_raw.reshape(-1, DKP), pad_gain(k_norm[1]), dk, BF16, "nsa_knorm_sel").reshape(b, g, s, DKP)
    kw = rmsnorm_rows(kw_raw.reshape(-1, DKP), pad_gain(k_norm[2]), dk, BF16, "nsa_knorm_win").reshape(b, g, s, DKP)
    vst = _tiles_t(vs_raw).astype(BF16)
    vwt = _tiles_t(vw_raw).astype(BF16)

    z0 = qw + 3 * kvw
    w_g = w_in[:, z0 + 3 * zw:].astype(BF16)
    sg = matmul(hn, w_g, _ep_sigmoid, tm=tm, tn=3 * h,
                out_shapes=jax.ShapeDtypeStruct((t, 3 * h), F32),
                out_specs=pl.BlockSpec((tm, 3 * h), lambda i, j: (i, j)), name="nsa_gate_proj")
    hbz = _pick(h, 8)
    n_zt = 3 * h // hbz
    sg_tiles = jnp.pad(sg.reshape(t, n_zt, hbz).transpose(1, 0, 2), ((0, 0), (0, 0), (0, V7X_LANES - hbz)))
    w_z = w_in[:, z0:z0 + 3 * zw].astype(BF16)
    gates = matmul(hn, w_z, functools.partial(_ep_gate_heads, hb=hbz, gated=True), tm=tm, tn=hbz * dv,
                   out_shapes=jax.ShapeDtypeStruct((3 * h, t // TQ, dv, TQ), BF16),
                   out_specs=pl.BlockSpec((hbz, tm // TQ, dv, TQ), lambda i, j: (j, i, 0, 0)),
                   extra=(sg_tiles,), extra_specs=(pl.BlockSpec((1, tm, V7X_LANES), lambda i, j: (j, i, 0)),),
                   name="nsa_z_proj")

    o = nsa_attention(qt, kc, vct, ks, vst, kw, vwt, gates, rel_bias)
    tmo = _pick(t, 512, 8)
    tno = _pick(d_model, 512, V7X_LANES)
    return matmul(_heads_to_rows(o), w_out.astype(BF16), _ep_residual, tm=tmo, tn=tno,
                  out_shapes=jax.ShapeDtypeStruct((t, d_model), F32),
                  out_specs=pl.BlockSpec((tmo, tno), lambda i, j: (i, j)),
                  extra=(x2,), extra_specs=(pl.BlockSpec((tmo, tno), lambda i, j: (i, j)),), name="nsa_out_proj")


def mla_layer(x2, b, s, positions, norm_w, w_in, q_a_norm, w_q_b, kv_a_norm, w_kv_b, q_norm, k_norm, w_out):
    t, d_model = x2.shape
    h = MLA_HEADS
    dqk = MLA_NOPE_DIM + MLA_ROPE_DIM
    tm = _pick(s, 1024, TQ)
    hn = rmsnorm_rows(x2, norm_w, d_model, BF16, "mla_prenorm")

    half = MLA_ROPE_DIM // 2
    inv_freq = ROPE_BASE ** (-jnp.arange(half, dtype=F32) / half)
    ang = positions.astype(F32).reshape(t, 1) * inv_freq[None, :]
    zeros = jnp.zeros((t, V7X_LANES - MLA_ROPE_DIM), F32)
    cos2 = jnp.concatenate([jnp.cos(ang), jnp.cos(ang), zeros], axis=1)
    sin2 = jnp.concatenate([-jnp.sin(ang), jnp.sin(ang), zeros], axis=1)
    rope_spec = pl.BlockSpec((tm, V7X_LANES), lambda i, j: (i, 0))
    row_spec = lambda w: pl.BlockSpec((1, w), lambda i, j: (0, 0))

    c0, c1, c2 = MLA_Q_LORA, MLA_Q_LORA + MLA_KV_LORA, MLA_Q_LORA + MLA_KV_LORA + MLA_ROPE_DIM
    w_lat = jnp.concatenate([w_in[:, :c1], jnp.pad(w_in[:, c1:c2], ((0, 0), (0, V7X_LANES - MLA_ROPE_DIM)))],
                            axis=1).astype(BF16)
    lat_w = c1 + V7X_LANES
    tn_lat = _pick(lat_w, 1024, V7X_LANES)
    lat = matmul(hn, w_lat, _ep_store, tm=tm, tn=tn_lat,
                 out_shapes=jax.ShapeDtypeStruct((t, lat_w), F32),
                 out_specs=pl.BlockSpec((tm, tn_lat), lambda i, j: (i, j)), name="mla_latent_proj")
    cq = rmsnorm_rows(lat[:, :c0], q_a_norm, MLA_Q_LORA, BF16, "mla_q_a_norm")
    ckv = rmsnorm_rows(lat[:, c0:c1], kv_a_norm, MLA_KV_LORA, BF16, "mla_kv_a_norm")
    kpe = lat[:, c1:]

    hbz = _pick(h, 8)
    zg = matmul(hn, w_in[:, c2:].astype(BF16), functools.partial(_ep_gate_heads, hb=hbz, gated=False),
                tm=tm, tn=hbz * DV, out_shapes=jax.ShapeDtypeStruct((h, t // TQ, DV, TQ), BF16),
                out_specs=pl.BlockSpec((hbz, tm // TQ, DV, TQ), lambda i, j: (j, i, 0, 0)), name="mla_z_proj")

    hb = _pick(h, 4)
    q_shape, q_spec = _head_tiles_out(b, s, h, tm, hb, DKP, BF16)
    gain_q = jnp.pad(q_norm, (0, DKP - dqk)).reshape(1, DKP)
    qt = matmul(cq, _pad_heads(w_q_b, h, dqk),
                functools.partial(_ep_q_heads, hb=hb, n_true=dqk, scale=dqk ** -0.5, rope=True),
                tm=tm, tn=hb * DKP, out_shapes=q_shape, out_specs=q_spec,
                extra=(gain_q, cos2, sin2), extra_specs=(row_spec(DKP), rope_spec, rope_spec), name="mla_q_proj")

    k_shape, k_spec = _head_rows_out(b, s, h, tm, hb, DKP, BF16)
    v_shape, v_spec = _head_tiles_out(b, s, h, tm, hb, DV, BF16)
    gain_n = k_norm[:MLA_NOPE_DIM].reshape(1, MLA_NOPE_DIM)
    gain_r = jnp.pad(k_norm[MLA_NOPE_DIM:], (0, V7X_LANES - MLA_ROPE_DIM)).reshape(1, V7X_LANES)
    k, vt = matmul(ckv, w_kv_b.astype(BF16), functools.partial(_ep_mla_kv, hb=hb),
                   tm=tm, tn=hb * DKP, out_shapes=(k_shape, v_shape), out_specs=(k_spec, v_spec),
                   extra=(kpe, gain_n, gain_r, cos2, sin2),
                   extra_specs=(rope_spec, row_spec(MLA_NOPE_DIM), row_spec(V7X_LANES), rope_spec, rope_spec),
                   name="mla_kv_proj")

    o = mla_attention(qt, k, vt, zg)
    tmo = _pick(t, 512, 8)
    tno = _pick(d_model, 512, V7X_LANES)
    return matmul(_heads_to_rows(o), w_out.astype(BF16), _ep_residual, tm=tmo, tn=tno,
                  out_shapes=jax.ShapeDtypeStruct((t, d_model), F32),
                  out_specs=pl.BlockSpec((tmo, tno), lambda i, j: (i, j)),
                  extra=(x2,), extra_specs=(pl.BlockSpec((tmo, tno), lambda i, j: (i, j)),), name="mla_out_proj")


def kernel(x, positions, norm_w, rel_bias, nsa_w_in, nsa_q_norm, nsa_k_norm, nsa_cmp_pe_k, nsa_cmp_w1_k,
           nsa_cmp_w2_k, nsa_cmp_pe_v, nsa_cmp_w1_v, nsa_cmp_w2_v, nsa_w_out, mla_w_in, mla_q_a_norm,
           mla_w_q_b, mla_kv_a_norm, mla_w_kv_b, mla_q_norm, mla_k_norm, mla_w_out):
    b, s, d_model = x.shape
    x2 = x.reshape(b * s, d_model)
    depth = norm_w.shape[0]
    for layer in range(depth):
        j = layer // 2
        if layer % 2 == 0:
            x2 = nsa_layer(x2, b, s, norm_w[layer], rel_bias, nsa_w_in[j], nsa_q_norm[j], nsa_k_norm[j],
                           nsa_cmp_pe_k[j], nsa_cmp_w1_k[j], nsa_cmp_w2_k[j], nsa_cmp_pe_v[j], nsa_cmp_w1_v[j],
                           nsa_cmp_w2_v[j], nsa_w_out[j])
        else:
            x2 = mla_layer(x2, b, s, positions, norm_w[layer], mla_w_in[j], mla_q_a_norm[j], mla_w_q_b[j],
                           mla_kv_a_norm[j], mla_w_kv_b[j], mla_q_norm[j], mla_k_norm[j], mla_w_out[j])
    return x2.reshape(b, s, d_model)
```

```python
import functools
import math

import numpy as np
import jax
import jax.numpy as jnp
from jax import lax
from jax.experimental import pallas as pl
from jax.experimental.pallas import tpu as pltpu

F32 = jnp.float32
BF16 = jnp.bfloat16

EPS = 1e-6
REL_BUCKETS = 32
REL_MAX_DISTANCE = 128
NSA_HEADS = 64
NSA_KV_GROUPS = 4
NSA_QK_DIM = 192
NSA_V_DIM = 128
CMP_BLOCK = 32
CMP_STRIDE = 16
SEL_BLOCK = 64
SEL_TOPK = 16
WINDOW = 512
MLA_HEADS = 64
MLA_Q_LORA = 1536
MLA_KV_LORA = 512
MLA_NOPE_DIM = 128
MLA_ROPE_DIM = 64
MLA_V_DIM = 128
ROPE_BASE = 10000.0

V7X_LANES = 128
V7X_VMEM_BYTES = 64 * 1024 * 1024
VMEM_LIMIT_BYTES = V7X_VMEM_BYTES - 8 * 1024 * 1024

DKP = 2 * V7X_LANES
DV = V7X_LANES
TQ = 256
NEG = -1e30
M_INIT = -1e20
SUM_ROWS = 16
LOG2E = math.log2(math.e)
MLA_HEADS_PER_STEP = 4
NSA_HEADS_PER_ITER = 4
FAR_TILES = 4


def _cparams(sem):
    return pltpu.CompilerParams(dimension_semantics=sem, vmem_limit_bytes=VMEM_LIMIT_BYTES)


def _dot(a, b):
    return jnp.dot(a, b, preferred_element_type=F32)


def _sigmoid(x):
    return 1.0 / (1.0 + jnp.exp(-x))


def _silu(x):
    return x * _sigmoid(x)


def _pick(n, target, unit=1):
    best = unit
    c = unit
    while c <= min(n, max(target, unit)):
        if n % c == 0:
            best = c
        c += unit
    return best


def _rmsnorm_kernel(x_ref, g_ref, o_ref, *, inv_n):
    x = x_ref[...].astype(F32)
    ms = jnp.sum(x * x, axis=-1, keepdims=True) * inv_n
    o_ref[...] = (x * lax.rsqrt(ms + EPS) * g_ref[...]).astype(o_ref.dtype)


def rmsnorm_rows(x, gain, n_true, out_dtype, name):
    m, d = x.shape
    tm = _pick(m, max(8, (2 * 1024 * 1024) // (4 * d)), 8)
    return pl.pallas_call(
        functools.partial(_rmsnorm_kernel, inv_n=1.0 / n_true),
        grid=(m // tm,),
        in_specs=[pl.BlockSpec((tm, d), lambda i: (i, 0)), pl.BlockSpec((1, d), lambda i: (0, 0))],
        out_specs=pl.BlockSpec((tm, d), lambda i: (i, 0)),
        out_shape=jax.ShapeDtypeStruct((m, d), out_dtype),
        compiler_params=_cparams(("parallel",)),
        name=name,
    )(x, gain.reshape(1, d).astype(F32))


def _matmul_kernel(*refs, epilogue, n_in, slabs):
    a_ref, w_ref = refs[0], refs[1]
    extra, outs = refs[2:n_in], refs[n_in:]
    if len(a_ref.shape) == 3:
        a = jnp.concatenate([a_ref[h] for h in range(a_ref.shape[0])], axis=1)
    else:
        a = a_ref[...]
    w = w_ref[...]
    if not slabs:
        epilogue(_dot(a, w), extra, outs)
        return
    n = a.shape[0] // TQ
    acc = _dot(a[:TQ], w)
    for u in range(n):
        nxt = _dot(a[(u + 1) * TQ:(u + 2) * TQ], w) if u + 1 < n else None
        epilogue(acc, u, extra, outs)
        acc = nxt


def matmul(a, w, epilogue, *, tm, tn, out_shapes, out_specs, extra=(), extra_specs=(), slabs=False, name):
    n = w.shape[1]
    if a.ndim == 3:
        m, k = a.shape[1], a.shape[0] * a.shape[2]
        a_spec = pl.BlockSpec((a.shape[0], tm, a.shape[2]), lambda i, j: (0, i, 0))
    else:
        m, k = a.shape
        a_spec = pl.BlockSpec((tm, k), lambda i, j: (i, 0))
    assert m % tm == 0 and n % tn == 0 and k == w.shape[0], (m, tm, n, tn, k)
    assert not slabs or tm % TQ == 0
    in_specs = [a_spec, pl.BlockSpec((k, tn), lambda i, j: (0, j))]
    in_specs += list(extra_specs)
    return pl.pallas_call(
        functools.partial(_matmul_kernel, epilogue=epilogue, n_in=2 + len(extra), slabs=slabs),
        grid=(m // tm, n // tn),
        in_specs=in_specs,
        out_specs=out_specs,
        out_shape=out_shapes,
        compiler_params=_cparams(("parallel", "arbitrary")),
        name=name,
    )(a, w, *extra)


def _ep_store(acc, extra, outs):
    outs[0][...] = acc.astype(outs[0].dtype)


def _ep_sigmoid(acc, extra, outs):
    outs[0][...] = _sigmoid(acc)


def _ep_residual(acc, extra, outs):
    outs[0][...] = extra[0][...] + acc


def _ep_nsa_kv(acc, u, extra, outs, *, groups, n_true, normed):
    rows = pl.ds(u * TQ, TQ)
    gain = extra[0][...]
    for gi in range(groups):
        k = acc[:, gi * DKP:(gi + 1) * DKP]
        v = acc[:, groups * DKP + gi * DV:groups * DKP + (gi + 1) * DV]
        if normed:
            ms = jnp.sum(k * k, axis=-1, keepdims=True) * (1.0 / n_true)
            outs[0][0, gi, rows, :] = (k * lax.rsqrt(ms + EPS) * gain).astype(outs[0].dtype)
            outs[1][0, gi, u] = v.T.astype(outs[1].dtype)
        else:
            outs[0][0, gi, rows, :] = k
            outs[1][0, gi, rows, :] = v


def _rope_lanes(x, cos2, sin2):
    half = MLA_ROPE_DIM // 2
    lane = lax.broadcasted_iota(jnp.int32, x.shape, 1)
    up = pltpu.roll(x, V7X_LANES - half, axis=1)
    down = pltpu.roll(x, half, axis=1)
    rot = jnp.where(lane < half, up, down)
    return x * cos2 + rot * sin2


def _rope_rows(x, cos2, sin2):
    half = MLA_ROPE_DIM // 2
    rot = jnp.concatenate([x[half:], x[:half]], axis=0)
    return x * cos2 + rot * sin2


def _ep_q_heads(acc, u, extra, outs, *, hb, n_true, scale, rope):
    acc_t = acc.T
    gain = extra[0][...]
    zeros = jnp.zeros((DKP - n_true, TQ), F32)
    for r in range(hb):
        y = acc_t[r * n_true:(r + 1) * n_true]
        ms = jnp.sum(y * y, axis=0, keepdims=True) * (1.0 / n_true)
        yn = y * lax.rsqrt(ms + EPS) * gain
        if rope:
            nope = n_true - MLA_ROPE_DIM
            yn = jnp.concatenate([yn[:nope], _rope_rows(yn[nope:], extra[1][u], extra[2][u])], axis=0)
        outs[0][0, r, u] = (jnp.concatenate([yn, zeros], axis=0) * scale).astype(outs[0].dtype)


def _ep_gate_heads(acc, u, extra, outs, *, hb, gated):
    acc_t = acc.T
    for r in range(hb):
        a = _silu(acc_t[r * DV:(r + 1) * DV])
        if gated:
            a = a * extra[0][0, u, r:r + 1, :]
        outs[0][r, u] = a.astype(outs[0].dtype)


def _ep_mla_kv(acc, u, extra, outs, *, hb):
    rows = pl.ds(u * TQ, TQ)
    kpe = extra[0][rows, :]
    gain_n = extra[1][...]
    gain_r = extra[2][...]
    ss_pe = jnp.sum(kpe * kpe, axis=-1, keepdims=True)
    kr = _rope_lanes(kpe * gain_r, extra[3][rows, :], extra[4][rows, :])
    inv_n = 1.0 / (MLA_NOPE_DIM + MLA_ROPE_DIM)
    for r in range(hb):
        kn = acc[:, r * DKP:r * DKP + V7X_LANES]
        v = acc[:, r * DKP + V7X_LANES:(r + 1) * DKP]
        rs = lax.rsqrt((jnp.sum(kn * kn, axis=-1, keepdims=True) + ss_pe) * inv_n + EPS)
        outs[0][0, r, rows, :] = jnp.concatenate([kn * rs * gain_n, kr * rs], axis=1).astype(outs[0].dtype)
        outs[1][0, r, u] = v.T.astype(outs[1].dtype)


def _compress_kernel(xa_ref, xb_ref, pea_ref, peb_ref, w1a_ref, w1b_ref, w2_ref, g_ref, o_ref, *, norm_n):
    xa = (xa_ref[0, 0] + pea_ref[...]).astype(BF16)
    xb = (xb_ref[0, 0] + peb_ref[...]).astype(BF16)
    pre = _dot(xa, w1a_ref[...]) + _dot(xb, w1b_ref[...])
    y = _dot(_silu(pre).astype(BF16), w2_ref[...])
    if norm_n:
        ms = jnp.sum(y * y, axis=-1, keepdims=True) * (1.0 / norm_n)
        y = y * lax.rsqrt(ms + EPS) * g_ref[...]
    o_ref[0, 0] = y.astype(o_ref.dtype)


def compress(x_raw, pe, w1, w2, gain, norm_n, name):
    b, g, s, dp = x_raw.shape
    d = pe.shape[1]
    nch = s // CMP_STRIDE
    half = CMP_STRIDE * dp
    chunks = x_raw.reshape(b, g, nch, half)
    xb = jnp.concatenate([chunks[:, :, 1:], jnp.zeros((b, g, 1, half), F32)], axis=2)
    pe_p = jnp.pad(pe, ((0, 0), (0, dp - d)))
    pea = pe_p[:CMP_STRIDE].reshape(1, half)
    peb = pe_p[CMP_STRIDE:].reshape(1, half)
    w1p = jnp.pad(w1.reshape(CMP_BLOCK, d, d), ((0, 0), (0, dp - d), (0, dp - d))).astype(BF16)
    w1a = w1p[:CMP_STRIDE].reshape(half, dp)
    w1b = w1p[CMP_STRIDE:].reshape(half, dp)
    w2p = jnp.pad(w2, ((0, dp - d), (0, dp - d))).astype(BF16)
    gp = jnp.pad(gain, (0, dp - d)).reshape(1, dp).astype(F32)
    full = lambda shape: pl.BlockSpec(shape, lambda bi, gi: (0,) * len(shape))
    return pl.pallas_call(
        functools.partial(_compress_kernel, norm_n=norm_n),
        grid=(b, g),
        in_specs=[
            pl.BlockSpec((1, 1, nch, half), lambda bi, gi: (bi, gi, 0, 0)),
            pl.BlockSpec((1, 1, nch, half), lambda bi, gi: (bi, gi, 0, 0)),
            full((1, half)), full((1, half)), full((half, dp)), full((half, dp)), full((dp, dp)), full((1, dp)),
        ],
        out_specs=pl.BlockSpec((1, 1, nch, dp), lambda bi, gi: (bi, gi, 0, 0)),
        out_shape=jax.ShapeDtypeStruct((b, g, nch, dp), BF16),
        compiler_params=_cparams(("parallel", "arbitrary")),
        name=name,
    )(chunks, xb, pea, peb, w1a, w1b, w2p, gp)


def _attn_init(m_ref, acc_ref):
    m_ref[...] = jnp.full(m_ref.shape, M_INIT, F32)
    acc_ref[...] = jnp.zeros(acc_ref.shape, F32)


def _attn_update(st, vts, add, m_ref, acc_ref):
    ones = jnp.ones((SUM_ROWS, TQ), BF16)
    for j, vt in enumerate(vts):
        sj = st[j * TQ:(j + 1) * TQ]
        if add is not None and add[j] is not None:
            sj = sj + add[j]
        m_prev = m_ref[...]
        m_new = jnp.maximum(m_prev, jnp.max(sj, axis=0, keepdims=True))
        alpha = jnp.exp2(m_prev - m_new)
        p = jnp.exp2(sj - m_new).astype(BF16)
        acc_ref[...] = alpha * acc_ref[...] + _dot(jnp.concatenate([vt, ones], axis=0), p)
        m_ref[...] = m_new


def _attn_result(acc_ref):
    acc = acc_ref[...]
    return acc[:DV] / acc[DV:DV + 1]


def _tile_ds(idx, size):
    return pl.ds(pl.multiple_of(idx * size, size), size)


def _nsa_attn_kernel(qt_ref, kc_ref, vct_ref, ks_ref, vst_ref, kw_ref, vwt_ref, ac_ref, as_ref, aw_ref,
                     cb_ref, t0_ref, t1_ref, ovl_ref, e_ref, o_ref,
                     m_scr, acc_scr, oc_scr, mask_scr, cmpb_scr, *, heads, n_cmp_pad, n_sel, k_sel):
    nh = NSA_HEADS_PER_ITER
    i = pl.program_id(2)
    tok = i * TQ + lax.broadcasted_iota(jnp.int32, (1, TQ), 1)
    per_tile = TQ // CMP_STRIDE
    band = cb_ref.shape[1]

    valid_c = (tok >= CMP_BLOCK - 1).astype(F32)
    nrow = lax.broadcasted_iota(jnp.int32, (n_cmp_pad, TQ), 0)
    base = jnp.where(nrow >= (i + 1) * per_tile, NEG, 0.0)
    for u in range(nh):
        cmpb_scr[u] = base
    kc = kc_ref[0, 0]
    vct = vct_ref[0, 0]

    def cmp_group(gi, psum):
        rs = [gi * nh + u for u in range(nh)]
        for u, r in enumerate(rs):
            @pl.when(i > 0)
            def _():
                cmpb_scr[u, pl.ds(pl.multiple_of((i - 1) * per_tile, per_tile), band), :] = cb_ref[r]

            @pl.when(i == 0)
            def _():
                cmpb_scr[u, 0:per_tile, :] = cb_ref[r, per_tile:band, :]

        sts = [_dot(kc, qt_ref[0, r, 0]) + cmpb_scr[u] for u, r in enumerate(rs)]
        for u, r in enumerate(rs):
            p = jnp.exp2(sts[u] - jnp.max(sts[u], axis=0, keepdims=True))
            pn = p * (valid_c / jnp.sum(p, axis=0, keepdims=True))
            oc_scr[r] = _dot(vct, pn.astype(BF16)) * ac_ref[r, 0].astype(F32)
            psum = psum + pn
        return psum

    psum = lax.fori_loop(0, heads // nh, cmp_group, jnp.zeros((n_cmp_pad, TQ), F32))

    ovl = ovl_ref[...]
    p_hi = psum.astype(BF16)
    rem = psum - p_hi.astype(F32)
    p_mid = rem.astype(BF16)
    p_lo = (rem - p_mid.astype(F32)).astype(BF16)
    imp = _dot(ovl, p_hi) + _dot(ovl, p_mid) + _dot(ovl, p_lo)

    blk = lax.broadcasted_iota(jnp.int32, (V7X_LANES, TQ), 0)
    blk_f = blk.astype(F32)
    cur = jnp.right_shift(tok, int(math.log2(SEL_BLOCK)))
    forced = (blk == 0) | (blk == cur) | (blk == cur - 1)
    future = blk * SEL_BLOCK > tok
    val = jnp.where(forced, 1e30, jnp.where(future, -1.0, imp))
    val = jnp.where(blk < n_sel, val, -2.0)

    def pick(_, c):
        v, sel = c
        top = jnp.max(v, axis=0, keepdims=True)
        first = jnp.min(jnp.where(v == top, blk_f, float(V7X_LANES)), axis=0, keepdims=True)
        hit = blk_f == first
        return jnp.where(hit, -3.0, v), jnp.where(hit, 1.0, sel)

    _, sel = lax.fori_loop(0, k_sel, pick, (val, jnp.zeros((V7X_LANES, TQ), F32)))
    sel_b = sel.astype(BF16)

    def build_mask(c, _):
        mask_scr[c] = (_dot(e_ref[c], sel_b) - 1.0) * (-NEG)
        return 0

    lax.fori_loop(0, i + 1, build_mask, 0)

    ft = FAR_TILES
    n_far = jnp.maximum(i - 1, 0) // ft
    kj = lax.broadcasted_iota(jnp.int32, (TQ, TQ), 0)
    qi = lax.broadcasted_iota(jnp.int32, (TQ, TQ), 1)
    half_t = TQ // 2
    zb = jnp.zeros((half_t, half_t), F32)
    negb = jnp.full((half_t, half_t), NEG, F32)

    def head_group(gi, _):
        rs = [gi * nh + u for u in range(nh)]
        sel_state = (m_scr.at[0], acc_scr.at[0])
        win_state = (m_scr.at[1], acc_scr.at[1])

        def scores(k_rows):
            return _dot(k_rows, jnp.concatenate([qt_ref[0, r, 0] for r in rs], axis=1))

        def update_all(st_all, vts, adds, state):
            for u in range(nh):
                _attn_update(st_all[:, u * TQ:(u + 1) * TQ], vts, adds[u], *(s.at[u] for s in state))

        _attn_init(*sel_state)
        _attn_init(*win_state)

        def far(c, _):
            st_all = scores(ks_ref[0, 0, _tile_ds(c, ft * TQ), :])
            masks = [mask_scr[ft * c + j] for j in range(ft)]
            update_all(st_all, tuple(vst_ref[0, 0, ft * c + j] for j in range(ft)), [masks] * nh, sel_state)
            return 0

        lax.fori_loop(0, n_far, far, 0)

        def near(n_s, n_w):
            t0s = [t0_ref[r] for r in rs]
            t1s = [t1_ref[r] for r in rs]
            prev = [jnp.concatenate([jnp.concatenate([zb, zb], axis=1),
                                     jnp.concatenate([t1, zb], axis=1)], axis=0) for t1 in t1s]
            diag = [jnp.concatenate([jnp.concatenate([t0, t1], axis=1),
                                     jnp.concatenate([negb, t0], axis=1)], axis=0) for t0, t1 in zip(t0s, t1s)]
            w0 = jnp.where(kj > qi, 0.0, NEG)
            st_s = scores(ks_ref[0, 0, pl.ds(pl.multiple_of((i - n_s + 1) * TQ, TQ), n_s * TQ), :])
            st_w = scores(kw_ref[0, 0, pl.ds(pl.multiple_of((i - n_w + 1) * TQ, TQ), n_w * TQ), :])
            masks = [mask_scr[i - n_s + 1 + j] for j in range(n_s)]
            adds_s, adds_w = [], []
            for u in range(nh):
                parts_s = list(masks)
                parts_s[-1] = diag[u] + masks[-1]
                parts_w = [diag[u]]
                if n_s >= 2:
                    parts_s[-2] = prev[u] + masks[-2]
                if n_w >= 2:
                    parts_w.insert(0, prev[u])
                if n_w >= 3:
                    parts_w.insert(0, w0)
                adds_s.append(parts_s)
                adds_w.append(parts_w)
            update_all(st_s, tuple(vst_ref[0, 0, i - n_s + 1 + j] for j in range(n_s)), adds_s, sel_state)
            update_all(st_w, tuple(vwt_ref[0, 0, i - n_w + 1 + j] for j in range(n_w)), adds_w, win_state)

        pl.when(i == 0)(lambda: near(1, 1))
        pl.when(i == 1)(lambda: near(2, 2))
        for k in range(ft):
            pl.when((i >= 2) & ((i - 1) % ft == k))(functools.partial(near, 2 + k, 3))

        for u, r in enumerate(rs):
            ot = (oc_scr[r] + _attn_result(acc_scr.at[0, u]) * as_ref[r, 0].astype(F32)
                  + _attn_result(acc_scr.at[1, u]) * aw_ref[r, 0].astype(F32))
            o_ref[r] = ot.T.astype(o_ref.dtype)
        return 0

    lax.fori_loop(0, heads // nh, head_group, 0)


def _t5_bucket(dist):
    n = jnp.maximum(dist, 0)
    max_exact = REL_BUCKETS // 2
    scaled = (jnp.log(jnp.maximum(n, max_exact).astype(F32) / max_exact)
              / math.log(REL_MAX_DISTANCE / max_exact))
    large = jnp.minimum(max_exact + (scaled * (REL_BUCKETS - max_exact)).astype(jnp.int32), REL_BUCKETS - 1)
    return jnp.where(n < max_exact, n, large)


def _bias_table(rel_bias, dist, shift):
    vals = (jnp.moveaxis(rel_bias[_t5_bucket(dist)], -1, 0) - shift) * LOG2E
    return jnp.where(dist >= 0, vals, NEG).astype(F32)


def nsa_attention(qt, kc, vct, ks, vst, kw, vwt, gates, rel_bias):
    b, h, nq = qt.shape[:3]
    s = nq * TQ
    g = kc.shape[1]
    heads = h // g
    n_cmp_pad = kc.shape[2]
    n_sel = s // SEL_BLOCK
    k_sel = min(SEL_TOPK, n_sel)
    half_t = TQ // 2
    nh = NSA_HEADS_PER_ITER
    assert heads % nh == 0
    assert s % (2 * TQ) == 0 and WINDOW == 2 * TQ and n_sel <= V7X_LANES and TQ % SEL_BLOCK == 0
    assert half_t >= REL_MAX_DISTANCE
    assert n_cmp_pad == s // CMP_STRIDE and n_cmp_pad % V7X_LANES == 0

    far = rel_bias[REL_BUCKETS - 1][:, None, None]
    per_tile = TQ // CMP_STRIDE
    band = 2 * per_tile
    assert CMP_STRIDE * (per_tile + 1) - (CMP_BLOCK - 1) >= REL_MAX_DISTANCE and n_cmp_pad % per_tile == 0
    a_rel = jnp.arange(band) - per_tile
    cb = _bias_table(rel_bias, jnp.arange(TQ)[None, :] - CMP_STRIDE * a_rel[:, None] - (CMP_BLOCK - 1), far)
    ij = jnp.arange(half_t)
    t0 = _bias_table(rel_bias, ij[None, :] - ij[:, None], far)
    t1 = _bias_table(rel_bias, half_t + ij[None, :] - ij[:, None], far)

    n_cmp = (s - CMP_BLOCK) // CMP_STRIDE + 1
    cmp_start = CMP_STRIDE * np.arange(n_cmp_pad)
    sel_start = SEL_BLOCK * np.arange(V7X_LANES)
    ovl = ((cmp_start[None, :] < (sel_start + SEL_BLOCK)[:, None])
           & ((cmp_start + CMP_BLOCK)[None, :] > sel_start[:, None])
           & (np.arange(n_cmp_pad) < n_cmp)[None, :] & (np.arange(V7X_LANES) < n_sel)[:, None])
    ovl = jnp.asarray(ovl, BF16)
    key_blk = (np.arange(s) // SEL_BLOCK).reshape(nq, TQ, 1)
    expand = jnp.asarray(key_blk == np.arange(V7X_LANES)[None, None, :], BF16)

    once = pl.Buffered(1)
    rows_spec = lambda n, d: pl.BlockSpec((1, 1, n, d), lambda bi, gi, i: (bi, gi, 0, 0), pipeline_mode=once)
    vt_spec = pl.BlockSpec((1, 1, nq, DV, TQ), lambda bi, gi, i: (bi, gi, 0, 0, 0), pipeline_mode=once)
    gate_spec = lambda br: pl.BlockSpec((heads, 1, DV, TQ),
                                        lambda bi, gi, i, br=br: (br * g + gi, bi * nq + i, 0, 0))
    tbl_spec = lambda shape: pl.BlockSpec((heads,) + shape, lambda bi, gi, i: (gi, 0, 0), pipeline_mode=once)
    kern = functools.partial(_nsa_attn_kernel, heads=heads, n_cmp_pad=n_cmp_pad, n_sel=n_sel, k_sel=k_sel)
    return pl.pallas_call(
        kern,
        grid=(b, g, nq),
        in_specs=[
            pl.BlockSpec((1, heads, 1, DKP, TQ), lambda bi, gi, i: (bi, gi, i, 0, 0)),
            rows_spec(n_cmp_pad, DKP), rows_spec(DV, n_cmp_pad),
            rows_spec(s, DKP), vt_spec, rows_spec(s, DKP), vt_spec,
            gate_spec(0), gate_spec(1), gate_spec(2),
            tbl_spec((band, TQ)), tbl_spec((half_t, half_t)), tbl_spec((half_t, half_t)),
            pl.BlockSpec((V7X_LANES, n_cmp_pad), lambda bi, gi, i: (0, 0)),
            pl.BlockSpec((nq, TQ, V7X_LANES), lambda bi, gi, i: (0, 0, 0)),
        ],
        out_specs=pl.BlockSpec((heads, TQ, DV), lambda bi, gi, i: (gi, bi * nq + i, 0)),
        out_shape=jax.ShapeDtypeStruct((h, b * s, DV), BF16),
        scratch_shapes=[
            pltpu.VMEM((2, nh, 1, TQ), F32), pltpu.VMEM((2, nh, DV + SUM_ROWS, TQ), F32),
            pltpu.VMEM((heads, DV, TQ), F32), pltpu.VMEM((nq, TQ, TQ), F32), pltpu.VMEM((nh, n_cmp_pad, TQ), F32),
        ],
        compiler_params=_cparams(("parallel", "parallel", "arbitrary")),
        name="nsa_attention",
    )(qt, kc, vct, ks, vst, kw, vwt, gates, gates, gates, cb, t0, t1, ovl, expand)


def _mla_attn_kernel(qt_ref, k_ref, vt_ref, z_ref, o_ref, m_scr, acc_scr, *, heads):
    i = pl.program_id(2)
    kj = lax.broadcasted_iota(jnp.int32, (TQ, TQ), 0)
    qi = lax.broadcasted_iota(jnp.int32, (TQ, TQ), 1)
    causal = jnp.where(kj <= qi, 0.0, NEG)

    ft = FAR_TILES

    def block(first, n_tiles, add):
        rows = pl.ds(pl.multiple_of(first * TQ, TQ), n_tiles * TQ)
        sts = [_dot(k_ref[0, r, rows, :], qt_ref[0, r, 0]) for r in range(heads)]
        for r in range(heads):
            vts = tuple(vt_ref[0, r, first + j] for j in range(n_tiles))
            _attn_update(sts[r], vts, add, m_scr.at[r], acc_scr.at[r])

    _attn_init(m_scr, acc_scr)

    def far(c, _):
        block(ft * c, ft, None)
        return 0

    lax.fori_loop(0, i // ft, far, 0)

    def near(n_tiles):
        block(i - n_tiles + 1, n_tiles, [None] * (n_tiles - 1) + [causal])

    for k in range(ft):
        pl.when(i % ft == k)(functools.partial(near, 1 + k))
    for r in range(heads):
        ot = _attn_result(acc_scr.at[r]) * z_ref[r, 0].astype(F32)
        o_ref[r] = ot.T.astype(o_ref.dtype)


def mla_attention(qt, k, vt, zg):
    b, h, nq = qt.shape[:3]
    s = nq * TQ
    hb = MLA_HEADS_PER_STEP
    assert s % (2 * TQ) == 0 and h % hb == 0
    return pl.pallas_call(
        functools.partial(_mla_attn_kernel, heads=hb),
        grid=(b, h // hb, nq),
        in_specs=[
            pl.BlockSpec((1, hb, 1, DKP, TQ), lambda bi, hi, i: (bi, hi, i, 0, 0)),
            pl.BlockSpec((1, hb, s, DKP), lambda bi, hi, i: (bi, hi, 0, 0)),
            pl.BlockSpec((1, hb, nq, DV, TQ), lambda bi, hi, i: (bi, hi, 0, 0, 0)),
            pl.BlockSpec((hb, 1, DV, TQ), lambda bi, hi, i: (hi, bi * nq + i, 0, 0)),
        ],
        out_specs=pl.BlockSpec((hb, TQ, DV), lambda bi, hi, i: (hi, bi * nq + i, 0)),
        out_shape=jax.ShapeDtypeStruct((h, b * s, DV), BF16),
        scratch_shapes=[pltpu.VMEM((hb, 1, TQ), F32), pltpu.VMEM((hb, DV + SUM_ROWS, TQ), F32)],
        compiler_params=_cparams(("parallel", "parallel", "arbitrary")),
        name="mla_attention",
    )(qt, k, vt, zg)


def _pad_heads(w, n_heads, d):
    k = w.shape[0]
    return jnp.pad(w.reshape(k, n_heads, d), ((0, 0), (0, 0), (0, DKP - d))).reshape(k, n_heads * DKP).astype(BF16)


def _head_rows_out(b, s, h, tm, hb, width, dtype):
    nt = s // tm
    spec = pl.BlockSpec((1, hb, tm, width), lambda i, j: (i // nt, j, i % nt, 0))
    return jax.ShapeDtypeStruct((b, h, s, width), dtype), spec


def _head_tiles_out(b, s, h, tm, hb, width, dtype):
    nt = s // tm
    spec = pl.BlockSpec((1, hb, tm // TQ, width, TQ), lambda i, j: (i // nt, j, i % nt, 0, 0))
    return jax.ShapeDtypeStruct((b, h, s // TQ, width, TQ), dtype), spec


def nsa_layer(x2, b, s, norm_w, rel_bias, w_in, q_norm, k_norm, pe_k, w1_k, w2_k, pe_v, w1_v, w2_v, w_out):
    t, d_model = x2.shape
    h, g, dk, dv = NSA_HEADS, NSA_KV_GROUPS, NSA_QK_DIM, NSA_V_DIM
    qw, kvw, zw = h * dk, g * (dk + dv), h * dv
    tm = _pick(s, 1024, TQ)
    hn = rmsnorm_rows(x2, norm_w, d_model, BF16, "nsa_prenorm")

    w_q = w_in[:, :qw].astype(BF16)
    hb = _pick(h, 4)
    q_shape, q_spec = _head_tiles_out(b, s, h, tm, hb, DKP, BF16)
    qt = matmul(hn, w_q, functools.partial(_ep_q_heads, hb=hb, n_true=dk, scale=LOG2E * dk ** -0.5, rope=False),
                tm=tm, tn=hb * dk, out_shapes=q_shape, out_specs=q_spec,
                extra=(jnp.broadcast_to(q_norm[:, None], (dk, TQ)),),
                extra_specs=(pl.BlockSpec((dk, TQ), lambda i, j: (0, 0)),), slabs=True, name="nsa_q_proj")

    def kv_proj(br, normed):
        w_br = w_in[:, qw + br * kvw: qw + (br + 1) * kvw]
        w_kv = jnp.concatenate([_pad_heads(w_br[:, :g * dk], g, dk), w_br[:, g * dk:].astype(BF16)], axis=1)
        k_out = _head_rows_out(b, s, g, tm, g, DKP, BF16 if normed else F32)
        v_out = _head_tiles_out(b, s, g, tm, g, dv, BF16) if normed else _head_rows_out(b, s, g, tm, g, dv, F32)
        gain = jnp.pad(k_norm[br], (0, DKP - dk)).reshape(1, DKP)
        return matmul(hn, w_kv, functools.partial(_ep_nsa_kv, groups=g, n_true=dk, normed=normed),
                      tm=tm, tn=g * (DKP + dv), out_shapes=(k_out[0], v_out[0]), out_specs=(k_out[1], v_out[1]),
                      extra=(gain,), extra_specs=(pl.BlockSpec((1, DKP), lambda i, j: (0, 0)),),
                      slabs=True, name="nsa_kv_proj_%d" % br)

    kc_raw, vc_raw = kv_proj(0, False)
    ks, vst = kv_proj(1, True)
    kw, vwt = kv_proj(2, True)
    kc = compress(kc_raw, pe_k, w1_k, w2_k, k_norm[0], dk, "nsa_compress_k")
    vc = compress(vc_raw, pe_v, w1_v, w2_v, jnp.ones((dv,), F32), 0, "nsa_compress_v")
    vct = jnp.swapaxes(vc, -1, -2)

    z0 = qw + 3 * kvw
    w_g = w_in[:, z0 + 3 * zw:].astype(BF16)
    sg = matmul(hn, w_g, _ep_sigmoid, tm=tm, tn=3 * h,
                out_shapes=jax.ShapeDtypeStruct((t, 3 * h), F32),
                out_specs=pl.BlockSpec((tm, 3 * h), lambda i, j: (i, j)), name="nsa_gate_proj")
    hbz = _pick(h, 8)
    n_zt = 3 * h // hbz
    sg_tiles = sg.reshape(t // TQ, TQ, n_zt, hbz).transpose(2, 0, 3, 1)
    w_z = w_in[:, z0:z0 + 3 * zw].astype(BF16)
    gates = matmul(hn, w_z, functools.partial(_ep_gate_heads, hb=hbz, gated=True), tm=tm, tn=hbz * dv,
                   out_shapes=jax.ShapeDtypeStruct((3 * h, t // TQ, dv, TQ), BF16),
                   out_specs=pl.BlockSpec((hbz, tm // TQ, dv, TQ), lambda i, j: (j, i, 0, 0)),
                   extra=(sg_tiles,),
                   extra_specs=(pl.BlockSpec((1, tm // TQ, hbz, TQ), lambda i, j: (j, i, 0, 0)),),
                   slabs=True, name="nsa_z_proj")

    o = nsa_attention(qt, kc, vct, ks, vst, kw, vwt, gates, rel_bias)
    tmo = _pick(t, 512, 8)
    tno = _pick(d_model, 512, V7X_LANES)
    return matmul(o, w_out.astype(BF16), _ep_residual, tm=tmo, tn=tno,
                  out_shapes=jax.ShapeDtypeStruct((t, d_model), F32),
                  out_specs=pl.BlockSpec((tmo, tno), lambda i, j: (i, j)),
                  extra=(x2,), extra_specs=(pl.BlockSpec((tmo, tno), lambda i, j: (i, j)),), name="nsa_out_proj")


def mla_layer(x2, b, s, positions, norm_w, w_in, q_a_norm, w_q_b, kv_a_norm, w_kv_b, q_norm, k_norm, w_out):
    t, d_model = x2.shape
    h = MLA_HEADS
    dqk = MLA_NOPE_DIM + MLA_ROPE_DIM
    tm = _pick(s, 1024, TQ)
    hn = rmsnorm_rows(x2, norm_w, d_model, BF16, "mla_prenorm")

    half = MLA_ROPE_DIM // 2
    inv_freq = ROPE_BASE ** (-jnp.arange(half, dtype=F32) / half)
    ang = positions.astype(F32).reshape(t, 1) * inv_freq[None, :]
    zeros = jnp.zeros((t, V7X_LANES - MLA_ROPE_DIM), F32)
    cos2 = jnp.concatenate([jnp.cos(ang), jnp.cos(ang), zeros], axis=1)
    sin2 = jnp.concatenate([-jnp.sin(ang), jnp.sin(ang), zeros], axis=1)
    rope_spec = pl.BlockSpec((tm, V7X_LANES), lambda i, j: (i, 0))
    cos2_t = jnp.swapaxes(cos2[:, :MLA_ROPE_DIM].reshape(t // TQ, TQ, MLA_ROPE_DIM), 1, 2)
    sin2_t = jnp.swapaxes(sin2[:, :MLA_ROPE_DIM].reshape(t // TQ, TQ, MLA_ROPE_DIM), 1, 2)
    rope_t_spec = pl.BlockSpec((tm // TQ, MLA_ROPE_DIM, TQ), lambda i, j: (i, 0, 0))
    row_spec = lambda w: pl.BlockSpec((1, w), lambda i, j: (0, 0))

    c0, c1, c2 = MLA_Q_LORA, MLA_Q_LORA + MLA_KV_LORA, MLA_Q_LORA + MLA_KV_LORA + MLA_ROPE_DIM
    w_lat = jnp.concatenate([w_in[:, :c1], jnp.pad(w_in[:, c1:c2], ((0, 0), (0, DKP - MLA_ROPE_DIM)))],
                            axis=1).astype(BF16)
    lat_w = c1 + DKP
    tn_lat = _pick(lat_w, 1024, DKP)
    lat = matmul(hn, w_lat, _ep_store, tm=tm, tn=tn_lat,
                 out_shapes=jax.ShapeDtypeStruct((t, lat_w), F32),
                 out_specs=pl.BlockSpec((tm, tn_lat), lambda i, j: (i, j)), name="mla_latent_proj")
    cq = rmsnorm_rows(lat[:, :c0], q_a_norm, MLA_Q_LORA, BF16, "mla_q_a_norm")
    ckv = rmsnorm_rows(lat[:, c0:c1], kv_a_norm, MLA_KV_LORA, BF16, "mla_kv_a_norm")
    kpe = lat[:, c1:c1 + V7X_LANES]

    hbz = _pick(h, 8)
    zg = matmul(hn, w_in[:, c2:].astype(BF16), functools.partial(_ep_gate_heads, hb=hbz, gated=False),
                tm=tm, tn=hbz * DV, out_shapes=jax.ShapeDtypeStruct((h, t // TQ, DV, TQ), BF16),
                out_specs=pl.BlockSpec((hbz, tm // TQ, DV, TQ), lambda i, j: (j, i, 0, 0)),
                slabs=True, name="mla_z_proj")

    hb = _pick(h, 4)
    q_shape, q_spec = _head_tiles_out(b, s, h, tm, hb, DKP, BF16)
    qt = matmul(cq, w_q_b.astype(BF16),
                functools.partial(_ep_q_heads, hb=hb, n_true=dqk, scale=LOG2E * dqk ** -0.5, rope=True),
                tm=tm, tn=hb * dqk, out_shapes=q_shape, out_specs=q_spec,
                extra=(jnp.broadcast_to(q_norm[:, None], (dqk, TQ)), cos2_t, sin2_t),
                extra_specs=(pl.BlockSpec((dqk, TQ), lambda i, j: (0, 0)), rope_t_spec, rope_t_spec),
                slabs=True, name="mla_q_proj")

    k_shape, k_spec = _head_rows_out(b, s, h, tm, hb, DKP, BF16)
    v_shape, v_spec = _head_tiles_out(b, s, h, tm, hb, DV, BF16)
    gain_n = k_norm[:MLA_NOPE_DIM].reshape(1, MLA_NOPE_DIM)
    gain_r = jnp.pad(k_norm[MLA_NOPE_DIM:], (0, V7X_LANES - MLA_ROPE_DIM)).reshape(1, V7X_LANES)
    k, vt = matmul(ckv, w_kv_b.astype(BF16), functools.partial(_ep_mla_kv, hb=hb),
                   tm=tm, tn=hb * DKP, out_shapes=(k_shape, v_shape), out_specs=(k_spec, v_spec),
                   extra=(kpe, gain_n, gain_r, cos2, sin2),
                   extra_specs=(rope_spec, row_spec(MLA_NOPE_DIM), row_spec(V7X_LANES), rope_spec, rope_spec),
                   slabs=True, name="mla_kv_proj")

    o = mla_attention(qt, k, vt, zg)
    tmo = _pick(t, 512, 8)
    tno = _pick(d_model, 512, V7X_LANES)
    return matmul(o, w_out.astype(BF16), _ep_residual, tm=tmo, tn=tno,
                  out_shapes=jax.ShapeDtypeStruct((t, d_model), F32),
                  out_specs=pl.BlockSpec((tmo, tno), lambda i, j: (i, j)),
                  extra=(x2,), extra_specs=(pl.BlockSpec((tmo, tno), lambda i, j: (i, j)),), name="mla_out_proj")


def kernel(x, positions, norm_w, rel_bias, nsa_w_in, nsa_q_norm, nsa_k_norm, nsa_cmp_pe_k, nsa_cmp_w1_k,
           nsa_cmp_w2_k, nsa_cmp_pe_v, nsa_cmp_w1_v, nsa_cmp_w2_v, nsa_w_out, mla_w_in, mla_q_a_norm,
           mla_w_q_b, mla_kv_a_norm, mla_w_kv_b, mla_q_norm, mla_k_norm, mla_w_out):
    b, s, d_model = x.shape
    x2 = x.reshape(b * s, d_model)
    depth = norm_w.shape[0]
    for layer in range(depth):
        j = layer // 2
        if layer % 2 == 0:
            x2 = nsa_layer(x2, b, s, norm_w[layer], rel_bias, nsa_w_in[j], nsa_q_norm[j], nsa_k_norm[j],
                           nsa_cmp_pe_k[j], nsa_cmp_w1_k[j], nsa_cmp_w2_k[j], nsa_cmp_pe_v[j], nsa_cmp_w1_v[j],
                           nsa_cmp_w2_v[j], nsa_w_out[j])
        else:
            x2 = mla_layer(x2, b, s, positions, norm_w[layer], mla_w_in[j], mla_q_a_norm[j], mla_w_q_b[j],
                           mla_kv_a_norm[j], mla_w_kv_b[j], mla_q_norm[j], mla_k_norm[j], mla_w_out[j])
    return x2.reshape(b, s, d_model)
```

```python
import functools
import math

import numpy as np
import jax
import jax.numpy as jnp
from jax import lax
from jax.experimental import pallas as pl
from jax.experimental.pallas import tpu as pltpu

F32 = jnp.float32
BF16 = jnp.bfloat16

EPS = 1e-6
REL_BUCKETS = 32
REL_MAX_DISTANCE = 128
NSA_HEADS = 64
NSA_KV_GROUPS = 4
NSA_QK_DIM = 192
NSA_V_DIM = 128
CMP_BLOCK = 32
CMP_STRIDE = 16
SEL_BLOCK = 64
SEL_TOPK = 16
WINDOW = 512
MLA_HEADS = 64
MLA_Q_LORA = 1536
MLA_KV_LORA = 512
MLA_NOPE_DIM = 128
MLA_ROPE_DIM = 64
MLA_V_DIM = 128
ROPE_BASE = 10000.0

V7X_LANES = 128
V7X_VMEM_BYTES = 64 * 1024 * 1024
VMEM_LIMIT_BYTES = V7X_VMEM_BYTES - 8 * 1024 * 1024

DKP = 2 * V7X_LANES
DV = V7X_LANES
TQ = 256
NEG = -1e30
M_INIT = -1e20
SUM_ROWS = 16
LOG2E = math.log2(math.e)
MLA_HEADS_PER_STEP = 4
NSA_HEADS_PER_ITER = 4
FAR_TILES = 4


def _cparams(sem):
    return pltpu.CompilerParams(dimension_semantics=sem, vmem_limit_bytes=VMEM_LIMIT_BYTES)


def _dot(a, b):
    return jnp.dot(a, b, preferred_element_type=F32)


def _sigmoid(x):
    return 1.0 / (1.0 + jnp.exp(-x))


def _silu(x):
    return x * _sigmoid(x)


def _pick(n, target, unit=1):
    best = unit
    c = unit
    while c <= min(n, max(target, unit)):
        if n % c == 0:
            best = c
        c += unit
    return best


def _rmsnorm_kernel(x_ref, g_ref, o_ref, *, inv_n):
    x = x_ref[...].astype(F32)
    ms = jnp.sum(x * x, axis=-1, keepdims=True) * inv_n
    o_ref[...] = (x * lax.rsqrt(ms + EPS) * g_ref[...]).astype(o_ref.dtype)


def rmsnorm_rows(x, gain, n_true, out_dtype, name):
    m, d = x.shape
    tm = _pick(m, max(8, (2 * 1024 * 1024) // (4 * d)), 8)
    return pl.pallas_call(
        functools.partial(_rmsnorm_kernel, inv_n=1.0 / n_true),
        grid=(m // tm,),
        in_specs=[pl.BlockSpec((tm, d), lambda i: (i, 0)), pl.BlockSpec((1, d), lambda i: (0, 0))],
        out_specs=pl.BlockSpec((tm, d), lambda i: (i, 0)),
        out_shape=jax.ShapeDtypeStruct((m, d), out_dtype),
        compiler_params=_cparams(("parallel",)),
        name=name,
    )(x, gain.reshape(1, d).astype(F32))


def _matmul_kernel(*refs, epilogue, n_in, slabs):
    a_ref, w_ref = refs[0], refs[1]
    extra, outs = refs[2:n_in], refs[n_in:]
    if len(a_ref.shape) == 3:
        a = jnp.concatenate([a_ref[h] for h in range(a_ref.shape[0])], axis=1)
    else:
        a = a_ref[...]
    w = w_ref[...]
    if not slabs:
        epilogue(_dot(a, w), extra, outs)
        return
    n = a.shape[0] // TQ
    acc = _dot(a[:TQ], w)
    for u in range(n):
        nxt = _dot(a[(u + 1) * TQ:(u + 2) * TQ], w) if u + 1 < n else None
        epilogue(acc, u, extra, outs)
        acc = nxt


def matmul(a, w, epilogue, *, tm, tn, out_shapes, out_specs, extra=(), extra_specs=(), slabs=False, name):
    n = w.shape[1]
    if a.ndim == 3:
        m, k = a.shape[1], a.shape[0] * a.shape[2]
        a_spec = pl.BlockSpec((a.shape[0], tm, a.shape[2]), lambda i, j: (0, i, 0))
    else:
        m, k = a.shape
        a_spec = pl.BlockSpec((tm, k), lambda i, j: (i, 0))
    assert m % tm == 0 and n % tn == 0 and k == w.shape[0], (m, tm, n, tn, k)
    assert not slabs or tm % TQ == 0
    in_specs = [a_spec, pl.BlockSpec((k, tn), lambda i, j: (0, j))]
    in_specs += list(extra_specs)
    return pl.pallas_call(
        functools.partial(_matmul_kernel, epilogue=epilogue, n_in=2 + len(extra), slabs=slabs),
        grid=(m // tm, n // tn),
        in_specs=in_specs,
        out_specs=out_specs,
        out_shape=out_shapes,
        compiler_params=_cparams(("parallel", "arbitrary")),
        name=name,
    )(a, w, *extra)


def _ep_store(acc, extra, outs):
    outs[0][...] = acc.astype(outs[0].dtype)


def _ep_sigmoid(acc, extra, outs):
    outs[0][...] = _sigmoid(acc)


def _ep_residual(acc, extra, outs):
    outs[0][...] = extra[0][...] + acc


def _ep_nsa_kv(acc, u, extra, outs, *, groups, n_true, normed, seq_tiles=0):
    rows = pl.ds(u * TQ, TQ)
    gain = extra[0][...]
    if seq_tiles:
        tm = outs[0].shape[2]
        pos = (pl.program_id(0) % seq_tiles) * tm + u * TQ + lax.broadcasted_iota(jnp.int32, (TQ, 1), 0)
        slot = n_true + jnp.right_shift(pos, int(math.log2(SEL_BLOCK)))
        block_lane = lax.broadcasted_iota(jnp.int32, (TQ, DKP), 1) == slot
    for gi in range(groups):
        k = acc[:, gi * DKP:(gi + 1) * DKP]
        v = acc[:, groups * DKP + gi * DV:groups * DKP + (gi + 1) * DV]
        if normed:
            ms = jnp.sum(k * k, axis=-1, keepdims=True) * (1.0 / n_true)
            kn = k * lax.rsqrt(ms + EPS) * gain
            if seq_tiles:
                kn = jnp.where(block_lane, 1.0, kn)
            outs[0][0, gi, rows, :] = kn.astype(outs[0].dtype)
            outs[1][0, gi, u] = v.T.astype(outs[1].dtype)
        else:
            outs[0][0, gi, rows, :] = k
            outs[1][0, gi, rows, :] = v


def _rope_lanes(x, cos2, sin2):
    half = MLA_ROPE_DIM // 2
    lane = lax.broadcasted_iota(jnp.int32, x.shape, 1)
    up = pltpu.roll(x, V7X_LANES - half, axis=1)
    down = pltpu.roll(x, half, axis=1)
    rot = jnp.where(lane < half, up, down)
    return x * cos2 + rot * sin2


def _rope_rows(x, cos2, sin2):
    half = MLA_ROPE_DIM // 2
    rot = jnp.concatenate([x[half:], x[:half]], axis=0)
    return x * cos2 + rot * sin2


def _ep_q_heads(acc, u, extra, outs, *, hb, n_true, scale, rope):
    acc_t = acc.T
    gain = extra[0][...]
    zeros = jnp.zeros((DKP - n_true, TQ), F32)
    for r in range(hb):
        y = acc_t[r * n_true:(r + 1) * n_true]
        ms = jnp.sum(y * y, axis=0, keepdims=True) * (1.0 / n_true)
        yn = y * lax.rsqrt(ms + EPS) * gain
        if rope:
            nope = n_true - MLA_ROPE_DIM
            yn = jnp.concatenate([yn[:nope], _rope_rows(yn[nope:], extra[1][u], extra[2][u])], axis=0)
        outs[0][0, r, u] = (jnp.concatenate([yn, zeros], axis=0) * scale).astype(outs[0].dtype)


def _ep_gate_heads(acc, u, extra, outs, *, hb, gated):
    acc_t = acc.T
    for r in range(hb):
        a = _silu(acc_t[r * DV:(r + 1) * DV])
        if gated:
            a = a * extra[0][0, u, r:r + 1, :]
        outs[0][r, u] = a.astype(outs[0].dtype)


def _ep_mla_kv(acc, u, extra, outs, *, hb):
    rows = pl.ds(u * TQ, TQ)
    kpe = extra[0][rows, :]
    gain_n = extra[1][...]
    gain_r = extra[2][...]
    ss_pe = jnp.sum(kpe * kpe, axis=-1, keepdims=True)
    kr = _rope_lanes(kpe * gain_r, extra[3][rows, :], extra[4][rows, :])
    inv_n = 1.0 / (MLA_NOPE_DIM + MLA_ROPE_DIM)
    for r in range(hb):
        kn = acc[:, r * DKP:r * DKP + V7X_LANES]
        v = acc[:, r * DKP + V7X_LANES:(r + 1) * DKP]
        rs = lax.rsqrt((jnp.sum(kn * kn, axis=-1, keepdims=True) + ss_pe) * inv_n + EPS)
        outs[0][0, r, rows, :] = jnp.concatenate([kn * rs * gain_n, kr * rs], axis=1).astype(outs[0].dtype)
        outs[1][0, r, u] = v.T.astype(outs[1].dtype)


def _compress_kernel(xa_ref, xb_ref, pea_ref, peb_ref, w1a_ref, w1b_ref, w2_ref, g_ref, o_ref, *, norm_n):
    xa = (xa_ref[0, 0] + pea_ref[...]).astype(BF16)
    xb = (xb_ref[0, 0] + peb_ref[...]).astype(BF16)
    pre = _dot(xa, w1a_ref[...]) + _dot(xb, w1b_ref[...])
    y = _dot(_silu(pre).astype(BF16), w2_ref[...])
    if norm_n:
        ms = jnp.sum(y * y, axis=-1, keepdims=True) * (1.0 / norm_n)
        y = y * lax.rsqrt(ms + EPS) * g_ref[...]
    o_ref[0, 0] = y.astype(o_ref.dtype)


def compress(x_raw, pe, w1, w2, gain, norm_n, name):
    b, g, s, dp = x_raw.shape
    d = pe.shape[1]
    nch = s // CMP_STRIDE
    half = CMP_STRIDE * dp
    chunks = x_raw.reshape(b, g, nch, half)
    xb = jnp.concatenate([chunks[:, :, 1:], jnp.zeros((b, g, 1, half), F32)], axis=2)
    pe_p = jnp.pad(pe, ((0, 0), (0, dp - d)))
    pea = pe_p[:CMP_STRIDE].reshape(1, half)
    peb = pe_p[CMP_STRIDE:].reshape(1, half)
    w1p = jnp.pad(w1.reshape(CMP_BLOCK, d, d), ((0, 0), (0, dp - d), (0, dp - d))).astype(BF16)
    w1a = w1p[:CMP_STRIDE].reshape(half, dp)
    w1b = w1p[CMP_STRIDE:].reshape(half, dp)
    w2p = jnp.pad(w2, ((0, dp - d), (0, dp - d))).astype(BF16)
    gp = jnp.pad(gain, (0, dp - d)).reshape(1, dp).astype(F32)
    full = lambda shape: pl.BlockSpec(shape, lambda bi, gi: (0,) * len(shape))
    return pl.pallas_call(
        functools.partial(_compress_kernel, norm_n=norm_n),
        grid=(b, g),
        in_specs=[
            pl.BlockSpec((1, 1, nch, half), lambda bi, gi: (bi, gi, 0, 0)),
            pl.BlockSpec((1, 1, nch, half), lambda bi, gi: (bi, gi, 0, 0)),
            full((1, half)), full((1, half)), full((half, dp)), full((half, dp)), full((dp, dp)), full((1, dp)),
        ],
        out_specs=pl.BlockSpec((1, 1, nch, dp), lambda bi, gi: (bi, gi, 0, 0)),
        out_shape=jax.ShapeDtypeStruct((b, g, nch, dp), BF16),
        compiler_params=_cparams(("parallel", "arbitrary")),
        name=name,
    )(chunks, xb, pea, peb, w1a, w1b, w2p, gp)


def _attn_init(m_ref, acc_ref):
    m_ref[...] = jnp.full(m_ref.shape, M_INIT, F32)
    acc_ref[...] = jnp.zeros(acc_ref.shape, F32)


def _attn_update(st, vts, add, m_ref, acc_ref):
    ones = jnp.ones((SUM_ROWS, TQ), BF16)
    for j, vt in enumerate(vts):
        sj = st[j * TQ:(j + 1) * TQ]
        if add is not None and add[j] is not None:
            sj = sj + add[j]
        m_prev = m_ref[...]
        m_new = jnp.maximum(m_prev, jnp.max(sj, axis=0, keepdims=True))
        alpha = jnp.exp2(m_prev - m_new)
        p = jnp.exp2(sj - m_new).astype(BF16)
        acc_ref[...] = alpha * acc_ref[...] + _dot(jnp.concatenate([vt, ones], axis=0), p)
        m_ref[...] = m_new


def _attn_result(acc_ref):
    acc = acc_ref[...]
    return acc[:DV] / acc[DV:DV + 1]


def _tile_ds(idx, size):
    return pl.ds(pl.multiple_of(idx * size, size), size)


def _nsa_attn_kernel(qt_ref, kc_ref, vct_ref, ks_ref, vst_ref, kw_ref, vwt_ref, ac_ref, as_ref, aw_ref,
                     cb_ref, t0_ref, t1_ref, ovl_ref, o_ref,
                     m_scr, acc_scr, oc_scr, qa_scr, cmpb_scr, *, heads, n_cmp_pad, n_sel, k_sel):
    nh = NSA_HEADS_PER_ITER
    i = pl.program_id(2)
    tok = i * TQ + lax.broadcasted_iota(jnp.int32, (1, TQ), 1)
    per_tile = TQ // CMP_STRIDE
    band = cb_ref.shape[1]

    valid_c = (tok >= CMP_BLOCK - 1).astype(F32)
    nrow = lax.broadcasted_iota(jnp.int32, (per_tile + n_cmp_pad, TQ), 0)
    base = jnp.where(nrow >= (i + 2) * per_tile, NEG, 0.0)
    for u in range(nh):
        cmpb_scr[u] = base
    kc = kc_ref[0, 0]
    vct = vct_ref[0, 0]

    def cmp_group(gi, psum):
        rs = [gi * nh + u for u in range(nh)]
        for u, r in enumerate(rs):
            cmpb_scr[u, pl.ds(pl.multiple_of(i * per_tile, per_tile), band), :] = cb_ref[r]
        st_all = _dot(kc, jnp.concatenate([qt_ref[0, r, 0] for r in rs], axis=1))
        for u, r in enumerate(rs):
            st = st_all[:, u * TQ:(u + 1) * TQ] + cmpb_scr[u, per_tile:per_tile + n_cmp_pad, :]
            p = jnp.exp2(st - jnp.max(st, axis=0, keepdims=True))
            pn = p * (valid_c / jnp.sum(p, axis=0, keepdims=True))
            oc_scr[r] = _dot(vct, pn.astype(BF16)) * ac_ref[r, 0].astype(F32)
            psum = psum + pn
        return psum

    psum = lax.fori_loop(0, heads // nh, cmp_group, jnp.zeros((n_cmp_pad, TQ), F32))

    ovl = ovl_ref[...]
    p_hi = psum.astype(BF16)
    rem = psum - p_hi.astype(F32)
    p_mid = rem.astype(BF16)
    p_lo = (rem - p_mid.astype(F32)).astype(BF16)
    imp = _dot(ovl, p_hi) + _dot(ovl, p_mid) + _dot(ovl, p_lo)

    n_blk = DKP - NSA_QK_DIM
    blk = lax.broadcasted_iota(jnp.int32, (n_blk, TQ), 0)
    blk_f = blk.astype(F32)
    cur = jnp.right_shift(tok, int(math.log2(SEL_BLOCK)))
    forced = (blk == 0) | (blk == cur) | (blk == cur - 1)
    future = blk * SEL_BLOCK > tok
    val = jnp.where(forced, 1e30, jnp.where(future, -1.0, imp))
    val = jnp.where(blk < n_sel, val, -2.0)

    def pick(_, c):
        v, sel = c
        top = jnp.max(v, axis=0, keepdims=True)
        first = jnp.min(jnp.where(v == top, blk_f, float(n_blk)), axis=0, keepdims=True)
        hit = blk_f == first
        return jnp.where(hit, -3.0, v), jnp.where(hit, 1.0, sel)

    _, sel = lax.fori_loop(0, k_sel, pick, (val, jnp.zeros((n_blk, TQ), F32)))

    mask_rows = ((sel - 1.0) * (-NEG)).astype(BF16)
    for r in range(heads):
        qa_scr[r] = jnp.concatenate([qt_ref[0, r, 0, 0:NSA_QK_DIM, :], mask_rows], axis=0)

    ft = FAR_TILES
    n_far = jnp.maximum(i - 1, 0) // ft
    kj = lax.broadcasted_iota(jnp.int32, (TQ, TQ), 0)
    qi = lax.broadcasted_iota(jnp.int32, (TQ, TQ), 1)
    half_t = TQ // 2
    zb = jnp.zeros((half_t, half_t), F32)
    negb = jnp.full((half_t, half_t), NEG, F32)

    def head_group(gi, _):
        rs = [gi * nh + u for u in range(nh)]
        sel_state = (m_scr.at[0], acc_scr.at[0])
        win_state = (m_scr.at[1], acc_scr.at[1])

        def scores(k_rows, masked):
            qs = [qa_scr[r] if masked else qt_ref[0, r, 0] for r in rs]
            return _dot(k_rows, jnp.concatenate(qs, axis=1))

        def update_all(st_all, vts, adds, state):
            for u in range(nh):
                _attn_update(st_all[:, u * TQ:(u + 1) * TQ], vts, adds[u], *(s.at[u] for s in state))

        _attn_init(*sel_state)
        _attn_init(*win_state)

        def far(c, _):
            st_all = scores(ks_ref[0, 0, _tile_ds(c, ft * TQ), :], True)
            update_all(st_all, tuple(vst_ref[0, 0, ft * c + j] for j in range(ft)), [None] * nh, sel_state)
            return 0

        lax.fori_loop(0, n_far, far, 0)

        def near(n_s, n_w):
            t0s = [t0_ref[r] for r in rs]
            t1s = [t1_ref[r] for r in rs]
            prev = [jnp.concatenate([jnp.concatenate([zb, zb], axis=1),
                                     jnp.concatenate([t1, zb], axis=1)], axis=0) for t1 in t1s]
            diag = [jnp.concatenate([jnp.concatenate([t0, t1], axis=1),
                                     jnp.concatenate([negb, t0], axis=1)], axis=0) for t0, t1 in zip(t0s, t1s)]
            w0 = jnp.where(kj > qi, 0.0, NEG)
            st_s = scores(ks_ref[0, 0, pl.ds(pl.multiple_of((i - n_s + 1) * TQ, TQ), n_s * TQ), :], True)
            st_w = scores(kw_ref[0, 0, pl.ds(pl.multiple_of((i - n_w + 1) * TQ, TQ), n_w * TQ), :], False)
            adds_s, adds_w = [], []
            for u in range(nh):
                parts_s = [None] * n_s
                parts_s[-1] = diag[u]
                parts_w = [diag[u]]
                if n_s >= 2:
                    parts_s[-2] = prev[u]
                if n_w >= 2:
                    parts_w.insert(0, prev[u])
                if n_w >= 3:
                    parts_w.insert(0, w0)
                adds_s.append(parts_s)
                adds_w.append(parts_w)
            update_all(st_s, tuple(vst_ref[0, 0, i - n_s + 1 + j] for j in range(n_s)), adds_s, sel_state)
            update_all(st_w, tuple(vwt_ref[0, 0, i - n_w + 1 + j] for j in range(n_w)), adds_w, win_state)

        pl.when(i == 0)(lambda: near(1, 1))
        pl.when(i == 1)(lambda: near(2, 2))
        for k in range(ft):
            pl.when((i >= 2) & ((i - 1) % ft == k))(functools.partial(near, 2 + k, 3))

        for u, r in enumerate(rs):
            ot = (oc_scr[r] + _attn_result(acc_scr.at[0, u]) * as_ref[r, 0].astype(F32)
                  + _attn_result(acc_scr.at[1, u]) * aw_ref[r, 0].astype(F32))
            o_ref[r] = ot.T.astype(o_ref.dtype)
        return 0

    lax.fori_loop(0, heads // nh, head_group, 0)


def _t5_bucket(dist):
    n = jnp.maximum(dist, 0)
    max_exact = REL_BUCKETS // 2
    scaled = (jnp.log(jnp.maximum(n, max_exact).astype(F32) / max_exact)
              / math.log(REL_MAX_DISTANCE / max_exact))
    large = jnp.minimum(max_exact + (scaled * (REL_BUCKETS - max_exact)).astype(jnp.int32), REL_BUCKETS - 1)
    return jnp.where(n < max_exact, n, large)


def _bias_table(rel_bias, dist, shift):
    vals = (jnp.moveaxis(rel_bias[_t5_bucket(dist)], -1, 0) - shift) * LOG2E
    return jnp.where(dist >= 0, vals, NEG).astype(F32)


def nsa_attention(qt, kc, vct, ks, vst, kw, vwt, gates, rel_bias):
    b, h, nq = qt.shape[:3]
    s = nq * TQ
    g = kc.shape[1]
    heads = h // g
    n_cmp_pad = kc.shape[2]
    n_sel = s // SEL_BLOCK
    k_sel = min(SEL_TOPK, n_sel)
    half_t = TQ // 2
    nh = NSA_HEADS_PER_ITER
    assert heads % nh == 0
    n_blk = DKP - NSA_QK_DIM
    assert s % (2 * TQ) == 0 and WINDOW == 2 * TQ and n_sel <= n_blk and TQ % SEL_BLOCK == 0
    assert half_t >= REL_MAX_DISTANCE
    assert n_cmp_pad == s // CMP_STRIDE and n_cmp_pad % V7X_LANES == 0

    far = rel_bias[REL_BUCKETS - 1][:, None, None]
    per_tile = TQ // CMP_STRIDE
    band = 2 * per_tile
    assert CMP_STRIDE * (per_tile + 1) - (CMP_BLOCK - 1) >= REL_MAX_DISTANCE and n_cmp_pad % per_tile == 0
    a_rel = jnp.arange(band) - per_tile
    cb = _bias_table(rel_bias, jnp.arange(TQ)[None, :] - CMP_STRIDE * a_rel[:, None] - (CMP_BLOCK - 1), far)
    ij = jnp.arange(half_t)
    t0 = _bias_table(rel_bias, ij[None, :] - ij[:, None], far)
    t1 = _bias_table(rel_bias, half_t + ij[None, :] - ij[:, None], far)

    n_cmp = (s - CMP_BLOCK) // CMP_STRIDE + 1
    cmp_start = CMP_STRIDE * np.arange(n_cmp_pad)
    sel_start = SEL_BLOCK * np.arange(n_blk)
    ovl = ((cmp_start[None, :] < (sel_start + SEL_BLOCK)[:, None])
           & ((cmp_start + CMP_BLOCK)[None, :] > sel_start[:, None])
           & (np.arange(n_cmp_pad) < n_cmp)[None, :] & (np.arange(n_blk) < n_sel)[:, None])
    ovl = jnp.asarray(ovl, BF16)

    once = pl.Buffered(1)
    rows_spec = lambda n, d: pl.BlockSpec((1, 1, n, d), lambda bi, gi, i: (bi, gi, 0, 0), pipeline_mode=once)
    vt_spec = pl.BlockSpec((1, 1, nq, DV, TQ), lambda bi, gi, i: (bi, gi, 0, 0, 0), pipeline_mode=once)
    gate_spec = lambda br: pl.BlockSpec((heads, 1, DV, TQ),
                                        lambda bi, gi, i, br=br: (br * g + gi, bi * nq + i, 0, 0))
    tbl_spec = lambda shape: pl.BlockSpec((heads,) + shape, lambda bi, gi, i: (gi, 0, 0), pipeline_mode=once)
    kern = functools.partial(_nsa_attn_kernel, heads=heads, n_cmp_pad=n_cmp_pad, n_sel=n_sel, k_sel=k_sel)
    return pl.pallas_call(
        kern,
        grid=(b, g, nq),
        in_specs=[
            pl.BlockSpec((1, heads, 1, DKP, TQ), lambda bi, gi, i: (bi, gi, i, 0, 0)),
            rows_spec(n_cmp_pad, DKP), rows_spec(DV, n_cmp_pad),
            rows_spec(s, DKP), vt_spec, rows_spec(s, DKP), vt_spec,
            gate_spec(0), gate_spec(1), gate_spec(2),
            tbl_spec((band, TQ)), tbl_spec((half_t, half_t)), tbl_spec((half_t, half_t)),
            pl.BlockSpec((n_blk, n_cmp_pad), lambda bi, gi, i: (0, 0)),
        ],
        out_specs=pl.BlockSpec((heads, TQ, DV), lambda bi, gi, i: (gi, bi * nq + i, 0)),
        out_shape=jax.ShapeDtypeStruct((h, b * s, DV), BF16),
        scratch_shapes=[
            pltpu.VMEM((2, nh, 1, TQ), F32), pltpu.VMEM((2, nh, DV + SUM_ROWS, TQ), F32),
            pltpu.VMEM((heads, DV, TQ), F32), pltpu.VMEM((heads, DKP, TQ), BF16),
            pltpu.VMEM((nh, TQ // CMP_STRIDE + n_cmp_pad, TQ), F32),
        ],
        compiler_params=_cparams(("parallel", "parallel", "arbitrary")),
        name="nsa_attention",
    )(qt, kc, vct, ks, vst, kw, vwt, gates, gates, gates, cb, t0, t1, ovl)


def _mla_attn_kernel(qt_ref, k_ref, vt_ref, z_ref, o_ref, m_scr, acc_scr, *, heads):
    i = pl.program_id(2)
    kj = lax.broadcasted_iota(jnp.int32, (TQ, TQ), 0)
    qi = lax.broadcasted_iota(jnp.int32, (TQ, TQ), 1)
    causal = jnp.where(kj <= qi, 0.0, NEG)

    ft = FAR_TILES

    def block(first, n_tiles, add):
        rows = pl.ds(pl.multiple_of(first * TQ, TQ), n_tiles * TQ)
        sts = [_dot(k_ref[0, r, rows, :], qt_ref[0, r, 0]) for r in range(heads)]
        for r in range(heads):
            vts = tuple(vt_ref[0, r, first + j] for j in range(n_tiles))
            _attn_update(sts[r], vts, add, m_scr.at[r], acc_scr.at[r])

    _attn_init(m_scr, acc_scr)

    def far(c, _):
        block(ft * c, ft, None)
        return 0

    lax.fori_loop(0, i // ft, far, 0)

    def near(n_tiles):
        block(i - n_tiles + 1, n_tiles, [None] * (n_tiles - 1) + [causal])

    for k in range(ft):
        pl.when(i % ft == k)(functools.partial(near, 1 + k))
    for r in range(heads):
        ot = _attn_result(acc_scr.at[r]) * z_ref[r, 0].astype(F32)
        o_ref[r] = ot.T.astype(o_ref.dtype)


def mla_attention(qt, k, vt, zg):
    b, h, nq = qt.shape[:3]
    s = nq * TQ
    hb = MLA_HEADS_PER_STEP
    assert s % (2 * TQ) == 0 and h % hb == 0
    return pl.pallas_call(
        functools.partial(_mla_attn_kernel, heads=hb),
        grid=(b, h // hb, nq),
        in_specs=[
            pl.BlockSpec((1, hb, 1, DKP, TQ), lambda bi, hi, i: (bi, hi, i, 0, 0)),
            pl.BlockSpec((1, hb, s, DKP), lambda bi, hi, i: (bi, hi, 0, 0)),
            pl.BlockSpec((1, hb, nq, DV, TQ), lambda bi, hi, i: (bi, hi, 0, 0, 0)),
            pl.BlockSpec((hb, 1, DV, TQ), lambda bi, hi, i: (hi, bi * nq + i, 0, 0)),
        ],
        out_specs=pl.BlockSpec((hb, TQ, DV), lambda bi, hi, i: (hi, bi * nq + i, 0)),
        out_shape=jax.ShapeDtypeStruct((h, b * s, DV), BF16),
        scratch_shapes=[pltpu.VMEM((hb, 1, TQ), F32), pltpu.VMEM((hb, DV + SUM_ROWS, TQ), F32)],
        compiler_params=_cparams(("parallel", "parallel", "arbitrary")),
        name="mla_attention",
    )(qt, k, vt, zg)


def _pad_heads(w, n_heads, d):
    k = w.shape[0]
    return jnp.pad(w.reshape(k, n_heads, d), ((0, 0), (0, 0), (0, DKP - d))).reshape(k, n_heads * DKP).astype(BF16)


def _head_rows_out(b, s, h, tm, hb, width, dtype):
    nt = s // tm
    spec = pl.BlockSpec((1, hb, tm, width), lambda i, j: (i // nt, j, i % nt, 0))
    return jax.ShapeDtypeStruct((b, h, s, width), dtype), spec


def _head_tiles_out(b, s, h, tm, hb, width, dtype):
    nt = s // tm
    spec = pl.BlockSpec((1, hb, tm // TQ, width, TQ), lambda i, j: (i // nt, j, i % nt, 0, 0))
    return jax.ShapeDtypeStruct((b, h, s // TQ, width, TQ), dtype), spec


def nsa_layer(x2, b, s, norm_w, rel_bias, w_in, q_norm, k_norm, pe_k, w1_k, w2_k, pe_v, w1_v, w2_v, w_out):
    t, d_model = x2.shape
    h, g, dk, dv = NSA_HEADS, NSA_KV_GROUPS, NSA_QK_DIM, NSA_V_DIM
    qw, kvw, zw = h * dk, g * (dk + dv), h * dv
    tm = _pick(s, 1024, TQ)
    hn = rmsnorm_rows(x2, norm_w, d_model, BF16, "nsa_prenorm")

    w_q = w_in[:, :qw].astype(BF16)
    hb = _pick(h, 4)
    q_shape, q_spec = _head_tiles_out(b, s, h, tm, hb, DKP, BF16)
    qt = matmul(hn, w_q, functools.partial(_ep_q_heads, hb=hb, n_true=dk, scale=LOG2E * dk ** -0.5, rope=False),
                tm=tm, tn=hb * dk, out_shapes=q_shape, out_specs=q_spec,
                extra=(jnp.broadcast_to(q_norm[:, None], (dk, TQ)),),
                extra_specs=(pl.BlockSpec((dk, TQ), lambda i, j: (0, 0)),), slabs=True, name="nsa_q_proj")

    def kv_proj(br, normed, block_onehot=False):
        w_br = w_in[:, qw + br * kvw: qw + (br + 1) * kvw]
        w_kv = jnp.concatenate([_pad_heads(w_br[:, :g * dk], g, dk), w_br[:, g * dk:].astype(BF16)], axis=1)
        k_out = _head_rows_out(b, s, g, tm, g, DKP, BF16 if normed else F32)
        v_out = _head_tiles_out(b, s, g, tm, g, dv, BF16) if normed else _head_rows_out(b, s, g, tm, g, dv, F32)
        gain = jnp.pad(k_norm[br], (0, DKP - dk)).reshape(1, DKP)
        ep = functools.partial(_ep_nsa_kv, groups=g, n_true=dk, normed=normed,
                               seq_tiles=s // tm if block_onehot else 0)
        return matmul(hn, w_kv, ep,
                      tm=tm, tn=g * (DKP + dv), out_shapes=(k_out[0], v_out[0]), out_specs=(k_out[1], v_out[1]),
                      extra=(gain,), extra_specs=(pl.BlockSpec((1, DKP), lambda i, j: (0, 0)),),
                      slabs=True, name="nsa_kv_proj_%d" % br)

    kc_raw, vc_raw = kv_proj(0, False)
    ks, vst = kv_proj(1, True, block_onehot=True)
    kw, vwt = kv_proj(2, True)
    kc = compress(kc_raw, pe_k, w1_k, w2_k, k_norm[0], dk, "nsa_compress_k")
    vc = compress(vc_raw, pe_v, w1_v, w2_v, jnp.ones((dv,), F32), 0, "nsa_compress_v")
    vct = jnp.swapaxes(vc, -1, -2)

    z0 = qw + 3 * kvw
    w_g = w_in[:, z0 + 3 * zw:].astype(BF16)
    sg = matmul(hn, w_g, _ep_sigmoid, tm=tm, tn=3 * h,
                out_shapes=jax.ShapeDtypeStruct((t, 3 * h), F32),
                out_specs=pl.BlockSpec((tm, 3 * h), lambda i, j: (i, j)), name="nsa_gate_proj")
    hbz = _pick(h, 8)
    n_zt = 3 * h // hbz
    sg_tiles = sg.reshape(t // TQ, TQ, n_zt, hbz).transpose(2, 0, 3, 1)
    w_z = w_in[:, z0:z0 + 3 * zw].astype(BF16)
    gates = matmul(hn, w_z, functools.partial(_ep_gate_heads, hb=hbz, gated=True), tm=tm, tn=hbz * dv,
                   out_shapes=jax.ShapeDtypeStruct((3 * h, t // TQ, dv, TQ), BF16),
                   out_specs=pl.BlockSpec((hbz, tm // TQ, dv, TQ), lambda i, j: (j, i, 0, 0)),
                   extra=(sg_tiles,),
                   extra_specs=(pl.BlockSpec((1, tm // TQ, hbz, TQ), lambda i, j: (j, i, 0, 0)),),
                   slabs=True, name="nsa_z_proj")

    o = nsa_attention(qt, kc, vct, ks, vst, kw, vwt, gates, rel_bias)
    tmo = _pick(t, 512, 8)
    tno = _pick(d_model, 512, V7X_LANES)
    return matmul(o, w_out.astype(BF16), _ep_residual, tm=tmo, tn=tno,
                  out_shapes=jax.ShapeDtypeStruct((t, d_model), F32),
                  out_specs=pl.BlockSpec((tmo, tno), lambda i, j: (i, j)),
                  extra=(x2,), extra_specs=(pl.BlockSpec((tmo, tno), lambda i, j: (i, j)),), name="nsa_out_proj")


def mla_layer(x2, b, s, positions, norm_w, w_in, q_a_norm, w_q_b, kv_a_norm, w_kv_b, q_norm, k_norm, w_out):
    t, d_model = x2.shape
    h = MLA_HEADS
    dqk = MLA_NOPE_DIM + MLA_ROPE_DIM
    tm = _pick(s, 1024, TQ)
    hn = rmsnorm_rows(x2, norm_w, d_model, BF16, "mla_prenorm")

    half = MLA_ROPE_DIM // 2
    inv_freq = ROPE_BASE ** (-jnp.arange(half, dtype=F32) / half)
    ang = positions.astype(F32).reshape(t, 1) * inv_freq[None, :]
    zeros = jnp.zeros((t, V7X_LANES - MLA_ROPE_DIM), F32)
    cos2 = jnp.concatenate([jnp.cos(ang), jnp.cos(ang), zeros], axis=1)
    sin2 = jnp.concatenate([-jnp.sin(ang), jnp.sin(ang), zeros], axis=1)
    rope_spec = pl.BlockSpec((tm, V7X_LANES), lambda i, j: (i, 0))
    cos2_t = jnp.swapaxes(cos2[:, :MLA_ROPE_DIM].reshape(t // TQ, TQ, MLA_ROPE_DIM), 1, 2)
    sin2_t = jnp.swapaxes(sin2[:, :MLA_ROPE_DIM].reshape(t // TQ, TQ, MLA_ROPE_DIM), 1, 2)
    rope_t_spec = pl.BlockSpec((tm // TQ, MLA_ROPE_DIM, TQ), lambda i, j: (i, 0, 0))
    row_spec = lambda w: pl.BlockSpec((1, w), lambda i, j: (0, 0))

    c0, c1, c2 = MLA_Q_LORA, MLA_Q_LORA + MLA_KV_LORA, MLA_Q_LORA + MLA_KV_LORA + MLA_ROPE_DIM
    w_lat = jnp.concatenate([w_in[:, :c1], jnp.pad(w_in[:, c1:c2], ((0, 0), (0, DKP - MLA_ROPE_DIM)))],
                            axis=1).astype(BF16)
    lat_w = c1 + DKP
    tn_lat = _pick(lat_w, 1024, DKP)
    lat = matmul(hn, w_lat, _ep_store, tm=tm, tn=tn_lat,
                 out_shapes=jax.ShapeDtypeStruct((t, lat_w), F32),
                 out_specs=pl.BlockSpec((tm, tn_lat), lambda i, j: (i, j)), name="mla_latent_proj")
    cq = rmsnorm_rows(lat[:, :c0], q_a_norm, MLA_Q_LORA, BF16, "mla_q_a_norm")
    ckv = rmsnorm_rows(lat[:, c0:c1], kv_a_norm, MLA_KV_LORA, BF16, "mla_kv_a_norm")
    kpe = lat[:, c1:c1 + V7X_LANES]

    hbz = _pick(h, 8)
    zg = matmul(hn, w_in[:, c2:].astype(BF16), functools.partial(_ep_gate_heads, hb=hbz, gated=False),
                tm=tm, tn=hbz * DV, out_shapes=jax.ShapeDtypeStruct((h, t // TQ, DV, TQ), BF16),
                out_specs=pl.BlockSpec((hbz, tm // TQ, DV, TQ), lambda i, j: (j, i, 0, 0)),
                slabs=True, name="mla_z_proj")

    hb = _pick(h, 4)
    q_shape, q_spec = _head_tiles_out(b, s, h, tm, hb, DKP, BF16)
    qt = matmul(cq, w_q_b.astype(BF16),
                functools.partial(_ep_q_heads, hb=hb, n_true=dqk, scale=LOG2E * dqk ** -0.5, rope=True),
                tm=tm, tn=hb * dqk, out_shapes=q_shape, out_specs=q_spec,
                extra=(jnp.broadcast_to(q_norm[:, None], (dqk, TQ)), cos2_t, sin2_t),
                extra_specs=(pl.BlockSpec((dqk, TQ), lambda i, j: (0, 0)), rope_t_spec, rope_t_spec),
                slabs=True, name="mla_q_proj")

    k_shape, k_spec = _head_rows_out(b, s, h, tm, hb, DKP, BF16)
    v_shape, v_spec = _head_tiles_out(b, s, h, tm, hb, DV, BF16)
    gain_n = k_norm[:MLA_NOPE_DIM].reshape(1, MLA_NOPE_DIM)
    gain_r = jnp.pad(k_norm[MLA_NOPE_DIM:], (0, V7X_LANES - MLA_ROPE_DIM)).reshape(1, V7X_LANES)
    k, vt = matmul(ckv, w_kv_b.astype(BF16), functools.partial(_ep_mla_kv, hb=hb),
                   tm=tm, tn=hb * DKP, out_shapes=(k_shape, v_shape), out_specs=(k_spec, v_spec),
                   extra=(kpe, gain_n, gain_r, cos2, sin2),
                   extra_specs=(rope_spec, row_spec(MLA_NOPE_DIM), row_spec(V7X_LANES), rope_spec, rope_spec),
                   slabs=True, name="mla_kv_proj")

    o = mla_attention(qt, k, vt, zg)
    tmo = _pick(t, 512, 8)
    tno = _pick(d_model, 512, V7X_LANES)
    return matmul(o, w_out.astype(BF16), _ep_residual, tm=tmo, tn=tno,
                  out_shapes=jax.ShapeDtypeStruct((t, d_model), F32),
                  out_specs=pl.BlockSpec((tmo, tno), lambda i, j: (i, j)),
                  extra=(x2,), extra_specs=(pl.BlockSpec((tmo, tno), lambda i, j: (i, j)),), name="mla_out_proj")


def kernel(x, positions, norm_w, rel_bias, nsa_w_in, nsa_q_norm, nsa_k_norm, nsa_cmp_pe_k, nsa_cmp_w1_k,
           nsa_cmp_w2_k, nsa_cmp_pe_v, nsa_cmp_w1_v, nsa_cmp_w2_v, nsa_w_out, mla_w_in, mla_q_a_norm,
           mla_w_q_b, mla_kv_a_norm, mla_w_kv_b, mla_q_norm, mla_k_norm, mla_w_out):
    b, s, d_model = x.shape
    x2 = x.reshape(b * s, d_model)
    depth = norm_w.shape[0]
    for layer in range(depth):
        j = layer // 2
        if layer % 2 == 0:
            x2 = nsa_layer(x2, b, s, norm_w[layer], rel_bias, nsa_w_in[j], nsa_q_norm[j], nsa_k_norm[j],
                           nsa_cmp_pe_k[j], nsa_cmp_w1_k[j], nsa_cmp_w2_k[j], nsa_cmp_pe_v[j], nsa_cmp_w1_v[j],
                           nsa_cmp_w2_v[j], nsa_w_out[j])
        else:
            x2 = mla_layer(x2, b, s, positions, norm_w[layer], mla_w_in[j], mla_q_a_norm[j], mla_w_q_b[j],
                           mla_kv_a_norm[j], mla_w_kv_b[j], mla_q_norm[j], mla_k_norm[j], mla_w_out[j])
    return x2.reshape(b, s, d_model)
```

```python
import functools
import math

import numpy as np
import jax
import jax.numpy as jnp
from jax import lax
from jax.experimental import pallas as pl
from jax.experimental.pallas import tpu as pltpu

F32 = jnp.float32
BF16 = jnp.bfloat16

EPS = 1e-6
REL_BUCKETS = 32
REL_MAX_DISTANCE = 128
NSA_HEADS = 64
NSA_KV_GROUPS = 4
NSA_QK_DIM = 192
NSA_V_DIM = 128
CMP_BLOCK = 32
CMP_STRIDE = 16
SEL_BLOCK = 64
SEL_TOPK = 16
WINDOW = 512
MLA_HEADS = 64
MLA_Q_LORA = 1536
MLA_KV_LORA = 512
MLA_NOPE_DIM = 128
MLA_ROPE_DIM = 64
MLA_V_DIM = 128
ROPE_BASE = 10000.0

V7X_LANES = 128
V7X_VMEM_BYTES = 64 * 1024 * 1024
VMEM_LIMIT_BYTES = V7X_VMEM_BYTES - 8 * 1024 * 1024

DKP = 2 * V7X_LANES
DV = V7X_LANES
TQ = 256
NEG = -1e30
M_INIT = -1e20
SUM_ROWS = 16
LOG2E = math.log2(math.e)
MLA_HEADS_PER_STEP = 4
NSA_HEADS_PER_ITER = 4
NSA_CMP_HEADS_PER_ITER = 8
FAR_TILES = 4


def _cparams(sem):
    return pltpu.CompilerParams(dimension_semantics=sem, vmem_limit_bytes=VMEM_LIMIT_BYTES)


def _dot(a, b):
    return jnp.dot(a, b, preferred_element_type=F32)


def _sigmoid(x):
    return 1.0 / (1.0 + jnp.exp(-x))


def _silu(x):
    return x * _sigmoid(x)


def _pick(n, target, unit=1):
    best = unit
    c = unit
    while c <= min(n, max(target, unit)):
        if n % c == 0:
            best = c
        c += unit
    return best


def _rmsnorm_kernel(x_ref, g_ref, o_ref, *, inv_n):
    x = x_ref[...].astype(F32)
    ms = jnp.sum(x * x, axis=-1, keepdims=True) * inv_n
    o_ref[...] = (x * lax.rsqrt(ms + EPS) * g_ref[...]).astype(o_ref.dtype)


def rmsnorm_rows(x, gain, n_true, out_dtype, name):
    m, d = x.shape
    tm = _pick(m, max(8, (2 * 1024 * 1024) // (4 * d)), 8)
    return pl.pallas_call(
        functools.partial(_rmsnorm_kernel, inv_n=1.0 / n_true),
        grid=(m // tm,),
        in_specs=[pl.BlockSpec((tm, d), lambda i: (i, 0)), pl.BlockSpec((1, d), lambda i: (0, 0))],
        out_specs=pl.BlockSpec((tm, d), lambda i: (i, 0)),
        out_shape=jax.ShapeDtypeStruct((m, d), out_dtype),
        compiler_params=_cparams(("parallel",)),
        name=name,
    )(x, gain.reshape(1, d).astype(F32))


def _matmul_kernel(*refs, epilogue, n_in, slabs):
    a_ref, w_ref = refs[0], refs[1]
    extra, outs = refs[2:n_in], refs[n_in:]
    if len(a_ref.shape) == 3:
        a = jnp.concatenate([a_ref[h] for h in range(a_ref.shape[0])], axis=1)
    else:
        a = a_ref[...]
    w = w_ref[...]
    if not slabs:
        epilogue(_dot(a, w), extra, outs)
        return
    n = a.shape[0] // TQ
    acc = _dot(a[:TQ], w)
    for u in range(n):
        nxt = _dot(a[(u + 1) * TQ:(u + 2) * TQ], w) if u + 1 < n else None
        epilogue(acc, u, extra, outs)
        acc = nxt


def matmul(a, w, epilogue, *, tm, tn, out_shapes, out_specs, extra=(), extra_specs=(), slabs=False, name):
    n = w.shape[1]
    if a.ndim == 3:
        m, k = a.shape[1], a.shape[0] * a.shape[2]
        a_spec = pl.BlockSpec((a.shape[0], tm, a.shape[2]), lambda i, j: (0, i, 0))
    else:
        m, k = a.shape
        a_spec = pl.BlockSpec((tm, k), lambda i, j: (i, 0))
    assert m % tm == 0 and n % tn == 0 and k == w.shape[0], (m, tm, n, tn, k)
    assert not slabs or tm % TQ == 0
    in_specs = [a_spec, pl.BlockSpec((k, tn), lambda i, j: (0, j))]
    in_specs += list(extra_specs)
    return pl.pallas_call(
        functools.partial(_matmul_kernel, epilogue=epilogue, n_in=2 + len(extra), slabs=slabs),
        grid=(m // tm, n // tn),
        in_specs=in_specs,
        out_specs=out_specs,
        out_shape=out_shapes,
        compiler_params=_cparams(("parallel", "arbitrary")),
        name=name,
    )(a, w, *extra)


def _ep_store(acc, extra, outs):
    outs[0][...] = acc.astype(outs[0].dtype)


def _ep_sigmoid(acc, extra, outs):
    outs[0][...] = _sigmoid(acc)


def _ep_residual(acc, extra, outs):
    outs[0][...] = extra[0][...] + acc


def _ep_nsa_kv(acc, u, extra, outs, *, groups, n_true, normed, seq_tiles=0):
    rows = pl.ds(u * TQ, TQ)
    gain = extra[0][...]
    if seq_tiles:
        tm = outs[0].shape[2]
        pos = (pl.program_id(0) % seq_tiles) * tm + u * TQ + lax.broadcasted_iota(jnp.int32, (TQ, 1), 0)
        slot = n_true + jnp.right_shift(pos, int(math.log2(SEL_BLOCK)))
        block_lane = lax.broadcasted_iota(jnp.int32, (TQ, DKP), 1) == slot
    for gi in range(groups):
        k = acc[:, gi * DKP:(gi + 1) * DKP]
        v = acc[:, groups * DKP + gi * DV:groups * DKP + (gi + 1) * DV]
        if normed:
            ms = jnp.sum(k * k, axis=-1, keepdims=True) * (1.0 / n_true)
            kn = k * lax.rsqrt(ms + EPS) * gain
            if seq_tiles:
                kn = jnp.where(block_lane, 1.0, kn)
            outs[0][0, gi, rows, :] = kn.astype(outs[0].dtype)
            outs[1][0, gi, u] = v.T.astype(outs[1].dtype)
        else:
            outs[0][0, gi, rows, :] = k
            outs[1][0, gi, rows, :] = v


def _rope_lanes(x, cos2, sin2):
    half = MLA_ROPE_DIM // 2
    lane = lax.broadcasted_iota(jnp.int32, x.shape, 1)
    up = pltpu.roll(x, V7X_LANES - half, axis=1)
    down = pltpu.roll(x, half, axis=1)
    rot = jnp.where(lane < half, up, down)
    return x * cos2 + rot * sin2


def _rope_rows(x, cos2, sin2):
    half = MLA_ROPE_DIM // 2
    rot = jnp.concatenate([x[half:], x[:half]], axis=0)
    return x * cos2 + rot * sin2


def _ep_q_heads(acc, u, extra, outs, *, hb, n_true, scale, rope):
    acc_t = acc.T
    gain = extra[0][...]
    zeros = jnp.zeros((DKP - n_true, TQ), F32)
    for r in range(hb):
        y = acc_t[r * n_true:(r + 1) * n_true]
        ms = jnp.sum(y * y, axis=0, keepdims=True) * (1.0 / n_true)
        yn = y * lax.rsqrt(ms + EPS) * gain
        if rope:
            nope = n_true - MLA_ROPE_DIM
            yn = jnp.concatenate([yn[:nope], _rope_rows(yn[nope:], extra[1][u], extra[2][u])], axis=0)
        outs[0][0, r, u] = (jnp.concatenate([yn, zeros], axis=0) * scale).astype(outs[0].dtype)


def _ep_gate_heads(acc, u, extra, outs, *, hb, gated):
    acc_t = acc.T
    for r in range(hb):
        a = _silu(acc_t[r * DV:(r + 1) * DV])
        if gated:
            a = a * extra[0][0, u, r:r + 1, :]
        outs[0][r, u] = a.astype(outs[0].dtype)


def _ep_mla_kv(acc, u, extra, outs, *, hb):
    rows = pl.ds(u * TQ, TQ)
    kpe = extra[0][rows, :]
    gain_n = extra[1][...]
    gain_r = extra[2][...]
    ss_pe = jnp.sum(kpe * kpe, axis=-1, keepdims=True)
    kr = _rope_lanes(kpe * gain_r, extra[3][rows, :], extra[4][rows, :])
    inv_n = 1.0 / (MLA_NOPE_DIM + MLA_ROPE_DIM)
    for r in range(hb):
        kn = acc[:, r * DKP:r * DKP + V7X_LANES]
        v = acc[:, r * DKP + V7X_LANES:(r + 1) * DKP]
        rs = lax.rsqrt((jnp.sum(kn * kn, axis=-1, keepdims=True) + ss_pe) * inv_n + EPS)
        outs[0][0, r, rows, :] = jnp.concatenate([kn * rs * gain_n, kr * rs], axis=1).astype(outs[0].dtype)
        outs[1][0, r, u] = v.T.astype(outs[1].dtype)


def _compress_kernel(xa_ref, xb_ref, pea_ref, peb_ref, w1a_ref, w1b_ref, w2_ref, g_ref, o_ref, *, norm_n):
    xa = (xa_ref[0, 0] + pea_ref[...]).astype(BF16)
    xb = (xb_ref[0, 0] + peb_ref[...]).astype(BF16)
    pre = _dot(xa, w1a_ref[...]) + _dot(xb, w1b_ref[...])
    y = _dot(_silu(pre).astype(BF16), w2_ref[...])
    if norm_n:
        ms = jnp.sum(y * y, axis=-1, keepdims=True) * (1.0 / norm_n)
        y = y * lax.rsqrt(ms + EPS) * g_ref[...]
    o_ref[0, 0] = y.astype(o_ref.dtype)


def compress(x_raw, pe, w1, w2, gain, norm_n, name):
    b, g, s, dp = x_raw.shape
    d = pe.shape[1]
    nch = s // CMP_STRIDE
    half = CMP_STRIDE * dp
    chunks = x_raw.reshape(b, g, nch, half)
    xb = jnp.concatenate([chunks[:, :, 1:], jnp.zeros((b, g, 1, half), F32)], axis=2)
    pe_p = jnp.pad(pe, ((0, 0), (0, dp - d)))
    pea = pe_p[:CMP_STRIDE].reshape(1, half)
    peb = pe_p[CMP_STRIDE:].reshape(1, half)
    w1p = jnp.pad(w1.reshape(CMP_BLOCK, d, d), ((0, 0), (0, dp - d), (0, dp - d))).astype(BF16)
    w1a = w1p[:CMP_STRIDE].reshape(half, dp)
    w1b = w1p[CMP_STRIDE:].reshape(half, dp)
    w2p = jnp.pad(w2, ((0, dp - d), (0, dp - d))).astype(BF16)
    gp = jnp.pad(gain, (0, dp - d)).reshape(1, dp).astype(F32)
    full = lambda shape: pl.BlockSpec(shape, lambda bi, gi: (0,) * len(shape))
    return pl.pallas_call(
        functools.partial(_compress_kernel, norm_n=norm_n),
        grid=(b, g),
        in_specs=[
            pl.BlockSpec((1, 1, nch, half), lambda bi, gi: (bi, gi, 0, 0)),
            pl.BlockSpec((1, 1, nch, half), lambda bi, gi: (bi, gi, 0, 0)),
            full((1, half)), full((1, half)), full((half, dp)), full((half, dp)), full((dp, dp)), full((1, dp)),
        ],
        out_specs=pl.BlockSpec((1, 1, nch, dp), lambda bi, gi: (bi, gi, 0, 0)),
        out_shape=jax.ShapeDtypeStruct((b, g, nch, dp), BF16),
        compiler_params=_cparams(("parallel", "arbitrary")),
        name=name,
    )(chunks, xb, pea, peb, w1a, w1b, w2p, gp)


def _attn_init(m_ref, acc_ref):
    m_ref[...] = jnp.full(m_ref.shape, M_INIT, F32)
    acc_ref[...] = jnp.zeros(acc_ref.shape, F32)


def _attn_update(st, vts, add, m_ref, acc_ref):
    ones = jnp.ones((SUM_ROWS, TQ), BF16)
    for j, vt in enumerate(vts):
        sj = st[j * TQ:(j + 1) * TQ]
        if add is not None and add[j] is not None:
            sj = sj + add[j]
        m_prev = m_ref[...]
        m_new = jnp.maximum(m_prev, jnp.max(sj, axis=0, keepdims=True))
        alpha = jnp.exp2(m_prev - m_new)
        p = jnp.exp2(sj - m_new).astype(BF16)
        acc_ref[...] = alpha * acc_ref[...] + _dot(jnp.concatenate([vt, ones], axis=0), p)
        m_ref[...] = m_new


def _attn_result(acc_ref):
    acc = acc_ref[...]
    return acc[:DV] / acc[DV:DV + 1]


def _tile_ds(idx, size):
    return pl.ds(pl.multiple_of(idx * size, size), size)


def _nsa_attn_kernel(qt_ref, kc_ref, vct_ref, ks_ref, vst_ref, kw_ref, vwt_ref, ac_ref, as_ref, aw_ref,
                     cb_ref, t0_ref, t1_ref, ovl_ref, o_ref,
                     m_scr, acc_scr, oc_scr, qa_scr, cmpb_scr, *, heads, n_cmp_pad, n_sel, k_sel):
    nh = NSA_HEADS_PER_ITER
    i = pl.program_id(2)
    tok = i * TQ + lax.broadcasted_iota(jnp.int32, (1, TQ), 1)
    per_tile = TQ // CMP_STRIDE
    band = cb_ref.shape[1]

    valid_c = (tok >= CMP_BLOCK - 1).astype(F32)
    nrow = lax.broadcasted_iota(jnp.int32, (per_tile + n_cmp_pad, TQ), 0)
    base = jnp.where(nrow >= (i + 2) * per_tile, NEG, 0.0)
    nc = cmpb_scr.shape[0]
    for u in range(nc):
        cmpb_scr[u] = base
    kc = kc_ref[0, 0]
    vct = vct_ref[0, 0]

    def cmp_group(gi, psum):
        rs = [gi * nc + u for u in range(nc)]
        for u, r in enumerate(rs):
            cmpb_scr[u, pl.ds(pl.multiple_of(i * per_tile, per_tile), band), :] = cb_ref[r]
        st_all = _dot(kc, jnp.concatenate([qt_ref[0, r, 0] for r in rs], axis=1))
        for u, r in enumerate(rs):
            st = st_all[:, u * TQ:(u + 1) * TQ] + cmpb_scr[u, per_tile:per_tile + n_cmp_pad, :]
            p = jnp.exp2(st - jnp.max(st, axis=0, keepdims=True))
            pn = p * (valid_c / jnp.sum(p, axis=0, keepdims=True))
            oc_scr[r] = _dot(vct, pn.astype(BF16)) * ac_ref[r, 0].astype(F32)
            psum = psum + pn
        return psum

    psum = lax.fori_loop(0, heads // nc, cmp_group, jnp.zeros((n_cmp_pad, TQ), F32))

    ovl = ovl_ref[...]
    p_hi = psum.astype(BF16)
    rem = psum - p_hi.astype(F32)
    p_mid = rem.astype(BF16)
    p_lo = (rem - p_mid.astype(F32)).astype(BF16)
    imp = _dot(ovl, p_hi) + _dot(ovl, p_mid) + _dot(ovl, p_lo)

    n_blk = DKP - NSA_QK_DIM
    blk = lax.broadcasted_iota(jnp.int32, (n_blk, TQ), 0)
    blk_f = blk.astype(F32)
    cur = jnp.right_shift(tok, int(math.log2(SEL_BLOCK)))
    forced = (blk == 0) | (blk == cur) | (blk == cur - 1)
    future = blk * SEL_BLOCK > tok
    val = jnp.where(forced, 1e30, jnp.where(future, -1.0, imp))
    val = jnp.where(blk < n_sel, val, -2.0)

    def pick(_, c):
        v, sel = c
        top = jnp.max(v, axis=0, keepdims=True)
        first = jnp.min(jnp.where(v == top, blk_f, float(n_blk)), axis=0, keepdims=True)
        hit = blk_f == first
        return jnp.where(hit, -3.0, v), jnp.where(hit, 1.0, sel)

    _, sel = lax.fori_loop(0, k_sel, pick, (val, jnp.zeros((n_blk, TQ), F32)))

    mask_rows = ((sel - 1.0) * (-NEG)).astype(BF16)
    for r in range(heads):
        qa_scr[r] = jnp.concatenate([qt_ref[0, r, 0, 0:NSA_QK_DIM, :], mask_rows], axis=0)

    ft = FAR_TILES
    n_far = jnp.maximum(i - 1, 0) // ft
    kj = lax.broadcasted_iota(jnp.int32, (TQ, TQ), 0)
    qi = lax.broadcasted_iota(jnp.int32, (TQ, TQ), 1)
    half_t = TQ // 2
    zb = jnp.zeros((half_t, half_t), F32)
    negb = jnp.full((half_t, half_t), NEG, F32)

    def head_group(gi, _):
        rs = [gi * nh + u for u in range(nh)]
        sel_state = (m_scr.at[0], acc_scr.at[0])
        win_state = (m_scr.at[1], acc_scr.at[1])

        def scores(k_rows, masked):
            qs = [qa_scr[r] if masked else qt_ref[0, r, 0] for r in rs]
            return _dot(k_rows, jnp.concatenate(qs, axis=1))

        def update_all(st_all, vts, adds, state):
            for u in range(nh):
                _attn_update(st_all[:, u * TQ:(u + 1) * TQ], vts, adds[u], *(s.at[u] for s in state))

        _attn_init(*sel_state)
        _attn_init(*win_state)

        def far(c, _):
            st_all = scores(ks_ref[0, 0, _tile_ds(c, ft * TQ), :], True)
            update_all(st_all, tuple(vst_ref[0, 0, ft * c + j] for j in range(ft)), [None] * nh, sel_state)
            return 0

        lax.fori_loop(0, n_far, far, 0)

        def near(n_s, n_w):
            t0s = [t0_ref[r] for r in rs]
            t1s = [t1_ref[r] for r in rs]
            prev = [jnp.concatenate([jnp.concatenate([zb, zb], axis=1),
                                     jnp.concatenate([t1, zb], axis=1)], axis=0) for t1 in t1s]
            diag = [jnp.concatenate([jnp.concatenate([t0, t1], axis=1),
                                     jnp.concatenate([negb, t0], axis=1)], axis=0) for t0, t1 in zip(t0s, t1s)]
            w0 = jnp.where(kj > qi, 0.0, NEG)
            st_s = scores(ks_ref[0, 0, pl.ds(pl.multiple_of((i - n_s + 1) * TQ, TQ), n_s * TQ), :], True)
            st_w = scores(kw_ref[0, 0, pl.ds(pl.multiple_of((i - n_w + 1) * TQ, TQ), n_w * TQ), :], False)
            adds_s, adds_w = [], []
            for u in range(nh):
                parts_s = [None] * n_s
                parts_s[-1] = diag[u]
                parts_w = [diag[u]]
                if n_s >= 2:
                    parts_s[-2] = prev[u]
                if n_w >= 2:
                    parts_w.insert(0, prev[u])
                if n_w >= 3:
                    parts_w.insert(0, w0)
                adds_s.append(parts_s)
                adds_w.append(parts_w)
            update_all(st_s, tuple(vst_ref[0, 0, i - n_s + 1 + j] for j in range(n_s)), adds_s, sel_state)
            update_all(st_w, tuple(vwt_ref[0, 0, i - n_w + 1 + j] for j in range(n_w)), adds_w, win_state)

        pl.when(i == 0)(lambda: near(1, 1))
        pl.when(i == 1)(lambda: near(2, 2))
        for k in range(ft):
            pl.when((i >= 2) & ((i - 1) % ft == k))(functools.partial(near, 2 + k, 3))

        for u, r in enumerate(rs):
            ot = (oc_scr[r] + _attn_result(acc_scr.at[0, u]) * as_ref[r, 0].astype(F32)
                  + _attn_result(acc_scr.at[1, u]) * aw_ref[r, 0].astype(F32))
            o_ref[r] = ot.T.astype(o_ref.dtype)
        return 0

    lax.fori_loop(0, heads // nh, head_group, 0)


def _t5_bucket(dist):
    n = jnp.maximum(dist, 0)
    max_exact = REL_BUCKETS // 2
    scaled = (jnp.log(jnp.maximum(n, max_exact).astype(F32) / max_exact)
              / math.log(REL_MAX_DISTANCE / max_exact))
    large = jnp.minimum(max_exact + (scaled * (REL_BUCKETS - max_exact)).astype(jnp.int32), REL_BUCKETS - 1)
    return jnp.where(n < max_exact, n, large)


def _bias_table(rel_bias, dist, shift):
    vals = (jnp.moveaxis(rel_bias[_t5_bucket(dist)], -1, 0) - shift) * LOG2E
    return jnp.where(dist >= 0, vals, NEG).astype(F32)


def nsa_attention(qt, kc, vct, ks, vst, kw, vwt, gates, rel_bias):
    b, h, nq = qt.shape[:3]
    s = nq * TQ
    g = kc.shape[1]
    heads = h // g
    n_cmp_pad = kc.shape[2]
    n_sel = s // SEL_BLOCK
    k_sel = min(SEL_TOPK, n_sel)
    half_t = TQ // 2
    nh = NSA_HEADS_PER_ITER
    nc = min(NSA_CMP_HEADS_PER_ITER, heads)
    assert heads % nh == 0 and heads % nc == 0
    n_blk = DKP - NSA_QK_DIM
    assert s % (2 * TQ) == 0 and WINDOW == 2 * TQ and n_sel <= n_blk and TQ % SEL_BLOCK == 0
    assert half_t >= REL_MAX_DISTANCE
    assert n_cmp_pad == s // CMP_STRIDE and n_cmp_pad % V7X_LANES == 0

    far = rel_bias[REL_BUCKETS - 1][:, None, None]
    per_tile = TQ // CMP_STRIDE
    band = 2 * per_tile
    assert CMP_STRIDE * (per_tile + 1) - (CMP_BLOCK - 1) >= REL_MAX_DISTANCE and n_cmp_pad % per_tile == 0
    a_rel = jnp.arange(band) - per_tile
    cb = _bias_table(rel_bias, jnp.arange(TQ)[None, :] - CMP_STRIDE * a_rel[:, None] - (CMP_BLOCK - 1), far)
    ij = jnp.arange(half_t)
    t0 = _bias_table(rel_bias, ij[None, :] - ij[:, None], far)
    t1 = _bias_table(rel_bias, half_t + ij[None, :] - ij[:, None], far)

    n_cmp = (s - CMP_BLOCK) // CMP_STRIDE + 1
    cmp_start = CMP_STRIDE * np.arange(n_cmp_pad)
    sel_start = SEL_BLOCK * np.arange(n_blk)
    ovl = ((cmp_start[None, :] < (sel_start + SEL_BLOCK)[:, None])
           & ((cmp_start + CMP_BLOCK)[None, :] > sel_start[:, None])
           & (np.arange(n_cmp_pad) < n_cmp)[None, :] & (np.arange(n_blk) < n_sel)[:, None])
    ovl = jnp.asarray(ovl, BF16)

    once = pl.Buffered(1)
    rows_spec = lambda n, d: pl.BlockSpec((1, 1, n, d), lambda bi, gi, i: (bi, gi, 0, 0), pipeline_mode=once)
    vt_spec = pl.BlockSpec((1, 1, nq, DV, TQ), lambda bi, gi, i: (bi, gi, 0, 0, 0), pipeline_mode=once)
    gate_spec = lambda br: pl.BlockSpec((heads, 1, DV, TQ),
                                        lambda bi, gi, i, br=br: (br * g + gi, bi * nq + i, 0, 0))
    tbl_spec = lambda shape: pl.BlockSpec((heads,) + shape, lambda bi, gi, i: (gi, 0, 0), pipeline_mode=once)
    kern = functools.partial(_nsa_attn_kernel, heads=heads, n_cmp_pad=n_cmp_pad, n_sel=n_sel, k_sel=k_sel)
    return pl.pallas_call(
        kern,
        grid=(b, g, nq),
        in_specs=[
            pl.BlockSpec((1, heads, 1, DKP, TQ), lambda bi, gi, i: (bi, gi, i, 0, 0)),
            rows_spec(n_cmp_pad, DKP), rows_spec(DV, n_cmp_pad),
            rows_spec(s, DKP), vt_spec, rows_spec(s, DKP), vt_spec,
            gate_spec(0), gate_spec(1), gate_spec(2),
            tbl_spec((band, TQ)), tbl_spec((half_t, half_t)), tbl_spec((half_t, half_t)),
            pl.BlockSpec((n_blk, n_cmp_pad), lambda bi, gi, i: (0, 0)),
        ],
        out_specs=pl.BlockSpec((heads, TQ, DV), lambda bi, gi, i: (gi, bi * nq + i, 0)),
        out_shape=jax.ShapeDtypeStruct((h, b * s, DV), BF16),
        scratch_shapes=[
            pltpu.VMEM((2, nh, 1, TQ), F32), pltpu.VMEM((2, nh, DV + SUM_ROWS, TQ), F32),
            pltpu.VMEM((heads, DV, TQ), F32), pltpu.VMEM((heads, DKP, TQ), BF16),
            pltpu.VMEM((nc, TQ // CMP_STRIDE + n_cmp_pad, TQ), F32),
        ],
        compiler_params=_cparams(("parallel", "parallel", "arbitrary")),
        name="nsa_attention",
    )(qt, kc, vct, ks, vst, kw, vwt, gates, gates, gates, cb, t0, t1, ovl)


def _mla_attn_kernel(qt_ref, k_ref, vt_ref, z_ref, o_ref, m_scr, acc_scr, *, heads):
    i = pl.program_id(2)
    kj = lax.broadcasted_iota(jnp.int32, (TQ, TQ), 0)
    qi = lax.broadcasted_iota(jnp.int32, (TQ, TQ), 1)
    causal = jnp.where(kj <= qi, 0.0, NEG)

    ft = FAR_TILES

    def block(first, n_tiles, add):
        rows = pl.ds(pl.multiple_of(first * TQ, TQ), n_tiles * TQ)
        sts = [_dot(k_ref[0, r, rows, :], qt_ref[0, r, 0]) for r in range(heads)]
        for r in range(heads):
            vts = tuple(vt_ref[0, r, first + j] for j in range(n_tiles))
            _attn_update(sts[r], vts, add, m_scr.at[r], acc_scr.at[r])

    _attn_init(m_scr, acc_scr)

    def far(c, _):
        block(ft * c, ft, None)
        return 0

    lax.fori_loop(0, jnp.maximum(i - 1, 0) // ft, far, 0)

    def near(n_tiles):
        block(i - n_tiles + 1, n_tiles, [None] * (n_tiles - 1) + [causal])

    pl.when(i == 0)(functools.partial(near, 1))
    for k in range(ft):
        pl.when((i >= 1) & ((i - 1) % ft == k))(functools.partial(near, 2 + k))
    for r in range(heads):
        ot = _attn_result(acc_scr.at[r]) * z_ref[r, 0].astype(F32)
        o_ref[r] = ot.T.astype(o_ref.dtype)


def mla_attention(qt, k, vt, zg):
    b, h, nq = qt.shape[:3]
    s = nq * TQ
    hb = MLA_HEADS_PER_STEP
    assert s % (2 * TQ) == 0 and h % hb == 0
    return pl.pallas_call(
        functools.partial(_mla_attn_kernel, heads=hb),
        grid=(b, h // hb, nq),
        in_specs=[
            pl.BlockSpec((1, hb, 1, DKP, TQ), lambda bi, hi, i: (bi, hi, i, 0, 0)),
            pl.BlockSpec((1, hb, s, DKP), lambda bi, hi, i: (bi, hi, 0, 0)),
            pl.BlockSpec((1, hb, nq, DV, TQ), lambda bi, hi, i: (bi, hi, 0, 0, 0)),
            pl.BlockSpec((hb, 1, DV, TQ), lambda bi, hi, i: (hi, bi * nq + i, 0, 0)),
        ],
        out_specs=pl.BlockSpec((hb, TQ, DV), lambda bi, hi, i: (hi, bi * nq + i, 0)),
        out_shape=jax.ShapeDtypeStruct((h, b * s, DV), BF16),
        scratch_shapes=[pltpu.VMEM((hb, 1, TQ), F32), pltpu.VMEM((hb, DV + SUM_ROWS, TQ), F32)],
        compiler_params=_cparams(("parallel", "parallel", "arbitrary")),
        name="mla_attention",
    )(qt, k, vt, zg)


def _pad_heads(w, n_heads, d):
    k = w.shape[0]
    return jnp.pad(w.reshape(k, n_heads, d), ((0, 0), (0, 0), (0, DKP - d))).reshape(k, n_heads * DKP).astype(BF16)


def _head_rows_out(b, s, h, tm, hb, width, dtype):
    nt = s // tm
    spec = pl.BlockSpec((1, hb, tm, width), lambda i, j: (i // nt, j, i % nt, 0))
    return jax.ShapeDtypeStruct((b, h, s, width), dtype), spec


def _head_tiles_out(b, s, h, tm, hb, width, dtype):
    nt = s // tm
    spec = pl.BlockSpec((1, hb, tm // TQ, width, TQ), lambda i, j: (i // nt, j, i % nt, 0, 0))
    return jax.ShapeDtypeStruct((b, h, s // TQ, width, TQ), dtype), spec


def nsa_layer(x2, b, s, norm_w, rel_bias, w_in, q_norm, k_norm, pe_k, w1_k, w2_k, pe_v, w1_v, w2_v, w_out):
    t, d_model = x2.shape
    h, g, dk, dv = NSA_HEADS, NSA_KV_GROUPS, NSA_QK_DIM, NSA_V_DIM
    qw, kvw, zw = h * dk, g * (dk + dv), h * dv
    tm = _pick(s, 1024, TQ)
    hn = rmsnorm_rows(x2, norm_w, d_model, BF16, "nsa_prenorm")

    w_q = w_in[:, :qw].astype(BF16)
    hb = _pick(h, 4)
    q_shape, q_spec = _head_tiles_out(b, s, h, tm, hb, DKP, BF16)
    qt = matmul(hn, w_q, functools.partial(_ep_q_heads, hb=hb, n_true=dk, scale=LOG2E * dk ** -0.5, rope=False),
                tm=tm, tn=hb * dk, out_shapes=q_shape, out_specs=q_spec,
                extra=(jnp.broadcast_to(q_norm[:, None], (dk, TQ)),),
                extra_specs=(pl.BlockSpec((dk, TQ), lambda i, j: (0, 0)),), slabs=True, name="nsa_q_proj")

    def kv_proj(br, normed, block_onehot=False):
        w_br = w_in[:, qw + br * kvw: qw + (br + 1) * kvw]
        w_kv = jnp.concatenate([_pad_heads(w_br[:, :g * dk], g, dk), w_br[:, g * dk:].astype(BF16)], axis=1)
        k_out = _head_rows_out(b, s, g, tm, g, DKP, BF16 if normed else F32)
        v_out = _head_tiles_out(b, s, g, tm, g, dv, BF16) if normed else _head_rows_out(b, s, g, tm, g, dv, F32)
        gain = jnp.pad(k_norm[br], (0, DKP - dk)).reshape(1, DKP)
        ep = functools.partial(_ep_nsa_kv, groups=g, n_true=dk, normed=normed,
                               seq_tiles=s // tm if block_onehot else 0)
        return matmul(hn, w_kv, ep,
                      tm=tm, tn=g * (DKP + dv), out_shapes=(k_out[0], v_out[0]), out_specs=(k_out[1], v_out[1]),
                      extra=(gain,), extra_specs=(pl.BlockSpec((1, DKP), lambda i, j: (0, 0)),),
                      slabs=True, name="nsa_kv_proj_%d" % br)

    kc_raw, vc_raw = kv_proj(0, False)
    ks, vst = kv_proj(1, True, block_onehot=True)
    kw, vwt = kv_proj(2, True)
    kc = compress(kc_raw, pe_k, w1_k, w2_k, k_norm[0], dk, "nsa_compress_k")
    vc = compress(vc_raw, pe_v, w1_v, w2_v, jnp.ones((dv,), F32), 0, "nsa_compress_v")
    vct = jnp.swapaxes(vc, -1, -2)

    z0 = qw + 3 * kvw
    w_g = w_in[:, z0 + 3 * zw:].astype(BF16)
    sg = matmul(hn, w_g, _ep_sigmoid, tm=tm, tn=3 * h,
                out_shapes=jax.ShapeDtypeStruct((t, 3 * h), F32),
                out_specs=pl.BlockSpec((tm, 3 * h), lambda i, j: (i, j)), name="nsa_gate_proj")
    hbz = _pick(h, 8)
    n_zt = 3 * h // hbz
    sg_tiles = sg.reshape(t // TQ, TQ, n_zt, hbz).transpose(2, 0, 3, 1)
    w_z = w_in[:, z0:z0 + 3 * zw].astype(BF16)
    gates = matmul(hn, w_z, functools.partial(_ep_gate_heads, hb=hbz, gated=True), tm=tm, tn=hbz * dv,
                   out_shapes=jax.ShapeDtypeStruct((3 * h, t // TQ, dv, TQ), BF16),
                   out_specs=pl.BlockSpec((hbz, tm // TQ, dv, TQ), lambda i, j: (j, i, 0, 0)),
                   extra=(sg_tiles,),
                   extra_specs=(pl.BlockSpec((1, tm // TQ, hbz, TQ), lambda i, j: (j, i, 0, 0)),),
                   slabs=True, name="nsa_z_proj")

    o = nsa_attention(qt, kc, vct, ks, vst, kw, vwt, gates, rel_bias)
    tmo = _pick(t, 512, 8)
    tno = _pick(d_model, 512, V7X_LANES)
    return matmul(o, w_out.astype(BF16), _ep_residual, tm=tmo, tn=tno,
                  out_shapes=jax.ShapeDtypeStruct((t, d_model), F32),
                  out_specs=pl.BlockSpec((tmo, tno), lambda i, j: (i, j)),
                  extra=(x2,), extra_specs=(pl.BlockSpec((tmo, tno), lambda i, j: (i, j)),), name="nsa_out_proj")


def mla_layer(x2, b, s, positions, norm_w, w_in, q_a_norm, w_q_b, kv_a_norm, w_kv_b, q_norm, k_norm, w_out):
    t, d_model = x2.shape
    h = MLA_HEADS
    dqk = MLA_NOPE_DIM + MLA_ROPE_DIM
    tm = _pick(s, 1024, TQ)
    hn = rmsnorm_rows(x2, norm_w, d_model, BF16, "mla_prenorm")

    half = MLA_ROPE_DIM // 2
    inv_freq = ROPE_BASE ** (-jnp.arange(half, dtype=F32) / half)
    ang = positions.astype(F32).reshape(t, 1) * inv_freq[None, :]
    zeros = jnp.zeros((t, V7X_LANES - MLA_ROPE_DIM), F32)
    cos2 = jnp.concatenate([jnp.cos(ang), jnp.cos(ang), zeros], axis=1)
    sin2 = jnp.concatenate([-jnp.sin(ang), jnp.sin(ang), zeros], axis=1)
    rope_spec = pl.BlockSpec((tm, V7X_LANES), lambda i, j: (i, 0))
    cos2_t = jnp.swapaxes(cos2[:, :MLA_ROPE_DIM].reshape(t // TQ, TQ, MLA_ROPE_DIM), 1, 2)
    sin2_t = jnp.swapaxes(sin2[:, :MLA_ROPE_DIM].reshape(t // TQ, TQ, MLA_ROPE_DIM), 1, 2)
    rope_t_spec = pl.BlockSpec((tm // TQ, MLA_ROPE_DIM, TQ), lambda i, j: (i, 0, 0))
    row_spec = lambda w: pl.BlockSpec((1, w), lambda i, j: (0, 0))

    c0, c1, c2 = MLA_Q_LORA, MLA_Q_LORA + MLA_KV_LORA, MLA_Q_LORA + MLA_KV_LORA + MLA_ROPE_DIM
    w_lat = jnp.concatenate([w_in[:, :c1], jnp.pad(w_in[:, c1:c2], ((0, 0), (0, DKP - MLA_ROPE_DIM)))],
                            axis=1).astype(BF16)
    lat_w = c1 + DKP
    tn_lat = _pick(lat_w, 1024, DKP)
    lat = matmul(hn, w_lat, _ep_store, tm=tm, tn=tn_lat,
                 out_shapes=jax.ShapeDtypeStruct((t, lat_w), F32),
                 out_specs=pl.BlockSpec((tm, tn_lat), lambda i, j: (i, j)), name="mla_latent_proj")
    cq = rmsnorm_rows(lat[:, :c0], q_a_norm, MLA_Q_LORA, BF16, "mla_q_a_norm")
    ckv = rmsnorm_rows(lat[:, c0:c1], kv_a_norm, MLA_KV_LORA, BF16, "mla_kv_a_norm")
    kpe = lat[:, c1:c1 + V7X_LANES]

    hbz = _pick(h, 8)
    zg = matmul(hn, w_in[:, c2:].astype(BF16), functools.partial(_ep_gate_heads, hb=hbz, gated=False),
                tm=tm, tn=hbz * DV, out_shapes=jax.ShapeDtypeStruct((h, t // TQ, DV, TQ), BF16),
                out_specs=pl.BlockSpec((hbz, tm // TQ, DV, TQ), lambda i, j: (j, i, 0, 0)),
                slabs=True, name="mla_z_proj")

    hb = _pick(h, 4)
    q_shape, q_spec = _head_tiles_out(b, s, h, tm, hb, DKP, BF16)
    qt = matmul(cq, w_q_b.astype(BF16),
                functools.partial(_ep_q_heads, hb=hb, n_true=dqk, scale=LOG2E * dqk ** -0.5, rope=True),
                tm=tm, tn=hb * dqk, out_shapes=q_shape, out_specs=q_spec,
                extra=(jnp.broadcast_to(q_norm[:, None], (dqk, TQ)), cos2_t, sin2_t),
                extra_specs=(pl.BlockSpec((dqk, TQ), lambda i, j: (0, 0)), rope_t_spec, rope_t_spec),
                slabs=True, name="mla_q_proj")

    k_shape, k_spec = _head_rows_out(b, s, h, tm, hb, DKP, BF16)
    v_shape, v_spec = _head_tiles_out(b, s, h, tm, hb, DV, BF16)
    gain_n = k_norm[:MLA_NOPE_DIM].reshape(1, MLA_NOPE_DIM)
    gain_r = jnp.pad(k_norm[MLA_NOPE_DIM:], (0, V7X_LANES - MLA_ROPE_DIM)).reshape(1, V7X_LANES)
    k, vt = matmul(ckv, w_kv_b.astype(BF16), functools.partial(_ep_mla_kv, hb=hb),
                   tm=tm, tn=hb * DKP, out_shapes=(k_shape, v_shape), out_specs=(k_spec, v_spec),
                   extra=(kpe, gain_n, gain_r, cos2, sin2),
                   extra_specs=(rope_spec, row_spec(MLA_NOPE_DIM), row_spec(V7X_LANES), rope_spec, rope_spec),
                   slabs=True, name="mla_kv_proj")

    o = mla_attention(qt, k, vt, zg)
    tmo = _pick(t, 512, 8)
    tno = _pick(d_model, 512, V7X_LANES)
    return matmul(o, w_out.astype(BF16), _ep_residual, tm=tmo, tn=tno,
                  out_shapes=jax.ShapeDtypeStruct((t, d_model), F32),
                  out_specs=pl.BlockSpec((tmo, tno), lambda i, j: (i, j)),
                  extra=(x2,), extra_specs=(pl.BlockSpec((tmo, tno), lambda i, j: (i, j)),), name="mla_out_proj")


def kernel(x, positions, norm_w, rel_bias, nsa_w_in, nsa_q_norm, nsa_k_norm, nsa_cmp_pe_k, nsa_cmp_w1_k,
           nsa_cmp_w2_k, nsa_cmp_pe_v, nsa_cmp_w1_v, nsa_cmp_w2_v, nsa_w_out, mla_w_in, mla_q_a_norm,
           mla_w_q_b, mla_kv_a_norm, mla_w_kv_b, mla_q_norm, mla_k_norm, mla_w_out):
    b, s, d_model = x.shape
    x2 = x.reshape(b * s, d_model)
    depth = norm_w.shape[0]
    for layer in range(depth):
        j = layer // 2
        if layer % 2 == 0:
            x2 = nsa_layer(x2, b, s, norm_w[layer], rel_bias, nsa_w_in[j], nsa_q_norm[j], nsa_k_norm[j],
                           nsa_cmp_pe_k[j], nsa_cmp_w1_k[j], nsa_cmp_w2_k[j], nsa_cmp_pe_v[j], nsa_cmp_w1_v[j],
                           nsa_cmp_w2_v[j], nsa_w_out[j])
        else:
            x2 = mla_layer(x2, b, s, positions, norm_w[layer], mla_w_in[j], mla_q_a_norm[j], mla_w_q_b[j],
                           mla_kv_a_norm[j], mla_w_kv_b[j], mla_q_norm[j], mla_k_norm[j], mla_w_out[j])
    return x2.reshape(b, s, d_model)
```

```python
import functools
import math

import numpy as np
import jax
import jax.numpy as jnp
from jax import lax
from jax.experimental import pallas as pl
from jax.experimental.pallas import tpu as pltpu

F32 = jnp.float32
BF16 = jnp.bfloat16

EPS = 1e-6
REL_BUCKETS = 32
REL_MAX_DISTANCE = 128
NSA_HEADS = 64
NSA_KV_GROUPS = 4
NSA_QK_DIM = 192
NSA_V_DIM = 128
CMP_BLOCK = 32
CMP_STRIDE = 16
SEL_BLOCK = 64
SEL_TOPK = 16
WINDOW = 512
MLA_HEADS = 64
MLA_Q_LORA = 1536
MLA_KV_LORA = 512
MLA_NOPE_DIM = 128
MLA_ROPE_DIM = 64
MLA_V_DIM = 128
ROPE_BASE = 10000.0

V7X_LANES = 128
V7X_VMEM_BYTES = 64 * 1024 * 1024
VMEM_LIMIT_BYTES = V7X_VMEM_BYTES - 8 * 1024 * 1024

DKP = 2 * V7X_LANES
DV = V7X_LANES
TQ = 256
NEG = -1e30
M_INIT = -1e20
SUM_ROWS = 16
LOG2E = math.log2(math.e)
MLA_HEADS_PER_STEP = 4
NSA_HEADS_PER_ITER = 4
NSA_CMP_HEADS_PER_ITER = 8
FAR_TILES = 4


def _cparams(sem):
    return pltpu.CompilerParams(dimension_semantics=sem, vmem_limit_bytes=VMEM_LIMIT_BYTES)


def _dot(a, b):
    return jnp.dot(a, b, preferred_element_type=F32)


def _sigmoid(x):
    return 1.0 / (1.0 + jnp.exp(-x))


def _silu(x):
    return x * _sigmoid(x)


def _pick(n, target, unit=1):
    best = unit
    c = unit
    while c <= min(n, max(target, unit)):
        if n % c == 0:
            best = c
        c += unit
    return best


def _rmsnorm_kernel(x_ref, g_ref, o_ref, *, inv_n):
    x = x_ref[...].astype(F32)
    ms = jnp.sum(x * x, axis=-1, keepdims=True) * inv_n
    o_ref[...] = (x * lax.rsqrt(ms + EPS) * g_ref[...]).astype(o_ref.dtype)


def rmsnorm_rows(x, gain, n_true, out_dtype, name):
    m, d = x.shape
    tm = _pick(m, max(8, (2 * 1024 * 1024) // (4 * d)), 8)
    return pl.pallas_call(
        functools.partial(_rmsnorm_kernel, inv_n=1.0 / n_true),
        grid=(m // tm,),
        in_specs=[pl.BlockSpec((tm, d), lambda i: (i, 0)), pl.BlockSpec((1, d), lambda i: (0, 0))],
        out_specs=pl.BlockSpec((tm, d), lambda i: (i, 0)),
        out_shape=jax.ShapeDtypeStruct((m, d), out_dtype),
        compiler_params=_cparams(("parallel",)),
        name=name,
    )(x, gain.reshape(1, d).astype(F32))


def _matmul_kernel(*refs, epilogue, n_in, slabs):
    a_ref, w_ref = refs[0], refs[1]
    extra, outs = refs[2:n_in], refs[n_in:]
    if len(a_ref.shape) == 3:
        a = jnp.concatenate([a_ref[h] for h in range(a_ref.shape[0])], axis=1)
    else:
        a = a_ref[...]
    w = w_ref[...]
    if not slabs:
        epilogue(_dot(a, w), extra, outs)
        return
    n = a.shape[0] // TQ
    acc = _dot(a[:TQ], w)
    for u in range(n):
        nxt = _dot(a[(u + 1) * TQ:(u + 2) * TQ], w) if u + 1 < n else None
        epilogue(acc, u, extra, outs)
        acc = nxt


def matmul(a, w, epilogue, *, tm, tn, out_shapes, out_specs, extra=(), extra_specs=(), slabs=False,
           w_once=False, name):
    n = w.shape[1]
    if a.ndim == 3:
        m, k = a.shape[1], a.shape[0] * a.shape[2]
        a_spec = pl.BlockSpec((a.shape[0], tm, a.shape[2]), lambda i, j: (0, i, 0))
    else:
        m, k = a.shape
        a_spec = pl.BlockSpec((tm, k), lambda i, j: (i, 0))
    assert m % tm == 0 and n % tn == 0 and k == w.shape[0], (m, tm, n, tn, k)
    assert not slabs or tm % TQ == 0
    assert not w_once or tn == n
    w_spec = pl.BlockSpec((k, tn), lambda i, j: (0, j), pipeline_mode=pl.Buffered(1) if w_once else None)
    in_specs = [a_spec, w_spec]
    in_specs += list(extra_specs)
    return pl.pallas_call(
        functools.partial(_matmul_kernel, epilogue=epilogue, n_in=2 + len(extra), slabs=slabs),
        grid=(m // tm, n // tn),
        in_specs=in_specs,
        out_specs=out_specs,
        out_shape=out_shapes,
        compiler_params=_cparams(("parallel", "arbitrary")),
        name=name,
    )(a, w, *extra)


def _ep_store(acc, extra, outs):
    outs[0][...] = acc.astype(outs[0].dtype)


def _ep_sigmoid(acc, extra, outs):
    outs[0][...] = _sigmoid(acc)


def _ep_residual(acc, extra, outs):
    outs[0][...] = extra[0][...] + acc


def _ep_mla_latent(acc, extra, outs, *, c0, c1):
    def normed(x, gain):
        ms = jnp.mean(x * x, axis=-1, keepdims=True)
        return x * lax.rsqrt(ms + EPS) * gain

    outs[0][...] = normed(acc[:, :c0], extra[0][...]).astype(outs[0].dtype)
    outs[1][...] = normed(acc[:, c0:c1], extra[1][...]).astype(outs[1].dtype)
    outs[2][...] = acc[:, c1:c1 + V7X_LANES]


def _ep_nsa_kv(acc, u, extra, outs, *, groups, n_true, normed, seq_tiles=0):
    rows = pl.ds(u * TQ, TQ)
    gain = extra[0][...]
    if seq_tiles:
        tm = outs[0].shape[2]
        pos = (pl.program_id(0) % seq_tiles) * tm + u * TQ + lax.broadcasted_iota(jnp.int32, (TQ, 1), 0)
        slot = n_true + jnp.right_shift(pos, int(math.log2(SEL_BLOCK)))
        block_lane = lax.broadcasted_iota(jnp.int32, (TQ, DKP), 1) == slot
    for gi in range(groups):
        k = acc[:, gi * DKP:(gi + 1) * DKP]
        v = acc[:, groups * DKP + gi * DV:groups * DKP + (gi + 1) * DV]
        if normed:
            ms = jnp.sum(k * k, axis=-1, keepdims=True) * (1.0 / n_true)
            kn = k * lax.rsqrt(ms + EPS) * gain
            if seq_tiles:
                kn = jnp.where(block_lane, 1.0, kn)
            outs[0][0, gi, rows, :] = kn.astype(outs[0].dtype)
            outs[1][0, gi, u] = v.T.astype(outs[1].dtype)
        else:
            outs[0][0, gi, rows, :] = k
            outs[1][0, gi, rows, :] = v


def _rope_lanes(x, cos2, sin2):
    half = MLA_ROPE_DIM // 2
    lane = lax.broadcasted_iota(jnp.int32, x.shape, 1)
    up = pltpu.roll(x, V7X_LANES - half, axis=1)
    down = pltpu.roll(x, half, axis=1)
    rot = jnp.where(lane < half, up, down)
    return x * cos2 + rot * sin2


def _rope_rows(x, cos2, sin2):
    half = MLA_ROPE_DIM // 2
    rot = jnp.concatenate([x[half:], x[:half]], axis=0)
    return x * cos2 + rot * sin2


def _ep_q_heads(acc, u, extra, outs, *, hb, n_true, scale, rope):
    acc_t = acc.T
    gain = extra[0][...]
    zeros = jnp.zeros((DKP - n_true, TQ), F32)
    for r in range(hb):
        y = acc_t[r * n_true:(r + 1) * n_true]
        ms = jnp.sum(y * y, axis=0, keepdims=True) * (1.0 / n_true)
        yn = y * lax.rsqrt(ms + EPS) * gain
        if rope:
            nope = n_true - MLA_ROPE_DIM
            yn = jnp.concatenate([yn[:nope], _rope_rows(yn[nope:], extra[1][u], extra[2][u])], axis=0)
        outs[0][0, r, u] = (jnp.concatenate([yn, zeros], axis=0) * scale).astype(outs[0].dtype)


def _ep_gate_heads(acc, u, extra, outs, *, hb, gated):
    acc_t = acc.T
    for r in range(hb):
        a = _silu(acc_t[r * DV:(r + 1) * DV])
        if gated:
            a = a * extra[0][0, u, r:r + 1, :]
        outs[0][r, u] = a.astype(outs[0].dtype)


def _ep_mla_kv(acc, u, extra, outs, *, hb):
    rows = pl.ds(u * TQ, TQ)
    kpe = extra[0][rows, :]
    gain_n = extra[1][...]
    gain_r = extra[2][...]
    ss_pe = jnp.sum(kpe * kpe, axis=-1, keepdims=True)
    kr = _rope_lanes(kpe * gain_r, extra[3][rows, :], extra[4][rows, :])
    inv_n = 1.0 / (MLA_NOPE_DIM + MLA_ROPE_DIM)
    for r in range(hb):
        kn = acc[:, r * DKP:r * DKP + V7X_LANES]
        v = acc[:, r * DKP + V7X_LANES:(r + 1) * DKP]
        rs = lax.rsqrt((jnp.sum(kn * kn, axis=-1, keepdims=True) + ss_pe) * inv_n + EPS)
        outs[0][0, r, rows, :] = jnp.concatenate([kn * rs * gain_n, kr * rs], axis=1).astype(outs[0].dtype)
        outs[1][0, r, u] = v.T.astype(outs[1].dtype)


def _compress_kernel(xa_ref, xb_ref, pea_ref, peb_ref, w1a_ref, w1b_ref, w2_ref, g_ref, o_ref, *, norm_n):
    xa = (xa_ref[0, 0] + pea_ref[...]).astype(BF16)
    xb = (xb_ref[0, 0] + peb_ref[...]).astype(BF16)
    pre = _dot(xa, w1a_ref[...]) + _dot(xb, w1b_ref[...])
    y = _dot(_silu(pre).astype(BF16), w2_ref[...])
    if norm_n:
        ms = jnp.sum(y * y, axis=-1, keepdims=True) * (1.0 / norm_n)
        y = y * lax.rsqrt(ms + EPS) * g_ref[...]
    o_ref[0, 0] = y.astype(o_ref.dtype)


def compress(x_raw, pe, w1, w2, gain, norm_n, name):
    b, g, s, dp = x_raw.shape
    d = pe.shape[1]
    nch = s // CMP_STRIDE
    half = CMP_STRIDE * dp
    chunks = x_raw.reshape(b, g, nch, half)
    xb = jnp.concatenate([chunks[:, :, 1:], jnp.zeros((b, g, 1, half), F32)], axis=2)
    pe_p = jnp.pad(pe, ((0, 0), (0, dp - d)))
    pea = pe_p[:CMP_STRIDE].reshape(1, half)
    peb = pe_p[CMP_STRIDE:].reshape(1, half)
    w1p = jnp.pad(w1.reshape(CMP_BLOCK, d, d), ((0, 0), (0, dp - d), (0, dp - d))).astype(BF16)
    w1a = w1p[:CMP_STRIDE].reshape(half, dp)
    w1b = w1p[CMP_STRIDE:].reshape(half, dp)
    w2p = jnp.pad(w2, ((0, dp - d), (0, dp - d))).astype(BF16)
    gp = jnp.pad(gain, (0, dp - d)).reshape(1, dp).astype(F32)
    full = lambda shape: pl.BlockSpec(shape, lambda bi, gi: (0,) * len(shape))
    return pl.pallas_call(
        functools.partial(_compress_kernel, norm_n=norm_n),
        grid=(b, g),
        in_specs=[
            pl.BlockSpec((1, 1, nch, half), lambda bi, gi: (bi, gi, 0, 0)),
            pl.BlockSpec((1, 1, nch, half), lambda bi, gi: (bi, gi, 0, 0)),
            full((1, half)), full((1, half)), full((half, dp)), full((half, dp)), full((dp, dp)), full((1, dp)),
        ],
        out_specs=pl.BlockSpec((1, 1, nch, dp), lambda bi, gi: (bi, gi, 0, 0)),
        out_shape=jax.ShapeDtypeStruct((b, g, nch, dp), BF16),
        compiler_params=_cparams(("parallel", "arbitrary")),
        name=name,
    )(chunks, xb, pea, peb, w1a, w1b, w2p, gp)


def _attn_init(m_ref, acc_ref):
    m_ref[...] = jnp.full(m_ref.shape, M_INIT, F32)
    acc_ref[...] = jnp.zeros(acc_ref.shape, F32)


def _attn_update(st, vts, add, m_ref, acc_ref):
    ones = jnp.ones((SUM_ROWS, TQ), BF16)
    for j, vt in enumerate(vts):
        sj = st[j * TQ:(j + 1) * TQ]
        if add is not None and add[j] is not None:
            sj = sj + add[j]
        m_prev = m_ref[...]
        m_new = jnp.maximum(m_prev, jnp.max(sj, axis=0, keepdims=True))
        alpha = jnp.exp2(m_prev - m_new)
        p = jnp.exp2(sj - m_new).astype(BF16)
        acc_ref[...] = alpha * acc_ref[...] + _dot(jnp.concatenate([vt, ones], axis=0), p)
        m_ref[...] = m_new


def _attn_result(acc_ref):
    acc = acc_ref[...]
    return acc[:DV] / acc[DV:DV + 1]


def _tile_ds(idx, size):
    return pl.ds(pl.multiple_of(idx * size, size), size)


def _nsa_attn_kernel(qt_ref, kc_ref, vct_ref, ks_ref, vst_ref, kw_ref, vwt_ref, ac_ref, as_ref, aw_ref,
                     cb_ref, t0_ref, t1_ref, ovl_ref, o_ref,
                     m_scr, acc_scr, oc_scr, qa_scr, cmpb_scr, *, heads, n_cmp_pad, n_sel, k_sel):
    nh = NSA_HEADS_PER_ITER
    i = pl.program_id(2)
    tok = i * TQ + lax.broadcasted_iota(jnp.int32, (1, TQ), 1)
    per_tile = TQ // CMP_STRIDE
    band = cb_ref.shape[1]

    valid_c = (tok >= CMP_BLOCK - 1).astype(F32)
    nrow = lax.broadcasted_iota(jnp.int32, (per_tile + n_cmp_pad, TQ), 0)
    base = jnp.where(nrow >= (i + 2) * per_tile, NEG, 0.0)
    nc = cmpb_scr.shape[0]
    for u in range(nc):
        cmpb_scr[u] = base
    kc = kc_ref[0, 0]
    vct = vct_ref[0, 0]

    def cmp_group(gi, psum):
        rs = [gi * nc + u for u in range(nc)]
        for u, r in enumerate(rs):
            cmpb_scr[u, pl.ds(pl.multiple_of(i * per_tile, per_tile), band), :] = cb_ref[r]
        st_all = _dot(kc, jnp.concatenate([qt_ref[0, r, 0] for r in rs], axis=1))
        for u, r in enumerate(rs):
            st = st_all[:, u * TQ:(u + 1) * TQ] + cmpb_scr[u, per_tile:per_tile + n_cmp_pad, :]
            p = jnp.exp2(st - jnp.max(st, axis=0, keepdims=True))
            pn = p * (valid_c / jnp.sum(p, axis=0, keepdims=True))
            oc_scr[r] = _dot(vct, pn.astype(BF16)) * ac_ref[r, 0].astype(F32)
            psum = psum + pn
        return psum

    psum = lax.fori_loop(0, heads // nc, cmp_group, jnp.zeros((n_cmp_pad, TQ), F32))

    ovl = ovl_ref[...]
    p_hi = psum.astype(BF16)
    rem = psum - p_hi.astype(F32)
    p_mid = rem.astype(BF16)
    p_lo = (rem - p_mid.astype(F32)).astype(BF16)
    imp = _dot(ovl, p_hi) + _dot(ovl, p_mid) + _dot(ovl, p_lo)

    n_blk = DKP - NSA_QK_DIM
    blk = lax.broadcasted_iota(jnp.int32, (n_blk, TQ), 0)
    blk_f = blk.astype(F32)
    cur = jnp.right_shift(tok, int(math.log2(SEL_BLOCK)))
    forced = (blk == 0) | (blk == cur) | (blk == cur - 1)
    future = blk * SEL_BLOCK > tok
    val = jnp.where(forced, 1e30, jnp.where(future, -1.0, imp))
    val = jnp.where(blk < n_sel, val, -2.0)

    def pick(_, c):
        v, sel = c
        top = jnp.max(v, axis=0, keepdims=True)
        first = jnp.min(jnp.where(v == top, blk_f, float(n_blk)), axis=0, keepdims=True)
        hit = blk_f == first
        return jnp.where(hit, -3.0, v), jnp.where(hit, 1.0, sel)

    _, sel = lax.fori_loop(0, k_sel, pick, (val, jnp.zeros((n_blk, TQ), F32)))

    mask_rows = ((sel - 1.0) * (-NEG)).astype(BF16)
    for r in range(heads):
        qa_scr[r] = jnp.concatenate([qt_ref[0, r, 0, 0:NSA_QK_DIM, :], mask_rows], axis=0)

    ft = FAR_TILES
    n_far = jnp.maximum(i - 1, 0) // ft
    kj = lax.broadcasted_iota(jnp.int32, (TQ, TQ), 0)
    qi = lax.broadcasted_iota(jnp.int32, (TQ, TQ), 1)
    half_t = TQ // 2
    zb = jnp.zeros((half_t, half_t), F32)
    negb = jnp.full((half_t, half_t), NEG, F32)

    sel_state = (m_scr.at[0], acc_scr.at[0])
    win_state = (m_scr.at[1], acc_scr.at[1])
    _attn_init(*sel_state)
    _attn_init(*win_state)

    def scores(k_rows, rs, masked):
        qs = [qa_scr[r] if masked else qt_ref[0, r, 0] for r in rs]
        return _dot(k_rows, jnp.concatenate(qs, axis=1))

    def update_all(st_all, vts, adds, state, rs):
        for u, r in enumerate(rs):
            _attn_update(st_all[:, u * TQ:(u + 1) * TQ], vts, adds[u], *(s.at[r] for s in state))

    def far(c, _):
        k_rows = ks_ref[0, 0, _tile_ds(c, ft * TQ), :]
        vts = tuple(vst_ref[0, 0, ft * c + j] for j in range(ft))
        for g0 in range(0, heads, nh):
            rs = list(range(g0, g0 + nh))
            update_all(scores(k_rows, rs, True), vts, [None] * nh, sel_state, rs)
        return 0

    lax.fori_loop(0, n_far, far, 0)

    def head_group(gi, _):
        rs = [gi * nh + u for u in range(nh)]

        def near(n_s, n_w):
            t0s = [t0_ref[r] for r in rs]
            t1s = [t1_ref[r] for r in rs]
            prev = [jnp.concatenate([jnp.concatenate([zb, zb], axis=1),
                                     jnp.concatenate([t1, zb], axis=1)], axis=0) for t1 in t1s]
            diag = [jnp.concatenate([jnp.concatenate([t0, t1], axis=1),
                                     jnp.concatenate([negb, t0], axis=1)], axis=0) for t0, t1 in zip(t0s, t1s)]
            w0 = jnp.where(kj > qi, 0.0, NEG)
            st_s = scores(ks_ref[0, 0, pl.ds(pl.multiple_of((i - n_s + 1) * TQ, TQ), n_s * TQ), :], rs, True)
            st_w = scores(kw_ref[0, 0, pl.ds(pl.multiple_of((i - n_w + 1) * TQ, TQ), n_w * TQ), :], rs, False)
            adds_s, adds_w = [], []
            for u in range(nh):
                parts_s = [None] * n_s
                parts_s[-1] = diag[u]
                parts_w = [diag[u]]
                if n_s >= 2:
                    parts_s[-2] = prev[u]
                if n_w >= 2:
                    parts_w.insert(0, prev[u])
                if n_w >= 3:
                    parts_w.insert(0, w0)
                adds_s.append(parts_s)
                adds_w.append(parts_w)
            update_all(st_s, tuple(vst_ref[0, 0, i - n_s + 1 + j] for j in range(n_s)), adds_s, sel_state, rs)
            update_all(st_w, tuple(vwt_ref[0, 0, i - n_w + 1 + j] for j in range(n_w)), adds_w, win_state, rs)

        pl.when(i == 0)(lambda: near(1, 1))
        pl.when(i == 1)(lambda: near(2, 2))
        for k in range(ft):
            pl.when((i >= 2) & ((i - 1) % ft == k))(functools.partial(near, 2 + k, 3))

        for u, r in enumerate(rs):
            ot = (oc_scr[r] + _attn_result(acc_scr.at[0, r]) * as_ref[r, 0].astype(F32)
                  + _attn_result(acc_scr.at[1, r]) * aw_ref[r, 0].astype(F32))
            o_ref[r] = ot.T.astype(o_ref.dtype)
        return 0

    lax.fori_loop(0, heads // nh, head_group, 0)


def _t5_bucket(dist):
    n = jnp.maximum(dist, 0)
    max_exact = REL_BUCKETS // 2
    scaled = (jnp.log(jnp.maximum(n, max_exact).astype(F32) / max_exact)
              / math.log(REL_MAX_DISTANCE / max_exact))
    large = jnp.minimum(max_exact + (scaled * (REL_BUCKETS - max_exact)).astype(jnp.int32), REL_BUCKETS - 1)
    return jnp.where(n < max_exact, n, large)


def _bias_table(rel_bias, dist, shift):
    onehot = (_t5_bucket(dist)[None] == jnp.arange(REL_BUCKETS).reshape((-1,) + (1,) * dist.ndim)).astype(F32)
    vals = jnp.tensordot(rel_bias.T, onehot, axes=1, precision=lax.Precision.HIGHEST)
    return jnp.where(dist >= 0, (vals - shift) * LOG2E, NEG).astype(F32)


def nsa_attention(qt, kc, vct, ks, vst, kw, vwt, gates, rel_bias):
    b, h, nq = qt.shape[:3]
    s = nq * TQ
    g = kc.shape[1]
    heads = h // g
    n_cmp_pad = kc.shape[2]
    n_sel = s // SEL_BLOCK
    k_sel = min(SEL_TOPK, n_sel)
    half_t = TQ // 2
    nh = NSA_HEADS_PER_ITER
    nc = min(NSA_CMP_HEADS_PER_ITER, heads)
    assert heads % nh == 0 and heads % nc == 0
    n_blk = DKP - NSA_QK_DIM
    assert s % (2 * TQ) == 0 and WINDOW == 2 * TQ and n_sel <= n_blk and TQ % SEL_BLOCK == 0
    assert half_t >= REL_MAX_DISTANCE
    assert n_cmp_pad == s // CMP_STRIDE and n_cmp_pad % V7X_LANES == 0

    far = rel_bias[REL_BUCKETS - 1][:, None, None]
    per_tile = TQ // CMP_STRIDE
    band = 2 * per_tile
    assert CMP_STRIDE * (per_tile + 1) - (CMP_BLOCK - 1) >= REL_MAX_DISTANCE and n_cmp_pad % per_tile == 0
    a_rel = jnp.arange(band) - per_tile
    cb = _bias_table(rel_bias, jnp.arange(TQ)[None, :] - CMP_STRIDE * a_rel[:, None] - (CMP_BLOCK - 1), far)
    ij = jnp.arange(half_t)
    t0 = _bias_table(rel_bias, ij[None, :] - ij[:, None], far)
    t1 = _bias_table(rel_bias, half_t + ij[None, :] - ij[:, None], far)

    n_cmp = (s - CMP_BLOCK) // CMP_STRIDE + 1
    cmp_start = CMP_STRIDE * np.arange(n_cmp_pad)
    sel_start = SEL_BLOCK * np.arange(n_blk)
    ovl = ((cmp_start[None, :] < (sel_start + SEL_BLOCK)[:, None])
           & ((cmp_start + CMP_BLOCK)[None, :] > sel_start[:, None])
           & (np.arange(n_cmp_pad) < n_cmp)[None, :] & (np.arange(n_blk) < n_sel)[:, None])
    ovl = jnp.asarray(ovl, BF16)

    once = pl.Buffered(1)
    rows_spec = lambda n, d: pl.BlockSpec((1, 1, n, d), lambda bi, gi, i: (bi, gi, 0, 0), pipeline_mode=once)
    vt_spec = pl.BlockSpec((1, 1, nq, DV, TQ), lambda bi, gi, i: (bi, gi, 0, 0, 0), pipeline_mode=once)
    gate_spec = lambda br: pl.BlockSpec((heads, 1, DV, TQ),
                                        lambda bi, gi, i, br=br: (br * g + gi, bi * nq + i, 0, 0))
    tbl_spec = lambda shape: pl.BlockSpec((heads,) + shape, lambda bi, gi, i: (gi, 0, 0), pipeline_mode=once)
    kern = functools.partial(_nsa_attn_kernel, heads=heads, n_cmp_pad=n_cmp_pad, n_sel=n_sel, k_sel=k_sel)
    return pl.pallas_call(
        kern,
        grid=(b, g, nq),
        in_specs=[
            pl.BlockSpec((1, heads, 1, DKP, TQ), lambda bi, gi, i: (bi, gi, i, 0, 0)),
            rows_spec(n_cmp_pad, DKP), rows_spec(DV, n_cmp_pad),
            rows_spec(s, DKP), vt_spec, rows_spec(s, DKP), vt_spec,
            gate_spec(0), gate_spec(1), gate_spec(2),
            tbl_spec((band, TQ)), tbl_spec((half_t, half_t)), tbl_spec((half_t, half_t)),
            pl.BlockSpec((n_blk, n_cmp_pad), lambda bi, gi, i: (0, 0)),
        ],
        out_specs=pl.BlockSpec((heads, TQ, DV), lambda bi, gi, i: (gi, bi * nq + i, 0)),
        out_shape=jax.ShapeDtypeStruct((h, b * s, DV), BF16),
        scratch_shapes=[
            pltpu.VMEM((2, heads, 1, TQ), F32), pltpu.VMEM((2, heads, DV + SUM_ROWS, TQ), F32),
            pltpu.VMEM((heads, DV, TQ), F32), pltpu.VMEM((heads, DKP, TQ), BF16),
            pltpu.VMEM((nc, TQ // CMP_STRIDE + n_cmp_pad, TQ), F32),
        ],
        compiler_params=_cparams(("parallel", "parallel", "arbitrary")),
        name="nsa_attention",
    )(qt, kc, vct, ks, vst, kw, vwt, gates, gates, gates, cb, t0, t1, ovl)


def _mla_attn_kernel(qt_ref, k_ref, vt_ref, z_ref, o_ref, m_scr, acc_scr, *, heads):
    i = pl.program_id(2)
    kj = lax.broadcasted_iota(jnp.int32, (TQ, TQ), 0)
    qi = lax.broadcasted_iota(jnp.int32, (TQ, TQ), 1)
    causal = jnp.where(kj <= qi, 0.0, NEG)

    ft = FAR_TILES

    def block(first, n_tiles, add):
        rows = pl.ds(pl.multiple_of(first * TQ, TQ), n_tiles * TQ)
        sts = [_dot(k_ref[0, r, rows, :], qt_ref[0, r, 0]) for r in range(heads)]
        for r in range(heads):
            vts = tuple(vt_ref[0, r, first + j] for j in range(n_tiles))
            _attn_update(sts[r], vts, add, m_scr.at[r], acc_scr.at[r])

    _attn_init(m_scr, acc_scr)

    def far(c, _):
        block(ft * c, ft, None)
        return 0

    lax.fori_loop(0, jnp.maximum(i - 1, 0) // ft, far, 0)

    def near(n_tiles):
        block(i - n_tiles + 1, n_tiles, [None] * (n_tiles - 1) + [causal])

    pl.when(i == 0)(functools.partial(near, 1))
    for k in range(ft):
        pl.when((i >= 1) & ((i - 1) % ft == k))(functools.partial(near, 2 + k))
    for r in range(heads):
        ot = _attn_result(acc_scr.at[r]) * z_ref[r, 0].astype(F32)
        o_ref[r] = ot.T.astype(o_ref.dtype)


def mla_attention(qt, k, vt, zg):
    b, h, nq = qt.shape[:3]
    s = nq * TQ
    hb = MLA_HEADS_PER_STEP
    assert s % (2 * TQ) == 0 and h % hb == 0
    return pl.pallas_call(
        functools.partial(_mla_attn_kernel, heads=hb),
        grid=(b, h // hb, nq),
        in_specs=[
            pl.BlockSpec((1, hb, 1, DKP, TQ), lambda bi, hi, i: (bi, hi, i, 0, 0)),
            pl.BlockSpec((1, hb, s, DKP), lambda bi, hi, i: (bi, hi, 0, 0)),
            pl.BlockSpec((1, hb, nq, DV, TQ), lambda bi, hi, i: (bi, hi, 0, 0, 0)),
            pl.BlockSpec((hb, 1, DV, TQ), lambda bi, hi, i: (hi, bi * nq + i, 0, 0)),
        ],
        out_specs=pl.BlockSpec((hb, TQ, DV), lambda bi, hi, i: (hi, bi * nq + i, 0)),
        out_shape=jax.ShapeDtypeStruct((h, b * s, DV), BF16),
        scratch_shapes=[pltpu.VMEM((hb, 1, TQ), F32), pltpu.VMEM((hb, DV + SUM_ROWS, TQ), F32)],
        compiler_params=_cparams(("parallel", "parallel", "arbitrary")),
        name="mla_attention",
    )(qt, k, vt, zg)


def _pad_heads(w, n_heads, d):
    k = w.shape[0]
    return jnp.pad(w.reshape(k, n_heads, d), ((0, 0), (0, 0), (0, DKP - d))).reshape(k, n_heads * DKP).astype(BF16)


def _head_rows_out(b, s, h, tm, hb, width, dtype):
    nt = s // tm
    spec = pl.BlockSpec((1, hb, tm, width), lambda i, j: (i // nt, j, i % nt, 0))
    return jax.ShapeDtypeStruct((b, h, s, width), dtype), spec


def _head_tiles_out(b, s, h, tm, hb, width, dtype):
    nt = s // tm
    spec = pl.BlockSpec((1, hb, tm // TQ, width, TQ), lambda i, j: (i // nt, j, i % nt, 0, 0))
    return jax.ShapeDtypeStruct((b, h, s // TQ, width, TQ), dtype), spec


def nsa_layer(x2, b, s, norm_w, rel_bias, w_in, q_norm, k_norm, pe_k, w1_k, w2_k, pe_v, w1_v, w2_v, w_out):
    t, d_model = x2.shape
    h, g, dk, dv = NSA_HEADS, NSA_KV_GROUPS, NSA_QK_DIM, NSA_V_DIM
    qw, kvw, zw = h * dk, g * (dk + dv), h * dv
    tm = _pick(s, 1024, TQ)
    hn = rmsnorm_rows(x2, norm_w, d_model, BF16, "nsa_prenorm")

    w_q = w_in[:, :qw].astype(BF16)
    hb = _pick(h, 4)
    q_shape, q_spec = _head_tiles_out(b, s, h, tm, hb, DKP, BF16)
    qt = matmul(hn, w_q, functools.partial(_ep_q_heads, hb=hb, n_true=dk, scale=LOG2E * dk ** -0.5, rope=False),
                tm=tm, tn=hb * dk, out_shapes=q_shape, out_specs=q_spec,
                extra=(jnp.broadcast_to(q_norm[:, None], (dk, TQ)),),
                extra_specs=(pl.BlockSpec((dk, TQ), lambda i, j: (0, 0)),), slabs=True, name="nsa_q_proj")

    def kv_proj(br, normed, block_onehot=False):
        w_br = w_in[:, qw + br * kvw: qw + (br + 1) * kvw]
        w_kv = jnp.concatenate([_pad_heads(w_br[:, :g * dk], g, dk), w_br[:, g * dk:].astype(BF16)], axis=1)
        k_out = _head_rows_out(b, s, g, tm, g, DKP, BF16 if normed else F32)
        v_out = _head_tiles_out(b, s, g, tm, g, dv, BF16) if normed else _head_rows_out(b, s, g, tm, g, dv, F32)
        gain = jnp.pad(k_norm[br], (0, DKP - dk)).reshape(1, DKP)
        ep = functools.partial(_ep_nsa_kv, groups=g, n_true=dk, normed=normed,
                               seq_tiles=s // tm if block_onehot else 0)
        return matmul(hn, w_kv, ep,
                      tm=tm, tn=g * (DKP + dv), out_shapes=(k_out[0], v_out[0]), out_specs=(k_out[1], v_out[1]),
                      extra=(gain,), extra_specs=(pl.BlockSpec((1, DKP), lambda i, j: (0, 0)),),
                      slabs=True, name="nsa_kv_proj_%d" % br)

    kc_raw, vc_raw = kv_proj(0, False)
    ks, vst = kv_proj(1, True, block_onehot=True)
    kw, vwt = kv_proj(2, True)
    kc = compress(kc_raw, pe_k, w1_k, w2_k, k_norm[0], dk, "nsa_compress_k")
    vc = compress(vc_raw, pe_v, w1_v, w2_v, jnp.ones((dv,), F32), 0, "nsa_compress_v")
    vct = jnp.swapaxes(vc, -1, -2)

    z0 = qw + 3 * kvw
    w_g = w_in[:, z0 + 3 * zw:].astype(BF16)
    sg = matmul(hn, w_g, _ep_sigmoid, tm=tm, tn=3 * h,
                out_shapes=jax.ShapeDtypeStruct((t, 3 * h), F32),
                out_specs=pl.BlockSpec((tm, 3 * h), lambda i, j: (i, j)), name="nsa_gate_proj")
    hbz = _pick(h, 8)
    n_zt = 3 * h // hbz
    sg_tiles = sg.reshape(t // TQ, TQ, n_zt, hbz).transpose(2, 0, 3, 1)
    w_z = w_in[:, z0:z0 + 3 * zw].astype(BF16)
    gates = matmul(hn, w_z, functools.partial(_ep_gate_heads, hb=hbz, gated=True), tm=tm, tn=hbz * dv,
                   out_shapes=jax.ShapeDtypeStruct((3 * h, t // TQ, dv, TQ), BF16),
                   out_specs=pl.BlockSpec((hbz, tm // TQ, dv, TQ), lambda i, j: (j, i, 0, 0)),
                   extra=(sg_tiles,),
                   extra_specs=(pl.BlockSpec((1, tm // TQ, hbz, TQ), lambda i, j: (j, i, 0, 0)),),
                   slabs=True, name="nsa_z_proj")

    o = nsa_attention(qt, kc, vct, ks, vst, kw, vwt, gates, rel_bias)
    tmo = _pick(t, 512, 8)
    tno = _pick(d_model, 512, V7X_LANES)
    return matmul(o, w_out.astype(BF16), _ep_residual, tm=tmo, tn=tno,
                  out_shapes=jax.ShapeDtypeStruct((t, d_model), F32),
                  out_specs=pl.BlockSpec((tmo, tno), lambda i, j: (i, j)),
                  extra=(x2,), extra_specs=(pl.BlockSpec((tmo, tno), lambda i, j: (i, j)),), name="nsa_out_proj")


def mla_layer(x2, b, s, positions, norm_w, w_in, q_a_norm, w_q_b, kv_a_norm, w_kv_b, q_norm, k_norm, w_out):
    t, d_model = x2.shape
    h = MLA_HEADS
    dqk = MLA_NOPE_DIM + MLA_ROPE_DIM
    tm = _pick(s, 1024, TQ)
    hn = rmsnorm_rows(x2, norm_w, d_model, BF16, "mla_prenorm")

    half = MLA_ROPE_DIM // 2
    inv_freq = ROPE_BASE ** (-jnp.arange(half, dtype=F32) / half)
    ang = positions.astype(F32).reshape(t, 1) * inv_freq[None, :]
    zeros = jnp.zeros((t, V7X_LANES - MLA_ROPE_DIM), F32)
    cos2 = jnp.concatenate([jnp.cos(ang), jnp.cos(ang), zeros], axis=1)
    sin2 = jnp.concatenate([-jnp.sin(ang), jnp.sin(ang), zeros], axis=1)
    rope_spec = pl.BlockSpec((tm, V7X_LANES), lambda i, j: (i, 0))
    cos2_t = jnp.swapaxes(cos2[:, :MLA_ROPE_DIM].reshape(t // TQ, TQ, MLA_ROPE_DIM), 1, 2)
    sin2_t = jnp.swapaxes(sin2[:, :MLA_ROPE_DIM].reshape(t // TQ, TQ, MLA_ROPE_DIM), 1, 2)
    rope_t_spec = pl.BlockSpec((tm // TQ, MLA_ROPE_DIM, TQ), lambda i, j: (i, 0, 0))
    row_spec = lambda w: pl.BlockSpec((1, w), lambda i, j: (0, 0))

    c0, c1, c2 = MLA_Q_LORA, MLA_Q_LORA + MLA_KV_LORA, MLA_Q_LORA + MLA_KV_LORA + MLA_ROPE_DIM
    w_lat = jnp.concatenate([w_in[:, :c1], jnp.pad(w_in[:, c1:c2], ((0, 0), (0, DKP - MLA_ROPE_DIM)))],
                            axis=1).astype(BF16)
    lat_w = c1 + DKP
    tml = _pick(s, 512, 8)
    row_out = lambda w, dt: (jax.ShapeDtypeStruct((t, w), dt), pl.BlockSpec((tml, w), lambda i, j: (i, 0)))
    lat_outs = (row_out(c0, BF16), row_out(c1 - c0, BF16), row_out(V7X_LANES, F32))
    cq, ckv, kpe = matmul(hn, w_lat, functools.partial(_ep_mla_latent, c0=c0, c1=c1), tm=tml, tn=lat_w,
                          out_shapes=tuple(o[0] for o in lat_outs), out_specs=tuple(o[1] for o in lat_outs),
                          extra=(q_a_norm.reshape(1, c0), kv_a_norm.reshape(1, c1 - c0)),
                          extra_specs=(row_spec(c0), row_spec(c1 - c0)), w_once=True, name="mla_latent_proj")

    hbz = _pick(h, 8)
    zg = matmul(hn, w_in[:, c2:].astype(BF16), functools.partial(_ep_gate_heads, hb=hbz, gated=False),
                tm=tm, tn=hbz * DV, out_shapes=jax.ShapeDtypeStruct((h, t // TQ, DV, TQ), BF16),
                out_specs=pl.BlockSpec((hbz, tm // TQ, DV, TQ), lambda i, j: (j, i, 0, 0)),
                slabs=True, name="mla_z_proj")

    hb = _pick(h, 4)
    q_shape, q_spec = _head_tiles_out(b, s, h, tm, hb, DKP, BF16)
    qt = matmul(cq, w_q_b.astype(BF16),
                functools.partial(_ep_q_heads, hb=hb, n_true=dqk, scale=LOG2E * dqk ** -0.5, rope=True),
                tm=tm, tn=hb * dqk, out_shapes=q_shape, out_specs=q_spec,
                extra=(jnp.broadcast_to(q_norm[:, None], (dqk, TQ)), cos2_t, sin2_t),
                extra_specs=(pl.BlockSpec((dqk, TQ), lambda i, j: (0, 0)), rope_t_spec, rope_t_spec),
                slabs=True, name="mla_q_proj")

    k_shape, k_spec = _head_rows_out(b, s, h, tm, hb, DKP, BF16)
    v_shape, v_spec = _head_tiles_out(b, s, h, tm, hb, DV, BF16)
    gain_n = k_norm[:MLA_NOPE_DIM].reshape(1, MLA_NOPE_DIM)
    gain_r = jnp.pad(k_norm[MLA_NOPE_DIM:], (0, V7X_LANES - MLA_ROPE_DIM)).reshape(1, V7X_LANES)
    k, vt = matmul(ckv, w_kv_b.astype(BF16), functools.partial(_ep_mla_kv, hb=hb),
                   tm=tm, tn=hb * DKP, out_shapes=(k_shape, v_shape), out_specs=(k_spec, v_spec),
                   extra=(kpe, gain_n, gain_r, cos2, sin2),
                   extra_specs=(rope_spec, row_spec(MLA_NOPE_DIM), row_spec(V7X_LANES), rope_spec, rope_spec),
                   slabs=True, name="mla_kv_proj")

    o = mla_attention(qt, k, vt, zg)
    tmo = _pick(t, 512, 8)
    tno = _pick(d_model, 512, V7X_LANES)
    return matmul(o, w_out.astype(BF16), _ep_residual, tm=tmo, tn=tno,
                  out_shapes=jax.ShapeDtypeStruct((t, d_model), F32),
                  out_specs=pl.BlockSpec((tmo, tno), lambda i, j: (i, j)),
                  extra=(x2,), extra_specs=(pl.BlockSpec((tmo, tno), lambda i, j: (i, j)),), name="mla_out_proj")


def kernel(x, positions, norm_w, rel_bias, nsa_w_in, nsa_q_norm, nsa_k_norm, nsa_cmp_pe_k, nsa_cmp_w1_k,
           nsa_cmp_w2_k, nsa_cmp_pe_v, nsa_cmp_w1_v, nsa_cmp_w2_v, nsa_w_out, mla_w_in, mla_q_a_norm,
           mla_w_q_b, mla_kv_a_norm, mla_w_kv_b, mla_q_norm, mla_k_norm, mla_w_out):
    b, s, d_model = x.shape
    x2 = x.reshape(b * s, d_model)
    depth = norm_w.shape[0]
    for layer in range(depth):
        j = layer // 2
        if layer % 2 == 0:
            x2 = nsa_layer(x2, b, s, norm_w[layer], rel_bias, nsa_w_in[j], nsa_q_norm[j], nsa_k_norm[j],
                           nsa_cmp_pe_k[j], nsa_cmp_w1_k[j], nsa_cmp_w2_k[j], nsa_cmp_pe_v[j], nsa_cmp_w1_v[j],
                           nsa_cmp_w2_v[j], nsa_w_out[j])
        else:
            x2 = mla_layer(x2, b, s, positions, norm_w[layer], mla_w_in[j], mla_q_a_norm[j], mla_w_q_b[j],
                           mla_kv_a_norm[j], mla_w_kv_b[j], mla_q_norm[j], mla_k_norm[j], mla_w_out[j])
    return x2.reshape(b, s, d_model)
```

```python
import functools
import math

import numpy as np
import jax
import jax.numpy as jnp
from jax import lax
from jax.experimental import pallas as pl
from jax.experimental.pallas import tpu as pltpu

F32 = jnp.float32
BF16 = jnp.bfloat16

EPS = 1e-6
REL_BUCKETS = 32
REL_MAX_DISTANCE = 128
NSA_HEADS = 64
NSA_KV_GROUPS = 4
NSA_QK_DIM = 192
NSA_V_DIM = 128
CMP_BLOCK = 32
CMP_STRIDE = 16
SEL_BLOCK = 64
SEL_TOPK = 16
WINDOW = 512
MLA_HEADS = 64
MLA_Q_LORA = 1536
MLA_KV_LORA = 512
MLA_NOPE_DIM = 128
MLA_ROPE_DIM = 64
MLA_V_DIM = 128
ROPE_BASE = 10000.0

V7X_LANES = 128
V7X_VMEM_BYTES = 64 * 1024 * 1024
VMEM_LIMIT_BYTES = V7X_VMEM_BYTES - 8 * 1024 * 1024

DKP = 2 * V7X_LANES
DV = V7X_LANES
TQ = 256
NEG = -1e30
M_INIT = -1e20
SUM_ROWS = 16
LOG2E = math.log2(math.e)
MLA_HEADS_PER_STEP = 4
NSA_HEADS_PER_ITER = 4
NSA_CMP_HEADS_PER_ITER = 8
FAR_TILES = 4


def _cparams(sem):
    return pltpu.CompilerParams(dimension_semantics=sem, vmem_limit_bytes=VMEM_LIMIT_BYTES)


def _dot(a, b):
    return jnp.dot(a, b, preferred_element_type=F32)


def _sigmoid(x):
    return 1.0 / (1.0 + jnp.exp(-x))


def _silu(x):
    return x * _sigmoid(x)


def _pick(n, target, unit=1):
    best = unit
    c = unit
    while c <= min(n, max(target, unit)):
        if n % c == 0:
            best = c
        c += unit
    return best


def _rmsnorm_kernel(x_ref, g_ref, o_ref, *, inv_n):
    x = x_ref[...].astype(F32)
    ms = jnp.sum(x * x, axis=-1, keepdims=True) * inv_n
    o_ref[...] = (x * lax.rsqrt(ms + EPS) * g_ref[...]).astype(o_ref.dtype)


def rmsnorm_rows(x, gain, n_true, out_dtype, name):
    m, d = x.shape
    tm = _pick(m, max(8, (2 * 1024 * 1024) // (4 * d)), 8)
    return pl.pallas_call(
        functools.partial(_rmsnorm_kernel, inv_n=1.0 / n_true),
        grid=(m // tm,),
        in_specs=[pl.BlockSpec((tm, d), lambda i: (i, 0)), pl.BlockSpec((1, d), lambda i: (0, 0))],
        out_specs=pl.BlockSpec((tm, d), lambda i: (i, 0)),
        out_shape=jax.ShapeDtypeStruct((m, d), out_dtype),
        compiler_params=_cparams(("parallel",)),
        name=name,
    )(x, gain.reshape(1, d).astype(F32))


def _matmul_kernel(*refs, epilogue, n_in, slabs):
    a_ref, w_ref = refs[0], refs[1]
    extra, outs = refs[2:n_in], refs[n_in:]
    if len(a_ref.shape) == 3:
        a = jnp.concatenate([a_ref[h] for h in range(a_ref.shape[0])], axis=1)
    else:
        a = a_ref[...]
    w = w_ref[...]
    if not slabs:
        epilogue(_dot(a, w), extra, outs)
        return
    n = a.shape[0] // TQ
    acc = _dot(a[:TQ], w)
    for u in range(n):
        nxt = _dot(a[(u + 1) * TQ:(u + 2) * TQ], w) if u + 1 < n else None
        epilogue(acc, u, extra, outs)
        acc = nxt


def matmul(a, w, epilogue, *, tm, tn, out_shapes, out_specs, extra=(), extra_specs=(), slabs=False,
           w_once=False, name):
    n = w.shape[1]
    if a.ndim == 3:
        m, k = a.shape[1], a.shape[0] * a.shape[2]
        a_spec = pl.BlockSpec((a.shape[0], tm, a.shape[2]), lambda i, j: (0, i, 0))
    else:
        m, k = a.shape
        a_spec = pl.BlockSpec((tm, k), lambda i, j: (i, 0))
    assert m % tm == 0 and n % tn == 0 and k == w.shape[0], (m, tm, n, tn, k)
    assert not slabs or tm % TQ == 0
    assert not w_once or tn == n
    w_spec = pl.BlockSpec((k, tn), lambda i, j: (0, j), pipeline_mode=pl.Buffered(1) if w_once else None)
    in_specs = [a_spec, w_spec]
    in_specs += list(extra_specs)
    return pl.pallas_call(
        functools.partial(_matmul_kernel, epilogue=epilogue, n_in=2 + len(extra), slabs=slabs),
        grid=(m // tm, n // tn),
        in_specs=in_specs,
        out_specs=out_specs,
        out_shape=out_shapes,
        compiler_params=_cparams(("parallel", "arbitrary")),
        name=name,
    )(a, w, *extra)


def _ep_store(acc, extra, outs):
    outs[0][...] = acc.astype(outs[0].dtype)


def _ep_sigmoid(acc, extra, outs):
    outs[0][...] = _sigmoid(acc)


def _ep_residual(acc, extra, outs):
    outs[0][...] = extra[0][...] + acc


def _ep_mla_latent(acc, extra, outs, *, c0, c1):
    def normed(x, gain):
        ms = jnp.mean(x * x, axis=-1, keepdims=True)
        return x * lax.rsqrt(ms + EPS) * gain

    outs[0][...] = normed(acc[:, :c0], extra[0][...]).astype(outs[0].dtype)
    outs[1][...] = normed(acc[:, c0:c1], extra[1][...]).astype(outs[1].dtype)
    outs[2][...] = acc[:, c1:c1 + V7X_LANES]


def _ep_nsa_kv(acc, u, extra, outs, *, groups, n_true, normed, seq_tiles=0):
    rows = pl.ds(u * TQ, TQ)
    gain = extra[0][...]
    if seq_tiles:
        tm = outs[0].shape[2]
        pos = (pl.program_id(0) % seq_tiles) * tm + u * TQ + lax.broadcasted_iota(jnp.int32, (TQ, 1), 0)
        slot = n_true + jnp.right_shift(pos, int(math.log2(SEL_BLOCK)))
        block_lane = lax.broadcasted_iota(jnp.int32, (TQ, DKP), 1) == slot
    for gi in range(groups):
        k = acc[:, gi * DKP:(gi + 1) * DKP]
        v = acc[:, groups * DKP + gi * DV:groups * DKP + (gi + 1) * DV]
        if normed:
            ms = jnp.sum(k * k, axis=-1, keepdims=True) * (1.0 / n_true)
            kn = k * lax.rsqrt(ms + EPS) * gain
            if seq_tiles:
                kn = jnp.where(block_lane, 1.0, kn)
            outs[0][0, gi, rows, :] = kn.astype(outs[0].dtype)
            outs[1][0, gi, u] = v.T.astype(outs[1].dtype)
        else:
            outs[0][0, gi, rows, :] = k
            outs[1][0, gi, rows, :] = v


def _rope_lanes(x, cos2, sin2):
    half = MLA_ROPE_DIM // 2
    lane = lax.broadcasted_iota(jnp.int32, x.shape, 1)
    up = pltpu.roll(x, V7X_LANES - half, axis=1)
    down = pltpu.roll(x, half, axis=1)
    rot = jnp.where(lane < half, up, down)
    return x * cos2 + rot * sin2


def _rope_rows(x, cos2, sin2):
    half = MLA_ROPE_DIM // 2
    rot = jnp.concatenate([x[half:], x[:half]], axis=0)
    return x * cos2 + rot * sin2


def _ep_q_heads(acc, u, extra, outs, *, hb, n_true, scale, rope):
    acc_t = acc.T
    gain = extra[0][...]
    zeros = jnp.zeros((DKP - n_true, TQ), F32)
    for r in range(hb):
        y = acc_t[r * n_true:(r + 1) * n_true]
        ms = jnp.sum(y * y, axis=0, keepdims=True) * (1.0 / n_true)
        yn = y * lax.rsqrt(ms + EPS) * gain
        if rope:
            nope = n_true - MLA_ROPE_DIM
            yn = jnp.concatenate([yn[:nope], _rope_rows(yn[nope:], extra[1][u], extra[2][u])], axis=0)
        outs[0][0, r, u] = (jnp.concatenate([yn, zeros], axis=0) * scale).astype(outs[0].dtype)


def _ep_gate_heads(acc, u, extra, outs, *, hb, gated):
    acc_t = acc.T
    for r in range(hb):
        a = _silu(acc_t[r * DV:(r + 1) * DV])
        if gated:
            a = a * extra[0][0, u, r:r + 1, :]
        outs[0][r, u] = a.astype(outs[0].dtype)


def _ep_mla_kv(acc, u, extra, outs, *, hb):
    rows = pl.ds(u * TQ, TQ)
    kpe = extra[0][rows, :]
    gain_n = extra[1][...]
    gain_r = extra[2][...]
    ss_pe = jnp.sum(kpe * kpe, axis=-1, keepdims=True)
    kr = _rope_lanes(kpe * gain_r, extra[3][rows, :], extra[4][rows, :])
    inv_n = 1.0 / (MLA_NOPE_DIM + MLA_ROPE_DIM)
    for r in range(hb):
        kn = acc[:, r * DKP:r * DKP + V7X_LANES]
        v = acc[:, r * DKP + V7X_LANES:(r + 1) * DKP]
        rs = lax.rsqrt((jnp.sum(kn * kn, axis=-1, keepdims=True) + ss_pe) * inv_n + EPS)
        outs[0][0, r, rows, :] = jnp.concatenate([kn * rs * gain_n, kr * rs], axis=1).astype(outs[0].dtype)
        outs[1][0, r, u] = v.T.astype(outs[1].dtype)


def _compress_kernel(xa_ref, xb_ref, pea_ref, peb_ref, w1a_ref, w1b_ref, w2_ref, g_ref, o_ref, *, norm_n):
    xa = (xa_ref[0, 0] + pea_ref[...]).astype(BF16)
    xb = (xb_ref[0, 0] + peb_ref[...]).astype(BF16)
    pre = _dot(xa, w1a_ref[...]) + _dot(xb, w1b_ref[...])
    y = _dot(_silu(pre).astype(BF16), w2_ref[...])
    if norm_n:
        ms = jnp.sum(y * y, axis=-1, keepdims=True) * (1.0 / norm_n)
        y = y * lax.rsqrt(ms + EPS) * g_ref[...]
    o_ref[0, 0] = y.astype(o_ref.dtype)


def compress(x_raw, pe, w1, w2, gain, norm_n, name):
    b, g, s, dp = x_raw.shape
    d = pe.shape[1]
    nch = s // CMP_STRIDE
    half = CMP_STRIDE * dp
    chunks = x_raw.reshape(b, g, nch, half)
    xb = jnp.concatenate([chunks[:, :, 1:], jnp.zeros((b, g, 1, half), F32)], axis=2)
    pe_p = jnp.pad(pe, ((0, 0), (0, dp - d)))
    pea = pe_p[:CMP_STRIDE].reshape(1, half)
    peb = pe_p[CMP_STRIDE:].reshape(1, half)
    w1p = jnp.pad(w1.reshape(CMP_BLOCK, d, d), ((0, 0), (0, dp - d), (0, dp - d))).astype(BF16)
    w1a = w1p[:CMP_STRIDE].reshape(half, dp)
    w1b = w1p[CMP_STRIDE:].reshape(half, dp)
    w2p = jnp.pad(w2, ((0, dp - d), (0, dp - d))).astype(BF16)
    gp = jnp.pad(gain, (0, dp - d)).reshape(1, dp).astype(F32)
    full = lambda shape: pl.BlockSpec(shape, lambda bi, gi: (0,) * len(shape))
    return pl.pallas_call(
        functools.partial(_compress_kernel, norm_n=norm_n),
        grid=(b, g),
        in_specs=[
            pl.BlockSpec((1, 1, nch, half), lambda bi, gi: (bi, gi, 0, 0)),
            pl.BlockSpec((1, 1, nch, half), lambda bi, gi: (bi, gi, 0, 0)),
            full((1, half)), full((1, half)), full((half, dp)), full((half, dp)), full((dp, dp)), full((1, dp)),
        ],
        out_specs=pl.BlockSpec((1, 1, nch, dp), lambda bi, gi: (bi, gi, 0, 0)),
        out_shape=jax.ShapeDtypeStruct((b, g, nch, dp), BF16),
        compiler_params=_cparams(("parallel", "arbitrary")),
        name=name,
    )(chunks, xb, pea, peb, w1a, w1b, w2p, gp)


def _attn_init(m_ref, acc_ref):
    m_ref[...] = jnp.full(m_ref.shape, M_INIT, F32)
    acc_ref[...] = jnp.zeros(acc_ref.shape, F32)


def _attn_update(st, vts, add, m_ref, acc_ref):
    ones = jnp.ones((SUM_ROWS, TQ), BF16)
    for j, vt in enumerate(vts):
        sj = st[j * TQ:(j + 1) * TQ]
        if add is not None and add[j] is not None:
            sj = sj + add[j]
        m_prev = m_ref[...]
        m_new = jnp.maximum(m_prev, jnp.max(sj, axis=0, keepdims=True))
        alpha = jnp.exp2(m_prev - m_new)
        p = jnp.exp2(sj - m_new).astype(BF16)
        acc_ref[...] = alpha * acc_ref[...] + _dot(jnp.concatenate([vt, ones], axis=0), p)
        m_ref[...] = m_new


def _attn_result(acc_ref):
    acc = acc_ref[...]
    return acc[:DV] / acc[DV:DV + 1]


def _tile_ds(idx, size):
    return pl.ds(pl.multiple_of(idx * size, size), size)


def _nsa_attn_kernel(qt_ref, kc_ref, vct_ref, ks_ref, vst_ref, kw_ref, vwt_ref, ac_ref, as_ref, aw_ref,
                     cb_ref, t0_ref, t1_ref, ovl_ref, o_ref,
                     m_scr, acc_scr, oc_scr, qa_scr, cmpb_scr, *, heads, n_cmp_pad, n_sel, k_sel):
    nh = NSA_HEADS_PER_ITER
    i = pl.program_id(2)
    tok = i * TQ + lax.broadcasted_iota(jnp.int32, (1, TQ), 1)
    per_tile = TQ // CMP_STRIDE
    band = cb_ref.shape[1]

    valid_c = (tok >= CMP_BLOCK - 1).astype(F32)
    nrow = lax.broadcasted_iota(jnp.int32, (per_tile + n_cmp_pad, TQ), 0)
    base = jnp.where(nrow >= (i + 2) * per_tile, NEG, 0.0)
    nc = cmpb_scr.shape[0]
    for u in range(nc):
        cmpb_scr[u] = base
    kc = kc_ref[0, 0]
    vct = vct_ref[0, 0]

    def cmp_group(gi, psum):
        rs = [gi * nc + u for u in range(nc)]
        for u, r in enumerate(rs):
            cmpb_scr[u, pl.ds(pl.multiple_of(i * per_tile, per_tile), band), :] = cb_ref[r]
        st_all = _dot(kc, jnp.concatenate([qt_ref[0, r, 0] for r in rs], axis=1))
        for u, r in enumerate(rs):
            st = st_all[:, u * TQ:(u + 1) * TQ] + cmpb_scr[u, per_tile:per_tile + n_cmp_pad, :]
            p = jnp.exp2(st - jnp.max(st, axis=0, keepdims=True))
            pn = p * (valid_c / jnp.sum(p, axis=0, keepdims=True))
            oc_scr[r] = _dot(vct, pn.astype(BF16)) * ac_ref[r, 0].astype(F32)
            psum = psum + pn
        return psum

    psum = lax.fori_loop(0, heads // nc, cmp_group, jnp.zeros((n_cmp_pad, TQ), F32))

    ovl = ovl_ref[...]
    p_hi = psum.astype(BF16)
    rem = psum - p_hi.astype(F32)
    p_mid = rem.astype(BF16)
    p_lo = (rem - p_mid.astype(F32)).astype(BF16)
    imp = _dot(ovl, p_hi) + _dot(ovl, p_mid) + _dot(ovl, p_lo)

    n_blk = DKP - NSA_QK_DIM
    blk = lax.broadcasted_iota(jnp.int32, (n_blk, TQ), 0)
    blk_f = blk.astype(F32)
    cur = jnp.right_shift(tok, int(math.log2(SEL_BLOCK)))
    forced = (blk == 0) | (blk == cur) | (blk == cur - 1)
    future = blk * SEL_BLOCK > tok
    val = jnp.where(forced, 1e30, jnp.where(future, -1.0, imp))
    val = jnp.where(blk < n_sel, val, -2.0)

    def pick(_, c):
        v, sel = c
        top = jnp.max(v, axis=0, keepdims=True)
        first = jnp.min(jnp.where(v == top, blk_f, float(n_blk)), axis=0, keepdims=True)
        hit = blk_f == first
        return jnp.where(hit, -3.0, v), jnp.where(hit, 1.0, sel)

    _, sel = lax.fori_loop(0, k_sel, pick, (val, jnp.zeros((n_blk, TQ), F32)), unroll=True)

    mask_rows = ((sel - 1.0) * (-NEG)).astype(BF16)
    for r in range(heads):
        qa_scr[r] = jnp.concatenate([qt_ref[0, r, 0, 0:NSA_QK_DIM, :], mask_rows], axis=0)

    ft = FAR_TILES
    n_far = jnp.maximum(i - 1, 0) // ft
    kj = lax.broadcasted_iota(jnp.int32, (TQ, TQ), 0)
    qi = lax.broadcasted_iota(jnp.int32, (TQ, TQ), 1)
    half_t = TQ // 2
    zb = jnp.zeros((half_t, half_t), F32)
    negb = jnp.full((half_t, half_t), NEG, F32)

    sel_state = (m_scr.at[0], acc_scr.at[0])
    win_state = (m_scr.at[1], acc_scr.at[1])
    _attn_init(*sel_state)
    _attn_init(*win_state)

    def scores(k_rows, rs, masked):
        qs = [qa_scr[r] if masked else qt_ref[0, r, 0] for r in rs]
        return _dot(k_rows, jnp.concatenate(qs, axis=1))

    def update_all(st_all, vts, adds, state, rs):
        for u, r in enumerate(rs):
            _attn_update(st_all[:, u * TQ:(u + 1) * TQ], vts, adds[u], *(s.at[r] for s in state))

    def far(c, _):
        k_rows = ks_ref[0, 0, _tile_ds(c, ft * TQ), :]
        vts = tuple(vst_ref[0, 0, ft * c + j] for j in range(ft))
        for g0 in range(0, heads, nh):
            rs = list(range(g0, g0 + nh))
            update_all(scores(k_rows, rs, True), vts, [None] * nh, sel_state, rs)
        return 0

    lax.fori_loop(0, n_far, far, 0)

    def near(n_s, n_w):
        w0 = jnp.where(kj > qi, 0.0, NEG)
        for g0 in range(0, heads, nh):
            rs = list(range(g0, g0 + nh))
            t0s = [t0_ref[r] for r in rs]
            t1s = [t1_ref[r] for r in rs]
            prev = [jnp.concatenate([jnp.concatenate([zb, zb], axis=1),
                                     jnp.concatenate([t1, zb], axis=1)], axis=0) for t1 in t1s]
            diag = [jnp.concatenate([jnp.concatenate([t0, t1], axis=1),
                                     jnp.concatenate([negb, t0], axis=1)], axis=0) for t0, t1 in zip(t0s, t1s)]
            st_s = scores(ks_ref[0, 0, pl.ds(pl.multiple_of((i - n_s + 1) * TQ, TQ), n_s * TQ), :], rs, True)
            st_w = scores(kw_ref[0, 0, pl.ds(pl.multiple_of((i - n_w + 1) * TQ, TQ), n_w * TQ), :], rs, False)
            adds_s, adds_w = [], []
            for u in range(nh):
                parts_s = [None] * n_s
                parts_s[-1] = diag[u]
                parts_w = [diag[u]]
                if n_s >= 2:
                    parts_s[-2] = prev[u]
                if n_w >= 2:
                    parts_w.insert(0, prev[u])
                if n_w >= 3:
                    parts_w.insert(0, w0)
                adds_s.append(parts_s)
                adds_w.append(parts_w)
            update_all(st_s, tuple(vst_ref[0, 0, i - n_s + 1 + j] for j in range(n_s)), adds_s, sel_state, rs)
            update_all(st_w, tuple(vwt_ref[0, 0, i - n_w + 1 + j] for j in range(n_w)), adds_w, win_state, rs)

    pl.when(i == 0)(lambda: near(1, 1))
    pl.when(i == 1)(lambda: near(2, 2))
    for k in range(ft):
        pl.when((i >= 2) & ((i - 1) % ft == k))(functools.partial(near, 2 + k, 3))

    def finish(r, _):
        ot = (oc_scr[r] + _attn_result(acc_scr.at[0, r]) * as_ref[r, 0].astype(F32)
              + _attn_result(acc_scr.at[1, r]) * aw_ref[r, 0].astype(F32))
        o_ref[r] = ot.T.astype(o_ref.dtype)
        return 0

    lax.fori_loop(0, heads, finish, 0)


def _t5_bucket(dist):
    n = jnp.maximum(dist, 0)
    max_exact = REL_BUCKETS // 2
    scaled = (jnp.log(jnp.maximum(n, max_exact).astype(F32) / max_exact)
              / math.log(REL_MAX_DISTANCE / max_exact))
    large = jnp.minimum(max_exact + (scaled * (REL_BUCKETS - max_exact)).astype(jnp.int32), REL_BUCKETS - 1)
    return jnp.where(n < max_exact, n, large)


def _bias_table(rel_bias, dist, shift):
    onehot = (_t5_bucket(dist)[None] == jnp.arange(REL_BUCKETS).reshape((-1,) + (1,) * dist.ndim)).astype(F32)
    vals = jnp.tensordot(rel_bias.T, onehot, axes=1, precision=lax.Precision.HIGHEST)
    return jnp.where(dist >= 0, (vals - shift) * LOG2E, NEG).astype(F32)


def nsa_attention(qt, kc, vct, ks, vst, kw, vwt, gates, rel_bias):
    b, h, nq = qt.shape[:3]
    s = nq * TQ
    g = kc.shape[1]
    heads = h // g
    n_cmp_pad = kc.shape[2]
    n_sel = s // SEL_BLOCK
    k_sel = min(SEL_TOPK, n_sel)
    half_t = TQ // 2
    nh = NSA_HEADS_PER_ITER
    nc = min(NSA_CMP_HEADS_PER_ITER, heads)
    assert heads % nh == 0 and heads % nc == 0
    n_blk = DKP - NSA_QK_DIM
    assert s % (2 * TQ) == 0 and WINDOW == 2 * TQ and n_sel <= n_blk and TQ % SEL_BLOCK == 0
    assert half_t >= REL_MAX_DISTANCE
    assert n_cmp_pad == s // CMP_STRIDE and n_cmp_pad % V7X_LANES == 0

    far = rel_bias[REL_BUCKETS - 1][:, None, None]
    per_tile = TQ // CMP_STRIDE
    band = 2 * per_tile
    assert CMP_STRIDE * (per_tile + 1) - (CMP_BLOCK - 1) >= REL_MAX_DISTANCE and n_cmp_pad % per_tile == 0
    a_rel = jnp.arange(band) - per_tile
    cb = _bias_table(rel_bias, jnp.arange(TQ)[None, :] - CMP_STRIDE * a_rel[:, None] - (CMP_BLOCK - 1), far)
    ij = jnp.arange(half_t)
    t0 = _bias_table(rel_bias, ij[None, :] - ij[:, None], far)
    t1 = _bias_table(rel_bias, half_t + ij[None, :] - ij[:, None], far)

    n_cmp = (s - CMP_BLOCK) // CMP_STRIDE + 1
    cmp_start = CMP_STRIDE * np.arange(n_cmp_pad)
    sel_start = SEL_BLOCK * np.arange(n_blk)
    ovl = ((cmp_start[None, :] < (sel_start + SEL_BLOCK)[:, None])
           & ((cmp_start + CMP_BLOCK)[None, :] > sel_start[:, None])
           & (np.arange(n_cmp_pad) < n_cmp)[None, :] & (np.arange(n_blk) < n_sel)[:, None])
    ovl = jnp.asarray(ovl, BF16)

    once = pl.Buffered(1)
    rows_spec = lambda n, d: pl.BlockSpec((1, 1, n, d), lambda bi, gi, i: (bi, gi, 0, 0), pipeline_mode=once)
    vt_spec = pl.BlockSpec((1, 1, nq, DV, TQ), lambda bi, gi, i: (bi, gi, 0, 0, 0), pipeline_mode=once)
    gate_spec = lambda br: pl.BlockSpec((heads, 1, DV, TQ),
                                        lambda bi, gi, i, br=br: (br * g + gi, bi * nq + i, 0, 0))
    tbl_spec = lambda shape: pl.BlockSpec((heads,) + shape, lambda bi, gi, i: (gi, 0, 0), pipeline_mode=once)
    kern = functools.partial(_nsa_attn_kernel, heads=heads, n_cmp_pad=n_cmp_pad, n_sel=n_sel, k_sel=k_sel)
    return pl.pallas_call(
        kern,
        grid=(b, g, nq),
        in_specs=[
            pl.BlockSpec((1, heads, 1, DKP, TQ), lambda bi, gi, i: (bi, gi, i, 0, 0)),
            rows_spec(n_cmp_pad, DKP), rows_spec(DV, n_cmp_pad),
            rows_spec(s, DKP), vt_spec, rows_spec(s, DKP), vt_spec,
            gate_spec(0), gate_spec(1), gate_spec(2),
            tbl_spec((band, TQ)), tbl_spec((half_t, half_t)), tbl_spec((half_t, half_t)),
            pl.BlockSpec((n_blk, n_cmp_pad), lambda bi, gi, i: (0, 0)),
        ],
        out_specs=pl.BlockSpec((heads, TQ, DV), lambda bi, gi, i: (gi, bi * nq + i, 0)),
        out_shape=jax.ShapeDtypeStruct((h, b * s, DV), BF16),
        scratch_shapes=[
            pltpu.VMEM((2, heads, 1, TQ), F32), pltpu.VMEM((2, heads, DV + SUM_ROWS, TQ), F32),
            pltpu.VMEM((heads, DV, TQ), F32), pltpu.VMEM((heads, DKP, TQ), BF16),
            pltpu.VMEM((nc, TQ // CMP_STRIDE + n_cmp_pad, TQ), F32),
        ],
        compiler_params=_cparams(("parallel", "parallel", "arbitrary")),
        name="nsa_attention",
    )(qt, kc, vct, ks, vst, kw, vwt, gates, gates, gates, cb, t0, t1, ovl)


def _mla_attn_kernel(qt_ref, k_ref, vt_ref, z_ref, o_ref, m_scr, acc_scr, *, heads):
    i = pl.program_id(2)
    kj = lax.broadcasted_iota(jnp.int32, (TQ, TQ), 0)
    qi = lax.broadcasted_iota(jnp.int32, (TQ, TQ), 1)
    causal = jnp.where(kj <= qi, 0.0, NEG)

    ft = FAR_TILES

    def block(first, n_tiles, add):
        rows = pl.ds(pl.multiple_of(first * TQ, TQ), n_tiles * TQ)
        sts = [_dot(k_ref[0, r, rows, :], qt_ref[0, r, 0]) for r in range(heads)]
        for r in range(heads):
            vts = tuple(vt_ref[0, r, first + j] for j in range(n_tiles))
            _attn_update(sts[r], vts, add, m_scr.at[r], acc_scr.at[r])

    _attn_init(m_scr, acc_scr)

    def far(c, _):
        block(ft * c, ft, None)
        return 0

    lax.fori_loop(0, jnp.maximum(i - 1, 0) // ft, far, 0)

    def near(n_tiles):
        block(i - n_tiles + 1, n_tiles, [None] * (n_tiles - 1) + [causal])

    pl.when(i == 0)(functools.partial(near, 1))
    for k in range(ft):
        pl.when((i >= 1) & ((i - 1) % ft == k))(functools.partial(near, 2 + k))
    for r in range(heads):
        ot = _attn_result(acc_scr.at[r]) * z_ref[r, 0].astype(F32)
        o_ref[r] = ot.T.astype(o_ref.dtype)


def mla_attention(qt, k, vt, zg):
    b, h, nq = qt.shape[:3]
    s = nq * TQ
    hb = MLA_HEADS_PER_STEP
    assert s % (2 * TQ) == 0 and h % hb == 0
    return pl.pallas_call(
        functools.partial(_mla_attn_kernel, heads=hb),
        grid=(b, h // hb, nq),
        in_specs=[
            pl.BlockSpec((1, hb, 1, DKP, TQ), lambda bi, hi, i: (bi, hi, i, 0, 0)),
            pl.BlockSpec((1, hb, s, DKP), lambda bi, hi, i: (bi, hi, 0, 0)),
            pl.BlockSpec((1, hb, nq, DV, TQ), lambda bi, hi, i: (bi, hi, 0, 0, 0)),
            pl.BlockSpec((hb, 1, DV, TQ), lambda bi, hi, i: (hi, bi * nq + i, 0, 0)),
        ],
        out_specs=pl.BlockSpec((hb, TQ, DV), lambda bi, hi, i: (hi, bi * nq + i, 0)),
        out_shape=jax.ShapeDtypeStruct((h, b * s, DV), BF16),
        scratch_shapes=[pltpu.VMEM((hb, 1, TQ), F32), pltpu.VMEM((hb, DV + SUM_ROWS, TQ), F32)],
        compiler_params=_cparams(("parallel", "parallel", "arbitrary")),
        name="mla_attention",
    )(qt, k, vt, zg)


def _pad_heads(w, n_heads, d):
    k = w.shape[0]
    return jnp.pad(w.reshape(k, n_heads, d), ((0, 0), (0, 0), (0, DKP - d))).reshape(k, n_heads * DKP).astype(BF16)


def _head_rows_out(b, s, h, tm, hb, width, dtype):
    nt = s // tm
    spec = pl.BlockSpec((1, hb, tm, width), lambda i, j: (i // nt, j, i % nt, 0))
    return jax.ShapeDtypeStruct((b, h, s, width), dtype), spec


def _head_tiles_out(b, s, h, tm, hb, width, dtype):
    nt = s // tm
    spec = pl.BlockSpec((1, hb, tm // TQ, width, TQ), lambda i, j: (i // nt, j, i % nt, 0, 0))
    return jax.ShapeDtypeStruct((b, h, s // TQ, width, TQ), dtype), spec


def nsa_layer(x2, b, s, norm_w, rel_bias, w_in, q_norm, k_norm, pe_k, w1_k, w2_k, pe_v, w1_v, w2_v, w_out):
    t, d_model = x2.shape
    h, g, dk, dv = NSA_HEADS, NSA_KV_GROUPS, NSA_QK_DIM, NSA_V_DIM
    qw, kvw, zw = h * dk, g * (dk + dv), h * dv
    tm = _pick(s, 1024, TQ)
    hn = rmsnorm_rows(x2, norm_w, d_model, BF16, "nsa_prenorm")

    w_q = w_in[:, :qw].astype(BF16)
    hb = _pick(h, 4)
    q_shape, q_spec = _head_tiles_out(b, s, h, tm, hb, DKP, BF16)
    qt = matmul(hn, w_q, functools.partial(_ep_q_heads, hb=hb, n_true=dk, scale=LOG2E * dk ** -0.5, rope=False),
                tm=tm, tn=hb * dk, out_shapes=q_shape, out_specs=q_spec,
                extra=(jnp.broadcast_to(q_norm[:, None], (dk, TQ)),),
                extra_specs=(pl.BlockSpec((dk, TQ), lambda i, j: (0, 0)),), slabs=True, name="nsa_q_proj")

    def kv_proj(br, normed, block_onehot=False):
        w_br = w_in[:, qw + br * kvw: qw + (br + 1) * kvw]
        w_kv = jnp.concatenate([_pad_heads(w_br[:, :g * dk], g, dk), w_br[:, g * dk:].astype(BF16)], axis=1)
        k_out = _head_rows_out(b, s, g, tm, g, DKP, BF16 if normed else F32)
        v_out = _head_tiles_out(b, s, g, tm, g, dv, BF16) if normed else _head_rows_out(b, s, g, tm, g, dv, F32)
        gain = jnp.pad(k_norm[br], (0, DKP - dk)).reshape(1, DKP)
        ep = functools.partial(_ep_nsa_kv, groups=g, n_true=dk, normed=normed,
                               seq_tiles=s // tm if block_onehot else 0)
        return matmul(hn, w_kv, ep,
                      tm=tm, tn=g * (DKP + dv), out_shapes=(k_out[0], v_out[0]), out_specs=(k_out[1], v_out[1]),
                      extra=(gain,), extra_specs=(pl.BlockSpec((1, DKP), lambda i, j: (0, 0)),),
                      slabs=True, name="nsa_kv_proj_%d" % br)

    kc_raw, vc_raw = kv_proj(0, False)
    ks, vst = kv_proj(1, True, block_onehot=True)
    kw, vwt = kv_proj(2, True)
    kc = compress(kc_raw, pe_k, w1_k, w2_k, k_norm[0], dk, "nsa_compress_k")
    vc = compress(vc_raw, pe_v, w1_v, w2_v, jnp.ones((dv,), F32), 0, "nsa_compress_v")
    vct = jnp.swapaxes(vc, -1, -2)

    z0 = qw + 3 * kvw
    w_g = w_in[:, z0 + 3 * zw:].astype(BF16)
    sg = matmul(hn, w_g, _ep_sigmoid, tm=tm, tn=3 * h,
                out_shapes=jax.ShapeDtypeStruct((t, 3 * h), F32),
                out_specs=pl.BlockSpec((tm, 3 * h), lambda i, j: (i, j)), name="nsa_gate_proj")
    hbz = _pick(h, 8)
    n_zt = 3 * h // hbz
    sg_tiles = sg.reshape(t // TQ, TQ, n_zt, hbz).transpose(2, 0, 3, 1)
    w_z = w_in[:, z0:z0 + 3 * zw].astype(BF16)
    gates = matmul(hn, w_z, functools.partial(_ep_gate_heads, hb=hbz, gated=True), tm=tm, tn=hbz * dv,
                   out_shapes=jax.ShapeDtypeStruct((3 * h, t // TQ, dv, TQ), BF16),
                   out_specs=pl.BlockSpec((hbz, tm // TQ, dv, TQ), lambda i, j: (j, i, 0, 0)),
                   extra=(sg_tiles,),
                   extra_specs=(pl.BlockSpec((1, tm // TQ, hbz, TQ), lambda i, j: (j, i, 0, 0)),),
                   slabs=True, name="nsa_z_proj")

    o = nsa_attention(qt, kc, vct, ks, vst, kw, vwt, gates, rel_bias)
    tmo = _pick(t, 512, 8)
    tno = _pick(d_model, 512, V7X_LANES)
    return matmul(o, w_out.astype(BF16), _ep_residual, tm=tmo, tn=tno,
                  out_shapes=jax.ShapeDtypeStruct((t, d_model), F32),
                  out_specs=pl.BlockSpec((tmo, tno), lambda i, j: (i, j)),
                  extra=(x2,), extra_specs=(pl.BlockSpec((tmo, tno), lambda i, j: (i, j)),), name="nsa_out_proj")


def mla_layer(x2, b, s, positions, norm_w, w_in, q_a_norm, w_q_b, kv_a_norm, w_kv_b, q_norm, k_norm, w_out):
    t, d_model = x2.shape
    h = MLA_HEADS
    dqk = MLA_NOPE_DIM + MLA_ROPE_DIM
    tm = _pick(s, 1024, TQ)
    hn = rmsnorm_rows(x2, norm_w, d_model, BF16, "mla_prenorm")

    half = MLA_ROPE_DIM // 2
    inv_freq = ROPE_BASE ** (-jnp.arange(half, dtype=F32) / half)
    ang = positions.astype(F32).reshape(t, 1) * inv_freq[None, :]
    zeros = jnp.zeros((t, V7X_LANES - MLA_ROPE_DIM), F32)
    cos2 = jnp.concatenate([jnp.cos(ang), jnp.cos(ang), zeros], axis=1)
    sin2 = jnp.concatenate([-jnp.sin(ang), jnp.sin(ang), zeros], axis=1)
    rope_spec = pl.BlockSpec((tm, V7X_LANES), lambda i, j: (i, 0))
    cos2_t = jnp.swapaxes(cos2[:, :MLA_ROPE_DIM].reshape(t // TQ, TQ, MLA_ROPE_DIM), 1, 2)
    sin2_t = jnp.swapaxes(sin2[:, :MLA_ROPE_DIM].reshape(t // TQ, TQ, MLA_ROPE_DIM), 1, 2)
    rope_t_spec = pl.BlockSpec((tm // TQ, MLA_ROPE_DIM, TQ), lambda i, j: (i, 0, 0))
    row_spec = lambda w: pl.BlockSpec((1, w), lambda i, j: (0, 0))

    c0, c1, c2 = MLA_Q_LORA, MLA_Q_LORA + MLA_KV_LORA, MLA_Q_LORA + MLA_KV_LORA + MLA_ROPE_DIM
    w_lat = jnp.concatenate([w_in[:, :c1], jnp.pad(w_in[:, c1:c2], ((0, 0), (0, DKP - MLA_ROPE_DIM)))],
                            axis=1).astype(BF16)
    lat_w = c1 + DKP
    tml = _pick(s, 512, 8)
    row_out = lambda w, dt: (jax.ShapeDtypeStruct((t, w), dt), pl.BlockSpec((tml, w), lambda i, j: (i, 0)))
    lat_outs = (row_out(c0, BF16), row_out(c1 - c0, BF16), row_out(V7X_LANES, F32))
    cq, ckv, kpe = matmul(hn, w_lat, functools.partial(_ep_mla_latent, c0=c0, c1=c1), tm=tml, tn=lat_w,
                          out_shapes=tuple(o[0] for o in lat_outs), out_specs=tuple(o[1] for o in lat_outs),
                          extra=(q_a_norm.reshape(1, c0), kv_a_norm.reshape(1, c1 - c0)),
                          extra_specs=(row_spec(c0), row_spec(c1 - c0)), w_once=True, name="mla_latent_proj")

    hbz = _pick(h, 8)
    zg = matmul(hn, w_in[:, c2:].astype(BF16), functools.partial(_ep_gate_heads, hb=hbz, gated=False),
                tm=tm, tn=hbz * DV, out_shapes=jax.ShapeDtypeStruct((h, t // TQ, DV, TQ), BF16),
                out_specs=pl.BlockSpec((hbz, tm // TQ, DV, TQ), lambda i, j: (j, i, 0, 0)),
                slabs=True, name="mla_z_proj")

    hb = _pick(h, 4)
    q_shape, q_spec = _head_tiles_out(b, s, h, tm, hb, DKP, BF16)
    qt = matmul(cq, w_q_b.astype(BF16),
                functools.partial(_ep_q_heads, hb=hb, n_true=dqk, scale=LOG2E * dqk ** -0.5, rope=True),
                tm=tm, tn=hb * dqk, out_shapes=q_shape, out_specs=q_spec,
                extra=(jnp.broadcast_to(q_norm[:, None], (dqk, TQ)), cos2_t, sin2_t),
                extra_specs=(pl.BlockSpec((dqk, TQ), lambda i, j: (0, 0)), rope_t_spec, rope_t_spec),
                slabs=True, name="mla_q_proj")

    k_shape, k_spec = _head_rows_out(b, s, h, tm, hb, DKP, BF16)
    v_shape, v_spec = _head_tiles_out(b, s, h, tm, hb, DV, BF16)
    gain_n = k_norm[:MLA_NOPE_DIM].reshape(1, MLA_NOPE_DIM)
    gain_r = jnp.pad(k_norm[MLA_NOPE_DIM:], (0, V7X_LANES - MLA_ROPE_DIM)).reshape(1, V7X_LANES)
    k, vt = matmul(ckv, w_kv_b.astype(BF16), functools.partial(_ep_mla_kv, hb=hb),
                   tm=tm, tn=hb * DKP, out_shapes=(k_shape, v_shape), out_specs=(k_spec, v_spec),
                   extra=(kpe, gain_n, gain_r, cos2, sin2),
                   extra_specs=(rope_spec, row_spec(MLA_NOPE_DIM), row_spec(V7X_LANES), rope_spec, rope_spec),
                   slabs=True, name="mla_kv_proj")

    o = mla_attention(qt, k, vt, zg)
    tmo = _pick(t, 512, 8)
    tno = _pick(d_model, 512, V7X_LANES)
    return matmul(o, w_out.astype(BF16), _ep_residual, tm=tmo, tn=tno,
                  out_shapes=jax.ShapeDtypeStruct((t, d_model), F32),
                  out_specs=pl.BlockSpec((tmo, tno), lambda i, j: (i, j)),
                  extra=(x2,), extra_specs=(pl.BlockSpec((tmo, tno), lambda i, j: (i, j)),), name="mla_out_proj")


def kernel(x, positions, norm_w, rel_bias, nsa_w_in, nsa_q_norm, nsa_k_norm, nsa_cmp_pe_k, nsa_cmp_w1_k,
           nsa_cmp_w2_k, nsa_cmp_pe_v, nsa_cmp_w1_v, nsa_cmp_w2_v, nsa_w_out, mla_w_in, mla_q_a_norm,
           mla_w_q_b, mla_kv_a_norm, mla_w_kv_b, mla_q_norm, mla_k_norm, mla_w_out):
    b, s, d_model = x.shape
    x2 = x.reshape(b * s, d_model)
    depth = norm_w.shape[0]
    for layer in range(depth):
        j = layer // 2
        if layer % 2 == 0:
            x2 = nsa_layer(x2, b, s, norm_w[layer], rel_bias, nsa_w_in[j], nsa_q_norm[j], nsa_k_norm[j],
                           nsa_cmp_pe_k[j], nsa_cmp_w1_k[j], nsa_cmp_w2_k[j], nsa_cmp_pe_v[j], nsa_cmp_w1_v[j],
                           nsa_cmp_w2_v[j], nsa_w_out[j])
        else:
            x2 = mla_layer(x2, b, s, positions, norm_w[layer], mla_w_in[j], mla_q_a_norm[j], mla_w_q_b[j],
                           mla_kv_a_norm[j], mla_w_kv_b[j], mla_q_norm[j], mla_k_norm[j], mla_w_out[j])
    return x2.reshape(b, s, d_model)
```

```python
import functools
import math

import numpy as np
import jax
import jax.numpy as jnp
from jax import lax
from jax.experimental import pallas as pl
from jax.experimental.pallas import tpu as pltpu

F32 = jnp.float32
BF16 = jnp.bfloat16

EPS = 1e-6
REL_BUCKETS = 32
REL_MAX_DISTANCE = 128
NSA_HEADS = 64
NSA_KV_GROUPS = 4
NSA_QK_DIM = 192
NSA_V_DIM = 128
CMP_BLOCK = 32
CMP_STRIDE = 16
SEL_BLOCK = 64
SEL_TOPK = 16
WINDOW = 512
MLA_HEADS = 64
MLA_Q_LORA = 1536
MLA_KV_LORA = 512
MLA_NOPE_DIM = 128
MLA_ROPE_DIM = 64
MLA_V_DIM = 128
ROPE_BASE = 10000.0

V7X_LANES = 128
V7X_VMEM_BYTES = 64 * 1024 * 1024
VMEM_LIMIT_BYTES = V7X_VMEM_BYTES - 8 * 1024 * 1024

DKP = 2 * V7X_LANES
DV = V7X_LANES
TQ = 256
NEG = -1e30
M_INIT = -1e20
SUM_ROWS = 16
LOG2E = math.log2(math.e)
MLA_HEADS_PER_STEP = 4
NSA_HEADS_PER_ITER = 4
NSA_CMP_HEADS_PER_ITER = 16
FAR_TILES = 4


def _cparams(sem):
    return pltpu.CompilerParams(dimension_semantics=sem, vmem_limit_bytes=VMEM_LIMIT_BYTES)


def _dot(a, b):
    return jnp.dot(a, b, preferred_element_type=F32)


def _sigmoid(x):
    return 1.0 / (1.0 + jnp.exp(-x))


def _silu(x):
    return x * _sigmoid(x)


def _pick(n, target, unit=1):
    best = unit
    c = unit
    while c <= min(n, max(target, unit)):
        if n % c == 0:
            best = c
        c += unit
    return best


def _rmsnorm_kernel(x_ref, g_ref, o_ref, *, inv_n):
    x = x_ref[...].astype(F32)
    ms = jnp.sum(x * x, axis=-1, keepdims=True) * inv_n
    o_ref[...] = (x * lax.rsqrt(ms + EPS) * g_ref[...]).astype(o_ref.dtype)


def rmsnorm_rows(x, gain, n_true, out_dtype, name):
    m, d = x.shape
    tm = _pick(m, max(8, (2 * 1024 * 1024) // (4 * d)), 8)
    return pl.pallas_call(
        functools.partial(_rmsnorm_kernel, inv_n=1.0 / n_true),
        grid=(m // tm,),
        in_specs=[pl.BlockSpec((tm, d), lambda i: (i, 0)), pl.BlockSpec((1, d), lambda i: (0, 0))],
        out_specs=pl.BlockSpec((tm, d), lambda i: (i, 0)),
        out_shape=jax.ShapeDtypeStruct((m, d), out_dtype),
        compiler_params=_cparams(("parallel",)),
        name=name,
    )(x, gain.reshape(1, d).astype(F32))


def _matmul_kernel(*refs, epilogue, n_in, slabs):
    a_ref, w_ref = refs[0], refs[1]
    extra, outs = refs[2:n_in], refs[n_in:]
    if len(a_ref.shape) == 3:
        a = jnp.concatenate([a_ref[h] for h in range(a_ref.shape[0])], axis=1)
    else:
        a = a_ref[...]
    w = w_ref[...]
    if not slabs:
        epilogue(_dot(a, w), extra, outs)
        return
    n = a.shape[0] // TQ
    acc = _dot(a[:TQ], w)
    for u in range(n):
        nxt = _dot(a[(u + 1) * TQ:(u + 2) * TQ], w) if u + 1 < n else None
        epilogue(acc, u, extra, outs)
        acc = nxt


def matmul(a, w, epilogue, *, tm, tn, out_shapes, out_specs, extra=(), extra_specs=(), slabs=False,
           w_once=False, name):
    n = w.shape[1]
    if a.ndim == 3:
        m, k = a.shape[1], a.shape[0] * a.shape[2]
        a_spec = pl.BlockSpec((a.shape[0], tm, a.shape[2]), lambda i, j: (0, i, 0))
    else:
        m, k = a.shape
        a_spec = pl.BlockSpec((tm, k), lambda i, j: (i, 0))
    assert m % tm == 0 and n % tn == 0 and k == w.shape[0], (m, tm, n, tn, k)
    assert not slabs or tm % TQ == 0
    assert not w_once or tn == n
    w_spec = pl.BlockSpec((k, tn), lambda i, j: (0, j), pipeline_mode=pl.Buffered(1) if w_once else None)
    in_specs = [a_spec, w_spec]
    in_specs += list(extra_specs)
    return pl.pallas_call(
        functools.partial(_matmul_kernel, epilogue=epilogue, n_in=2 + len(extra), slabs=slabs),
        grid=(m // tm, n // tn),
        in_specs=in_specs,
        out_specs=out_specs,
        out_shape=out_shapes,
        compiler_params=_cparams(("parallel", "arbitrary")),
        name=name,
    )(a, w, *extra)


def _ep_store(acc, extra, outs):
    outs[0][...] = acc.astype(outs[0].dtype)


def _ep_sigmoid(acc, extra, outs):
    outs[0][...] = _sigmoid(acc)


def _ep_residual(acc, extra, outs):
    outs[0][...] = extra[0][...] + acc


def _ep_mla_latent(acc, extra, outs, *, c0, c1):
    def normed(x, gain):
        ms = jnp.mean(x * x, axis=-1, keepdims=True)
        return x * lax.rsqrt(ms + EPS) * gain

    outs[0][...] = normed(acc[:, :c0], extra[0][...]).astype(outs[0].dtype)
    outs[1][...] = normed(acc[:, c0:c1], extra[1][...]).astype(outs[1].dtype)
    outs[2][...] = acc[:, c1:c1 + V7X_LANES]


def _ep_nsa_kv(acc, u, extra, outs, *, groups, n_true, normed, seq_tiles=0):
    rows = pl.ds(u * TQ, TQ)
    gain = extra[0][...]
    if seq_tiles:
        tm = outs[0].shape[2]
        pos = (pl.program_id(0) % seq_tiles) * tm + u * TQ + lax.broadcasted_iota(jnp.int32, (TQ, 1), 0)
        slot = n_true + jnp.right_shift(pos, int(math.log2(SEL_BLOCK)))
        block_lane = lax.broadcasted_iota(jnp.int32, (TQ, DKP), 1) == slot
    for gi in range(groups):
        k = acc[:, gi * DKP:(gi + 1) * DKP]
        v = acc[:, groups * DKP + gi * DV:groups * DKP + (gi + 1) * DV]
        if normed:
            ms = jnp.sum(k * k, axis=-1, keepdims=True) * (1.0 / n_true)
            kn = k * lax.rsqrt(ms + EPS) * gain
            if seq_tiles:
                kn = jnp.where(block_lane, 1.0, kn)
            outs[0][0, gi, rows, :] = kn.astype(outs[0].dtype)
            outs[1][0, gi, u] = v.T.astype(outs[1].dtype)
        else:
            outs[0][0, gi, rows, :] = k
            outs[1][0, gi, rows, :] = v


def _rope_lanes(x, cos2, sin2):
    half = MLA_ROPE_DIM // 2
    lane = lax.broadcasted_iota(jnp.int32, x.shape, 1)
    up = pltpu.roll(x, V7X_LANES - half, axis=1)
    down = pltpu.roll(x, half, axis=1)
    rot = jnp.where(lane < half, up, down)
    return x * cos2 + rot * sin2


def _rope_rows(x, cos2, sin2):
    half = MLA_ROPE_DIM // 2
    rot = jnp.concatenate([x[half:], x[:half]], axis=0)
    return x * cos2 + rot * sin2


def _ep_q_heads(acc, u, extra, outs, *, hb, n_true, scale, rope):
    acc_t = acc.T
    gain = extra[0][...]
    zeros = jnp.zeros((DKP - n_true, TQ), F32)
    for r in range(hb):
        y = acc_t[r * n_true:(r + 1) * n_true]
        ms = jnp.sum(y * y, axis=0, keepdims=True) * (1.0 / n_true)
        yn = y * lax.rsqrt(ms + EPS) * gain
        if rope:
            nope = n_true - MLA_ROPE_DIM
            yn = jnp.concatenate([yn[:nope], _rope_rows(yn[nope:], extra[1][u], extra[2][u])], axis=0)
        outs[0][0, r, u] = (jnp.concatenate([yn, zeros], axis=0) * scale).astype(outs[0].dtype)


def _ep_gate_heads(acc, u, extra, outs, *, hb, gated):
    acc_t = acc.T
    for r in range(hb):
        a = _silu(acc_t[r * DV:(r + 1) * DV])
        if gated:
            a = a * extra[0][0, u, r:r + 1, :]
        outs[0][r, u] = a.astype(outs[0].dtype)


def _ep_mla_kv(acc, u, extra, outs, *, hb):
    rows = pl.ds(u * TQ, TQ)
    kpe = extra[0][rows, :]
    gain_n = extra[1][...]
    gain_r = extra[2][...]
    ss_pe = jnp.sum(kpe * kpe, axis=-1, keepdims=True)
    kr = _rope_lanes(kpe * gain_r, extra[3][rows, :], extra[4][rows, :])
    inv_n = 1.0 / (MLA_NOPE_DIM + MLA_ROPE_DIM)
    for r in range(hb):
        kn = acc[:, r * DKP:r * DKP + V7X_LANES]
        v = acc[:, r * DKP + V7X_LANES:(r + 1) * DKP]
        rs = lax.rsqrt((jnp.sum(kn * kn, axis=-1, keepdims=True) + ss_pe) * inv_n + EPS)
        outs[0][0, r, rows, :] = jnp.concatenate([kn * rs * gain_n, kr * rs], axis=1).astype(outs[0].dtype)
        outs[1][0, r, u] = v.T.astype(outs[1].dtype)


def _compress_kernel(xa_ref, xb_ref, pea_ref, peb_ref, w1a_ref, w1b_ref, w2_ref, g_ref, o_ref, *, norm_n):
    xa = (xa_ref[0, 0] + pea_ref[...]).astype(BF16)
    xb = (xb_ref[0, 0] + peb_ref[...]).astype(BF16)
    pre = _dot(xa, w1a_ref[...]) + _dot(xb, w1b_ref[...])
    y = _dot(_silu(pre).astype(BF16), w2_ref[...])
    if norm_n:
        ms = jnp.sum(y * y, axis=-1, keepdims=True) * (1.0 / norm_n)
        y = y * lax.rsqrt(ms + EPS) * g_ref[...]
    o_ref[0, 0] = y.astype(o_ref.dtype)


def compress(x_raw, pe, w1, w2, gain, norm_n, name):
    b, g, s, dp = x_raw.shape
    d = pe.shape[1]
    nch = s // CMP_STRIDE
    half = CMP_STRIDE * dp
    chunks = x_raw.reshape(b, g, nch, half)
    xb = jnp.concatenate([chunks[:, :, 1:], jnp.zeros((b, g, 1, half), F32)], axis=2)
    pe_p = jnp.pad(pe, ((0, 0), (0, dp - d)))
    pea = pe_p[:CMP_STRIDE].reshape(1, half)
    peb = pe_p[CMP_STRIDE:].reshape(1, half)
    w1p = jnp.pad(w1.reshape(CMP_BLOCK, d, d), ((0, 0), (0, dp - d), (0, dp - d))).astype(BF16)
    w1a = w1p[:CMP_STRIDE].reshape(half, dp)
    w1b = w1p[CMP_STRIDE:].reshape(half, dp)
    w2p = jnp.pad(w2, ((0, dp - d), (0, dp - d))).astype(BF16)
    gp = jnp.pad(gain, (0, dp - d)).reshape(1, dp).astype(F32)
    full = lambda shape: pl.BlockSpec(shape, lambda bi, gi: (0,) * len(shape))
    return pl.pallas_call(
        functools.partial(_compress_kernel, norm_n=norm_n),
        grid=(b, g),
        in_specs=[
            pl.BlockSpec((1, 1, nch, half), lambda bi, gi: (bi, gi, 0, 0)),
            pl.BlockSpec((1, 1, nch, half), lambda bi, gi: (bi, gi, 0, 0)),
            full((1, half)), full((1, half)), full((half, dp)), full((half, dp)), full((dp, dp)), full((1, dp)),
        ],
        out_specs=pl.BlockSpec((1, 1, nch, dp), lambda bi, gi: (bi, gi, 0, 0)),
        out_shape=jax.ShapeDtypeStruct((b, g, nch, dp), BF16),
        compiler_params=_cparams(("parallel", "arbitrary")),
        name=name,
    )(chunks, xb, pea, peb, w1a, w1b, w2p, gp)


def _attn_init(m_ref, acc_ref):
    m_ref[...] = jnp.full(m_ref.shape, M_INIT, F32)
    acc_ref[...] = jnp.zeros(acc_ref.shape, F32)


def _attn_update(st, vts, add, m_ref, acc_ref):
    ones = jnp.ones((SUM_ROWS, TQ), BF16)
    for j, vt in enumerate(vts):
        sj = st[j * TQ:(j + 1) * TQ]
        if add is not None and add[j] is not None:
            sj = sj + add[j]
        m_prev = m_ref[...]
        m_new = jnp.maximum(m_prev, jnp.max(sj, axis=0, keepdims=True))
        alpha = jnp.exp2(m_prev - m_new)
        p = jnp.exp2(sj - m_new).astype(BF16)
        acc_ref[...] = alpha * acc_ref[...] + _dot(jnp.concatenate([vt, ones], axis=0), p)
        m_ref[...] = m_new


def _attn_result(acc_ref):
    acc = acc_ref[...]
    return acc[:DV] / acc[DV:DV + 1]


def _tile_ds(idx, size):
    return pl.ds(pl.multiple_of(idx * size, size), size)


def _nsa_attn_kernel(qt_ref, kc_ref, vct_ref, ks_ref, vst_ref, kw_ref, vwt_ref, ac_ref, as_ref, aw_ref,
                     cb_ref, t0_ref, t1_ref, ovl_ref, o_ref,
                     m_scr, acc_scr, oc_scr, qa_scr, cmpb_scr, *, heads, n_cmp_pad, n_sel, k_sel):
    nh = NSA_HEADS_PER_ITER
    i = pl.program_id(2)
    tok = i * TQ + lax.broadcasted_iota(jnp.int32, (1, TQ), 1)
    per_tile = TQ // CMP_STRIDE
    band = cb_ref.shape[1]

    valid_c = (tok >= CMP_BLOCK - 1).astype(F32)
    nrow = lax.broadcasted_iota(jnp.int32, (per_tile + n_cmp_pad, TQ), 0)
    base = jnp.where(nrow >= (i + 2) * per_tile, NEG, 0.0)
    nc = cmpb_scr.shape[0]
    for u in range(nc):
        cmpb_scr[u] = base
    kc = kc_ref[0, 0]
    vct = vct_ref[0, 0]

    def cmp_group(gi, psum):
        rs = [gi * nc + u for u in range(nc)]
        for u, r in enumerate(rs):
            cmpb_scr[u, pl.ds(pl.multiple_of(i * per_tile, per_tile), band), :] = cb_ref[r]
        st_all = _dot(kc, jnp.concatenate([qt_ref[0, r, 0] for r in rs], axis=1))
        for u, r in enumerate(rs):
            st = st_all[:, u * TQ:(u + 1) * TQ] + cmpb_scr[u, per_tile:per_tile + n_cmp_pad, :]
            p = jnp.exp2(st - jnp.max(st, axis=0, keepdims=True))
            pn = p * (valid_c / jnp.sum(p, axis=0, keepdims=True))
            oc_scr[r] = _dot(vct, pn.astype(BF16)) * ac_ref[r, 0].astype(F32)
            psum = psum + pn
        return psum

    psum = lax.fori_loop(0, heads // nc, cmp_group, jnp.zeros((n_cmp_pad, TQ), F32))

    ovl = ovl_ref[...]
    p_hi = psum.astype(BF16)
    rem = psum - p_hi.astype(F32)
    p_mid = rem.astype(BF16)
    p_lo = (rem - p_mid.astype(F32)).astype(BF16)
    imp = _dot(ovl, p_hi) + _dot(ovl, p_mid) + _dot(ovl, p_lo)

    n_blk = DKP - NSA_QK_DIM
    blk = lax.broadcasted_iota(jnp.int32, (n_blk, TQ), 0)
    blk_f = blk.astype(F32)
    cur = jnp.right_shift(tok, int(math.log2(SEL_BLOCK)))
    forced = (blk == 0) | (blk == cur) | (blk == cur - 1)
    future = blk * SEL_BLOCK > tok
    val = jnp.where(forced, 1e30, jnp.where(future, -1.0, imp))
    val = jnp.where(blk < n_sel, val, -2.0)

    def pick(_, c):
        v, sel = c
        top = jnp.max(v, axis=0, keepdims=True)
        first = jnp.min(jnp.where(v == top, blk_f, float(n_blk)), axis=0, keepdims=True)
        hit = blk_f == first
        return jnp.where(hit, -3.0, v), jnp.where(hit, 1.0, sel)

    _, sel = lax.fori_loop(0, k_sel, pick, (val, jnp.zeros((n_blk, TQ), F32)), unroll=True)

    mask_rows = ((sel - 1.0) * (-NEG)).astype(BF16)
    for r in range(heads):
        qa_scr[r] = jnp.concatenate([qt_ref[0, r, 0, 0:NSA_QK_DIM, :], mask_rows], axis=0)

    ft = FAR_TILES
    n_far = jnp.maximum(i - 1, 0) // ft
    kj = lax.broadcasted_iota(jnp.int32, (TQ, TQ), 0)
    qi = lax.broadcasted_iota(jnp.int32, (TQ, TQ), 1)
    half_t = TQ // 2
    zb = jnp.zeros((half_t, half_t), F32)
    negb = jnp.full((half_t, half_t), NEG, F32)

    sel_state = (m_scr.at[0], acc_scr.at[0])
    win_state = (m_scr.at[1], acc_scr.at[1])
    _attn_init(*sel_state)
    _attn_init(*win_state)

    def scores(k_rows, rs, masked):
        qs = [qa_scr[r] if masked else qt_ref[0, r, 0] for r in rs]
        return _dot(k_rows, jnp.concatenate(qs, axis=1))

    def update_all(st_all, vts, adds, state, rs):
        for u, r in enumerate(rs):
            _attn_update(st_all[:, u * TQ:(u + 1) * TQ], vts, adds[u], *(s.at[r] for s in state))

    def far(c, _):
        k_rows = ks_ref[0, 0, _tile_ds(c, ft * TQ), :]
        vts = tuple(vst_ref[0, 0, ft * c + j] for j in range(ft))
        for g0 in range(0, heads, nh):
            rs = list(range(g0, g0 + nh))
            update_all(scores(k_rows, rs, True), vts, [None] * nh, sel_state, rs)
        return 0

    lax.fori_loop(0, n_far, far, 0)

    def head_group(gi, _):
        rs = [gi * nh + u for u in range(nh)]

        def near(n_s, n_w):
            t0s = [t0_ref[r] for r in rs]
            t1s = [t1_ref[r] for r in rs]
            prev = [jnp.concatenate([jnp.concatenate([zb, zb], axis=1),
                                     jnp.concatenate([t1, zb], axis=1)], axis=0) for t1 in t1s]
            diag = [jnp.concatenate([jnp.concatenate([t0, t1], axis=1),
                                     jnp.concatenate([negb, t0], axis=1)], axis=0) for t0, t1 in zip(t0s, t1s)]
            w0 = jnp.where(kj > qi, 0.0, NEG)
            st_s = scores(ks_ref[0, 0, pl.ds(pl.multiple_of((i - n_s + 1) * TQ, TQ), n_s * TQ), :], rs, True)
            st_w = scores(kw_ref[0, 0, pl.ds(pl.multiple_of((i - n_w + 1) * TQ, TQ), n_w * TQ), :], rs, False)
            adds_s, adds_w = [], []
            for u in range(nh):
                parts_s = [None] * n_s
                parts_s[-1] = diag[u]
                parts_w = [diag[u]]
                if n_s >= 2:
                    parts_s[-2] = prev[u]
                if n_w >= 2:
                    parts_w.insert(0, prev[u])
                if n_w >= 3:
                    parts_w.insert(0, w0)
                adds_s.append(parts_s)
                adds_w.append(parts_w)
            update_all(st_s, tuple(vst_ref[0, 0, i - n_s + 1 + j] for j in range(n_s)), adds_s, sel_state, rs)
            update_all(st_w, tuple(vwt_ref[0, 0, i - n_w + 1 + j] for j in range(n_w)), adds_w, win_state, rs)

        pl.when(i == 0)(lambda: near(1, 1))
        pl.when(i == 1)(lambda: near(2, 2))
        for k in range(ft):
            pl.when((i >= 2) & ((i - 1) % ft == k))(functools.partial(near, 2 + k, 3))

        for u, r in enumerate(rs):
            ot = (oc_scr[r] + _attn_result(acc_scr.at[0, r]) * as_ref[r, 0].astype(F32)
                  + _attn_result(acc_scr.at[1, r]) * aw_ref[r, 0].astype(F32))
            o_ref[r] = ot.T.astype(o_ref.dtype)
        return 0

    lax.fori_loop(0, heads // nh, head_group, 0)


def _t5_bucket(dist):
    n = jnp.maximum(dist, 0)
    max_exact = REL_BUCKETS // 2
    scaled = (jnp.log(jnp.maximum(n, max_exact).astype(F32) / max_exact)
              / math.log(REL_MAX_DISTANCE / max_exact))
    large = jnp.minimum(max_exact + (scaled * (REL_BUCKETS - max_exact)).astype(jnp.int32), REL_BUCKETS - 1)
    return jnp.where(n < max_exact, n, large)


def _bias_table(rel_bias, dist, shift):
    onehot = (_t5_bucket(dist)[None] == jnp.arange(REL_BUCKETS).reshape((-1,) + (1,) * dist.ndim)).astype(F32)
    vals = jnp.tensordot(rel_bias.T, onehot, axes=1, precision=lax.Precision.HIGHEST)
    return jnp.where(dist >= 0, (vals - shift) * LOG2E, NEG).astype(F32)


def nsa_attention(qt, kc, vct, ks, vst, kw, vwt, gates, rel_bias):
    b, h, nq = qt.shape[:3]
    s = nq * TQ
    g = kc.shape[1]
    heads = h // g
    n_cmp_pad = kc.shape[2]
    n_sel = s // SEL_BLOCK
    k_sel = min(SEL_TOPK, n_sel)
    half_t = TQ // 2
    nh = NSA_HEADS_PER_ITER
    nc = min(NSA_CMP_HEADS_PER_ITER, heads)
    assert heads % nh == 0 and heads % nc == 0
    n_blk = DKP - NSA_QK_DIM
    assert s % (2 * TQ) == 0 and WINDOW == 2 * TQ and n_sel <= n_blk and TQ % SEL_BLOCK == 0
    assert half_t >= REL_MAX_DISTANCE
    assert n_cmp_pad == s // CMP_STRIDE and n_cmp_pad % V7X_LANES == 0

    far = rel_bias[REL_BUCKETS - 1][:, None, None]
    per_tile = TQ // CMP_STRIDE
    band = 2 * per_tile
    assert CMP_STRIDE * (per_tile + 1) - (CMP_BLOCK - 1) >= REL_MAX_DISTANCE and n_cmp_pad % per_tile == 0
    a_rel = jnp.arange(band) - per_tile
    cb = _bias_table(rel_bias, jnp.arange(TQ)[None, :] - CMP_STRIDE * a_rel[:, None] - (CMP_BLOCK - 1), far)
    ij = jnp.arange(half_t)
    t0 = _bias_table(rel_bias, ij[None, :] - ij[:, None], far)
    t1 = _bias_table(rel_bias, half_t + ij[None, :] - ij[:, None], far)

    n_cmp = (s - CMP_BLOCK) // CMP_STRIDE + 1
    cmp_start = CMP_STRIDE * np.arange(n_cmp_pad)
    sel_start = SEL_BLOCK * np.arange(n_blk)
    ovl = ((cmp_start[None, :] < (sel_start + SEL_BLOCK)[:, None])
           & ((cmp_start + CMP_BLOCK)[None, :] > sel_start[:, None])
           & (np.arange(n_cmp_pad) < n_cmp)[None, :] & (np.arange(n_blk) < n_sel)[:, None])
    ovl = jnp.asarray(ovl, BF16)

    once = pl.Buffered(1)
    rows_spec = lambda n, d: pl.BlockSpec((1, 1, n, d), lambda bi, gi, i: (bi, gi, 0, 0), pipeline_mode=once)
    vt_spec = pl.BlockSpec((1, 1, nq, DV, TQ), lambda bi, gi, i: (bi, gi, 0, 0, 0), pipeline_mode=once)
    gate_spec = lambda br: pl.BlockSpec((heads, 1, DV, TQ),
                                        lambda bi, gi, i, br=br: (br * g + gi, bi * nq + i, 0, 0))
    tbl_spec = lambda shape: pl.BlockSpec((heads,) + shape, lambda bi, gi, i: (gi, 0, 0), pipeline_mode=once)
    kern = functools.partial(_nsa_attn_kernel, heads=heads, n_cmp_pad=n_cmp_pad, n_sel=n_sel, k_sel=k_sel)
    return pl.pallas_call(
        kern,
        grid=(b, g, nq),
        in_specs=[
            pl.BlockSpec((1, heads, 1, DKP, TQ), lambda bi, gi, i: (bi, gi, i, 0, 0)),
            rows_spec(n_cmp_pad, DKP), rows_spec(DV, n_cmp_pad),
            rows_spec(s, DKP), vt_spec, rows_spec(s, DKP), vt_spec,
            gate_spec(0), gate_spec(1), gate_spec(2),
            tbl_spec((band, TQ)), tbl_spec((half_t, half_t)), tbl_spec((half_t, half_t)),
            pl.BlockSpec((n_blk, n_cmp_pad), lambda bi, gi, i: (0, 0)),
        ],
        out_specs=pl.BlockSpec((heads, TQ, DV), lambda bi, gi, i: (gi, bi * nq + i, 0)),
        out_shape=jax.ShapeDtypeStruct((h, b * s, DV), BF16),
        scratch_shapes=[
            pltpu.VMEM((2, heads, 1, TQ), F32), pltpu.VMEM((2, heads, DV + SUM_ROWS, TQ), F32),
            pltpu.VMEM((heads, DV, TQ), F32), pltpu.VMEM((heads, DKP, TQ), BF16),
            pltpu.VMEM((nc, TQ // CMP_STRIDE + n_cmp_pad, TQ), F32),
        ],
        compiler_params=_cparams(("parallel", "parallel", "arbitrary")),
        name="nsa_attention",
    )(qt, kc, vct, ks, vst, kw, vwt, gates, gates, gates, cb, t0, t1, ovl)


def _mla_attn_kernel(qt_ref, k_ref, vt_ref, z_ref, o_ref, m_scr, acc_scr, *, heads):
    i = pl.program_id(2)
    kj = lax.broadcasted_iota(jnp.int32, (TQ, TQ), 0)
    qi = lax.broadcasted_iota(jnp.int32, (TQ, TQ), 1)
    causal = jnp.where(kj <= qi, 0.0, NEG)

    ft = FAR_TILES

    def block(first, n_tiles, add):
        rows = pl.ds(pl.multiple_of(first * TQ, TQ), n_tiles * TQ)
        sts = [_dot(k_ref[0, r, rows, :], qt_ref[0, r, 0]) for r in range(heads)]
        for r in range(heads):
            vts = tuple(vt_ref[0, r, first + j] for j in range(n_tiles))
            _attn_update(sts[r], vts, add, m_scr.at[r], acc_scr.at[r])

    _attn_init(m_scr, acc_scr)

    def far(c, _):
        block(ft * c, ft, None)
        return 0

    lax.fori_loop(0, jnp.maximum(i - 1, 0) // ft, far, 0)

    def near(n_tiles):
        block(i - n_tiles + 1, n_tiles, [None] * (n_tiles - 1) + [causal])

    pl.when(i == 0)(functools.partial(near, 1))
    for k in range(ft):
        pl.when((i >= 1) & ((i - 1) % ft == k))(functools.partial(near, 2 + k))
    for r in range(heads):
        ot = _attn_result(acc_scr.at[r]) * z_ref[r, 0].astype(F32)
        o_ref[r] = ot.T.astype(o_ref.dtype)


def mla_attention(qt, k, vt, zg):
    b, h, nq = qt.shape[:3]
    s = nq * TQ
    hb = MLA_HEADS_PER_STEP
    assert s % (2 * TQ) == 0 and h % hb == 0
    return pl.pallas_call(
        functools.partial(_mla_attn_kernel, heads=hb),
        grid=(b, h // hb, nq),
        in_specs=[
            pl.BlockSpec((1, hb, 1, DKP, TQ), lambda bi, hi, i: (bi, hi, i, 0, 0)),
            pl.BlockSpec((1, hb, s, DKP), lambda bi, hi, i: (bi, hi, 0, 0)),
            pl.BlockSpec((1, hb, nq, DV, TQ), lambda bi, hi, i: (bi, hi, 0, 0, 0)),
            pl.BlockSpec((hb, 1, DV, TQ), lambda bi, hi, i: (hi, bi * nq + i, 0, 0)),
        ],
        out_specs=pl.BlockSpec((hb, TQ, DV), lambda bi, hi, i: (hi, bi * nq + i, 0)),
        out_shape=jax.ShapeDtypeStruct((h, b * s, DV), BF16),
        scratch_shapes=[pltpu.VMEM((hb, 1, TQ), F32), pltpu.VMEM((hb, DV + SUM_ROWS, TQ), F32)],
        compiler_params=_cparams(("parallel", "parallel", "arbitrary")),
        name="mla_attention",
    )(qt, k, vt, zg)


def _pad_heads(w, n_heads, d):
    k = w.shape[0]
    return jnp.pad(w.reshape(k, n_heads, d), ((0, 0), (0, 0), (0, DKP - d))).reshape(k, n_heads * DKP).astype(BF16)


def _head_rows_out(b, s, h, tm, hb, width, dtype):
    nt = s // tm
    spec = pl.BlockSpec((1, hb, tm, width), lambda i, j: (i // nt, j, i % nt, 0))
    return jax.ShapeDtypeStruct((b, h, s, width), dtype), spec


def _head_tiles_out(b, s, h, tm, hb, width, dtype):
    nt = s // tm
    spec = pl.BlockSpec((1, hb, tm // TQ, width, TQ), lambda i, j: (i // nt, j, i % nt, 0, 0))
    return jax.ShapeDtypeStruct((b, h, s // TQ, width, TQ), dtype), spec


def nsa_layer(x2, b, s, norm_w, rel_bias, w_in, q_norm, k_norm, pe_k, w1_k, w2_k, pe_v, w1_v, w2_v, w_out):
    t, d_model = x2.shape
    h, g, dk, dv = NSA_HEADS, NSA_KV_GROUPS, NSA_QK_DIM, NSA_V_DIM
    qw, kvw, zw = h * dk, g * (dk + dv), h * dv
    tm = _pick(s, 1024, TQ)
    hn = rmsnorm_rows(x2, norm_w, d_model, BF16, "nsa_prenorm")

    w_q = w_in[:, :qw].astype(BF16)
    hb = _pick(h, 4)
    q_shape, q_spec = _head_tiles_out(b, s, h, tm, hb, DKP, BF16)
    qt = matmul(hn, w_q, functools.partial(_ep_q_heads, hb=hb, n_true=dk, scale=LOG2E * dk ** -0.5, rope=False),
                tm=tm, tn=hb * dk, out_shapes=q_shape, out_specs=q_spec,
                extra=(jnp.broadcast_to(q_norm[:, None], (dk, TQ)),),
                extra_specs=(pl.BlockSpec((dk, TQ), lambda i, j: (0, 0)),), slabs=True, name="nsa_q_proj")

    def kv_proj(br, normed, block_onehot=False):
        w_br = w_in[:, qw + br * kvw: qw + (br + 1) * kvw]
        w_kv = jnp.concatenate([_pad_heads(w_br[:, :g * dk], g, dk), w_br[:, g * dk:].astype(BF16)], axis=1)
        k_out = _head_rows_out(b, s, g, tm, g, DKP, BF16 if normed else F32)
        v_out = _head_tiles_out(b, s, g, tm, g, dv, BF16) if normed else _head_rows_out(b, s, g, tm, g, dv, F32)
        gain = jnp.pad(k_norm[br], (0, DKP - dk)).reshape(1, DKP)
        ep = functools.partial(_ep_nsa_kv, groups=g, n_true=dk, normed=normed,
                               seq_tiles=s // tm if block_onehot else 0)
        return matmul(hn, w_kv, ep,
                      tm=tm, tn=g * (DKP + dv), out_shapes=(k_out[0], v_out[0]), out_specs=(k_out[1], v_out[1]),
                      extra=(gain,), extra_specs=(pl.BlockSpec((1, DKP), lambda i, j: (0, 0)),),
                      slabs=True, name="nsa_kv_proj_%d" % br)

    kc_raw, vc_raw = kv_proj(0, False)
    ks, vst = kv_proj(1, True, block_onehot=True)
    kw, vwt = kv_proj(2, True)
    kc = compress(kc_raw, pe_k, w1_k, w2_k, k_norm[0], dk, "nsa_compress_k")
    vc = compress(vc_raw, pe_v, w1_v, w2_v, jnp.ones((dv,), F32), 0, "nsa_compress_v")
    vct = jnp.swapaxes(vc, -1, -2)

    z0 = qw + 3 * kvw
    w_g = w_in[:, z0 + 3 * zw:].astype(BF16)
    sg = matmul(hn, w_g, _ep_sigmoid, tm=tm, tn=3 * h,
                out_shapes=jax.ShapeDtypeStruct((t, 3 * h), F32),
                out_specs=pl.BlockSpec((tm, 3 * h), lambda i, j: (i, j)), name="nsa_gate_proj")
    hbz = _pick(h, 8)
    n_zt = 3 * h // hbz
    sg_tiles = sg.reshape(t // TQ, TQ, n_zt, hbz).transpose(2, 0, 3, 1)
    w_z = w_in[:, z0:z0 + 3 * zw].astype(BF16)
    gates = matmul(hn, w_z, functools.partial(_ep_gate_heads, hb=hbz, gated=True), tm=tm, tn=hbz * dv,
                   out_shapes=jax.ShapeDtypeStruct((3 * h, t // TQ, dv, TQ), BF16),
                   out_specs=pl.BlockSpec((hbz, tm // TQ, dv, TQ), lambda i, j: (j, i, 0, 0)),
                   extra=(sg_tiles,),
                   extra_specs=(pl.BlockSpec((1, tm // TQ, hbz, TQ), lambda i, j: (j, i, 0, 0)),),
                   slabs=True, name="nsa_z_proj")

    o = nsa_attention(qt, kc, vct, ks, vst, kw, vwt, gates, rel_bias)
    tmo = _pick(t, 512, 8)
    tno = _pick(d_model, 512, V7X_LANES)
    return matmul(o, w_out.astype(BF16), _ep_residual, tm=tmo, tn=tno,
                  out_shapes=jax.ShapeDtypeStruct((t, d_model), F32),
                  out_specs=pl.BlockSpec((tmo, tno), lambda i, j: (i, j)),
                  extra=(x2,), extra_specs=(pl.BlockSpec((tmo, tno), lambda i, j: (i, j)),), name="nsa_out_proj")


def mla_layer(x2, b, s, positions, norm_w, w_in, q_a_norm, w_q_b, kv_a_norm, w_kv_b, q_norm, k_norm, w_out):
    t, d_model = x2.shape
    h = MLA_HEADS
    dqk = MLA_NOPE_DIM + MLA_ROPE_DIM
    tm = _pick(s, 1024, TQ)
    hn = rmsnorm_rows(x2, norm_w, d_model, BF16, "mla_prenorm")

    half = MLA_ROPE_DIM // 2
    inv_freq = ROPE_BASE ** (-jnp.arange(half, dtype=F32) / half)
    ang = positions.astype(F32).reshape(t, 1) * inv_freq[None, :]
    zeros = jnp.zeros((t, V7X_LANES - MLA_ROPE_DIM), F32)
    cos2 = jnp.concatenate([jnp.cos(ang), jnp.cos(ang), zeros], axis=1)
    sin2 = jnp.concatenate([-jnp.sin(ang), jnp.sin(ang), zeros], axis=1)
    rope_spec = pl.BlockSpec((tm, V7X_LANES), lambda i, j: (i, 0))
    cos2_t = jnp.swapaxes(cos2[:, :MLA_ROPE_DIM].reshape(t // TQ, TQ, MLA_ROPE_DIM), 1, 2)
    sin2_t = jnp.swapaxes(sin2[:, :MLA_ROPE_DIM].reshape(t // TQ, TQ, MLA_ROPE_DIM), 1, 2)
    rope_t_spec = pl.BlockSpec((tm // TQ, MLA_ROPE_DIM, TQ), lambda i, j: (i, 0, 0))
    row_spec = lambda w: pl.BlockSpec((1, w), lambda i, j: (0, 0))

    c0, c1, c2 = MLA_Q_LORA, MLA_Q_LORA + MLA_KV_LORA, MLA_Q_LORA + MLA_KV_LORA + MLA_ROPE_DIM
    w_lat = jnp.concatenate([w_in[:, :c1], jnp.pad(w_in[:, c1:c2], ((0, 0), (0, DKP - MLA_ROPE_DIM)))],
                            axis=1).astype(BF16)
    lat_w = c1 + DKP
    tml = _pick(s, 512, 8)
    row_out = lambda w, dt: (jax.ShapeDtypeStruct((t, w), dt), pl.BlockSpec((tml, w), lambda i, j: (i, 0)))
    lat_outs = (row_out(c0, BF16), row_out(c1 - c0, BF16), row_out(V7X_LANES, F32))
    cq, ckv, kpe = matmul(hn, w_lat, functools.partial(_ep_mla_latent, c0=c0, c1=c1), tm=tml, tn=lat_w,
                          out_shapes=tuple(o[0] for o in lat_outs), out_specs=tuple(o[1] for o in lat_outs),
                          extra=(q_a_norm.reshape(1, c0), kv_a_norm.reshape(1, c1 - c0)),
                          extra_specs=(row_spec(c0), row_spec(c1 - c0)), w_once=True, name="mla_latent_proj")

    hbz = _pick(h, 8)
    zg = matmul(hn, w_in[:, c2:].astype(BF16), functools.partial(_ep_gate_heads, hb=hbz, gated=False),
                tm=tm, tn=hbz * DV, out_shapes=jax.ShapeDtypeStruct((h, t // TQ, DV, TQ), BF16),
                out_specs=pl.BlockSpec((hbz, tm // TQ, DV, TQ), lambda i, j: (j, i, 0, 0)),
                slabs=True, name="mla_z_proj")

    hb = _pick(h, 4)
    q_shape, q_spec = _head_tiles_out(b, s, h, tm, hb, DKP, BF16)
    qt = matmul(cq, w_q_b.astype(BF16),
                functools.partial(_ep_q_heads, hb=hb, n_true=dqk, scale=LOG2E * dqk ** -0.5, rope=True),
                tm=tm, tn=hb * dqk, out_shapes=q_shape, out_specs=q_spec,
                extra=(jnp.broadcast_to(q_norm[:, None], (dqk, TQ)), cos2_t, sin2_t),
                extra_specs=(pl.BlockSpec((dqk, TQ), lambda i, j: (0, 0)), rope_t_spec, rope_t_spec),
                slabs=True, name="mla_q_proj")

    k_shape, k_spec = _head_rows_out(b, s, h, tm, hb, DKP, BF16)
    v_shape, v_spec = _head_tiles_out(b, s, h, tm, hb, DV, BF16)
    gain_n = k_norm[:MLA_NOPE_DIM].reshape(1, MLA_NOPE_DIM)
    gain_r = jnp.pad(k_norm[MLA_NOPE_DIM:], (0, V7X_LANES - MLA_ROPE_DIM)).reshape(1, V7X_LANES)
    k, vt = matmul(ckv, w_kv_b.astype(BF16), functools.partial(_ep_mla_kv, hb=hb),
                   tm=tm, tn=hb * DKP, out_shapes=(k_shape, v_shape), out_specs=(k_spec, v_spec),
                   extra=(kpe, gain_n, gain_r, cos2, sin2),
                   extra_specs=(rope_spec, row_spec(MLA_NOPE_DIM), row_spec(V7X_LANES), rope_spec, rope_spec),
                   slabs=True, name="mla_kv_proj")

    o = mla_attention(qt, k, vt, zg)
    tmo = _pick(t, 512, 8)
    tno = _pick(d_model, 512, V7X_LANES)
    return matmul(o, w_out.astype(BF16), _ep_residual, tm=tmo, tn=tno,
                  out_shapes=jax.ShapeDtypeStruct((t, d_model), F32),
                  out_specs=pl.BlockSpec((tmo, tno), lambda i, j: (i, j)),
                  extra=(x2,), extra_specs=(pl.BlockSpec((tmo, tno), lambda i, j: (i, j)),), name="mla_out_proj")


def kernel(x, positions, norm_w, rel_bias, nsa_w_in, nsa_q_norm, nsa_k_norm, nsa_cmp_pe_k, nsa_cmp_w1_k,
           nsa_cmp_w2_k, nsa_cmp_pe_v, nsa_cmp_w1_v, nsa_cmp_w2_v, nsa_w_out, mla_w_in, mla_q_a_norm,
           mla_w_q_b, mla_kv_a_norm, mla_w_kv_b, mla_q_norm, mla_k_norm, mla_w_out):
    b, s, d_model = x.shape
    x2 = x.reshape(b * s, d_model)
    depth = norm_w.shape[0]
    for layer in range(depth):
        j = layer // 2
        if layer % 2 == 0:
            x2 = nsa_layer(x2, b, s, norm_w[layer], rel_bias, nsa_w_in[j], nsa_q_norm[j], nsa_k_norm[j],
                           nsa_cmp_pe_k[j], nsa_cmp_w1_k[j], nsa_cmp_w2_k[j], nsa_cmp_pe_v[j], nsa_cmp_w1_v[j],
                           nsa_cmp_w2_v[j], nsa_w_out[j])
        else:
            x2 = mla_layer(x2, b, s, positions, norm_w[layer], mla_w_in[j], mla_q_a_norm[j], mla_w_q_b[j],
                           mla_kv_a_norm[j], mla_w_kv_b[j], mla_q_norm[j], mla_k_norm[j], mla_w_out[j])
    return x2.reshape(b, s, d_model)
```

```python
import functools
import math

import numpy as np
import jax
import jax.numpy as jnp
from jax import lax
from jax.experimental import pallas as pl
from jax.experimental.pallas import tpu as pltpu

F32 = jnp.float32
BF16 = jnp.bfloat16

EPS = 1e-6
REL_BUCKETS = 32
REL_MAX_DISTANCE = 128
NSA_HEADS = 64
NSA_KV_GROUPS = 4
NSA_QK_DIM = 192
NSA_V_DIM = 128
CMP_BLOCK = 32
CMP_STRIDE = 16
SEL_BLOCK = 64
SEL_TOPK = 16
WINDOW = 512
MLA_HEADS = 64
MLA_Q_LORA = 1536
MLA_KV_LORA = 512
MLA_NOPE_DIM = 128
MLA_ROPE_DIM = 64
MLA_V_DIM = 128
ROPE_BASE = 10000.0

V7X_LANES = 128
V7X_VMEM_BYTES = 64 * 1024 * 1024
VMEM_LIMIT_BYTES = V7X_VMEM_BYTES - 8 * 1024 * 1024

DKP = 2 * V7X_LANES
DV = V7X_LANES
TQ = 256
NEG = -1e30
M_INIT = -1e20
SUM_ROWS = 16
LOG2E = math.log2(math.e)
MLA_HEADS_PER_STEP = 4
NSA_HEADS_PER_ITER = 4
NSA_CMP_HEADS_PER_ITER = 16
FAR_TILES = 4


def _cparams(sem):
    return pltpu.CompilerParams(dimension_semantics=sem, vmem_limit_bytes=VMEM_LIMIT_BYTES)


def _dot(a, b):
    return jnp.dot(a, b, preferred_element_type=F32)


def _sigmoid(x):
    return 1.0 / (1.0 + jnp.exp(-x))


def _silu(x):
    return x * _sigmoid(x)


def _pick(n, target, unit=1):
    best = unit
    c = unit
    while c <= min(n, max(target, unit)):
        if n % c == 0:
            best = c
        c += unit
    return best


def _rmsnorm_kernel(x_ref, g_ref, o_ref, *, inv_n):
    x = x_ref[...].astype(F32)
    ms = jnp.sum(x * x, axis=-1, keepdims=True) * inv_n
    o_ref[...] = (x * lax.rsqrt(ms + EPS) * g_ref[...]).astype(o_ref.dtype)


def rmsnorm_rows(x, gain, n_true, out_dtype, name):
    m, d = x.shape
    tm = _pick(m, max(8, (2 * 1024 * 1024) // (4 * d)), 8)
    return pl.pallas_call(
        functools.partial(_rmsnorm_kernel, inv_n=1.0 / n_true),
        grid=(m // tm,),
        in_specs=[pl.BlockSpec((tm, d), lambda i: (i, 0)), pl.BlockSpec((1, d), lambda i: (0, 0))],
        out_specs=pl.BlockSpec((tm, d), lambda i: (i, 0)),
        out_shape=jax.ShapeDtypeStruct((m, d), out_dtype),
        compiler_params=_cparams(("parallel",)),
        name=name,
    )(x, gain.reshape(1, d).astype(F32))


def _matmul_kernel(*refs, epilogue, n_in, slabs):
    a_ref, w_ref = refs[0], refs[1]
    extra, outs = refs[2:n_in], refs[n_in:]
    if len(a_ref.shape) == 3:
        a = jnp.concatenate([a_ref[h] for h in range(a_ref.shape[0])], axis=1)
    else:
        a = a_ref[...]
    w = w_ref[...]
    if not slabs:
        epilogue(_dot(a, w), extra, outs)
        return
    n = a.shape[0] // TQ
    acc = _dot(a[:TQ], w)
    for u in range(n):
        nxt = _dot(a[(u + 1) * TQ:(u + 2) * TQ], w) if u + 1 < n else None
        epilogue(acc, u, extra, outs)
        acc = nxt


def matmul(a, w, epilogue, *, tm, tn, out_shapes, out_specs, extra=(), extra_specs=(), slabs=False,
           w_once=False, cols=None, name):
    col0, n = cols if cols else (0, w.shape[1])
    assert col0 % tn == 0
    jb = col0 // tn
    if a.ndim == 3:
        m, k = a.shape[1], a.shape[0] * a.shape[2]
        a_spec = pl.BlockSpec((a.shape[0], tm, a.shape[2]), lambda i, j: (0, i, 0))
    else:
        m, k = a.shape
        a_spec = pl.BlockSpec((tm, k), lambda i, j: (i, 0))
    assert m % tm == 0 and n % tn == 0 and k == w.shape[0], (m, tm, n, tn, k)
    assert not slabs or tm % TQ == 0
    assert not w_once or tn == n
    w_spec = pl.BlockSpec((k, tn), lambda i, j: (0, jb + j), pipeline_mode=pl.Buffered(1) if w_once else None)
    in_specs = [a_spec, w_spec]
    in_specs += list(extra_specs)
    return pl.pallas_call(
        functools.partial(_matmul_kernel, epilogue=epilogue, n_in=2 + len(extra), slabs=slabs),
        grid=(m // tm, n // tn),
        in_specs=in_specs,
        out_specs=out_specs,
        out_shape=out_shapes,
        compiler_params=_cparams(("parallel", "arbitrary")),
        name=name,
    )(a, w, *extra)


def _ep_store(acc, extra, outs):
    outs[0][...] = acc.astype(outs[0].dtype)


def _ep_sigmoid(acc, extra, outs):
    outs[0][...] = _sigmoid(acc)


def _ep_residual(acc, extra, outs):
    outs[0][...] = extra[0][...] + acc


def _ep_mla_latent(acc, extra, outs, *, c0, c1):
    def normed(x, gain):
        ms = jnp.mean(x * x, axis=-1, keepdims=True)
        return x * lax.rsqrt(ms + EPS) * gain

    outs[0][...] = normed(acc[:, :c0], extra[0][...]).astype(outs[0].dtype)
    outs[1][...] = normed(acc[:, c0:c1], extra[1][...]).astype(outs[1].dtype)
    outs[2][...] = acc[:, c1:c1 + V7X_LANES]


def _ep_nsa_kv(acc, u, extra, outs, *, groups, n_true, normed, seq_tiles=0):
    rows = pl.ds(u * TQ, TQ)
    gain = extra[0][...]
    if seq_tiles:
        tm = outs[0].shape[2]
        pos = (pl.program_id(0) % seq_tiles) * tm + u * TQ + lax.broadcasted_iota(jnp.int32, (TQ, 1), 0)
        slot = n_true + jnp.right_shift(pos, int(math.log2(SEL_BLOCK)))
        block_lane = lax.broadcasted_iota(jnp.int32, (TQ, DKP), 1) == slot
    for gi in range(groups):
        k = acc[:, gi * DKP:(gi + 1) * DKP]
        v = acc[:, groups * DKP + gi * DV:groups * DKP + (gi + 1) * DV]
        if normed:
            ms = jnp.sum(k * k, axis=-1, keepdims=True) * (1.0 / n_true)
            kn = k * lax.rsqrt(ms + EPS) * gain
            if seq_tiles:
                kn = jnp.where(block_lane, 1.0, kn)
            outs[0][0, gi, rows, :] = kn.astype(outs[0].dtype)
            outs[1][0, gi, u] = v.T.astype(outs[1].dtype)
        else:
            outs[0][0, gi, rows, :] = k
            outs[1][0, gi, rows, :] = v


def _rope_lanes(x, cos2, sin2):
    half = MLA_ROPE_DIM // 2
    lane = lax.broadcasted_iota(jnp.int32, x.shape, 1)
    up = pltpu.roll(x, V7X_LANES - half, axis=1)
    down = pltpu.roll(x, half, axis=1)
    rot = jnp.where(lane < half, up, down)
    return x * cos2 + rot * sin2


def _rope_rows(x, cos2, sin2):
    half = MLA_ROPE_DIM // 2
    rot = jnp.concatenate([x[half:], x[:half]], axis=0)
    return x * cos2 + rot * sin2


def _ep_q_heads(acc, u, extra, outs, *, hb, n_true, scale, rope):
    acc_t = acc.T
    gain = extra[0][...]
    zeros = jnp.zeros((DKP - n_true, TQ), F32)
    for r in range(hb):
        y = acc_t[r * n_true:(r + 1) * n_true]
        ms = jnp.sum(y * y, axis=0, keepdims=True) * (1.0 / n_true)
        yn = y * lax.rsqrt(ms + EPS) * gain
        if rope:
            nope = n_true - MLA_ROPE_DIM
            yn = jnp.concatenate([yn[:nope], _rope_rows(yn[nope:], extra[1][u], extra[2][u])], axis=0)
        outs[0][0, r, u] = (jnp.concatenate([yn, zeros], axis=0) * scale).astype(outs[0].dtype)


def _ep_gate_heads(acc, u, extra, outs, *, hb, gated):
    acc_t = acc.T
    for r in range(hb):
        a = _silu(acc_t[r * DV:(r + 1) * DV])
        if gated:
            a = a * extra[0][0, u, r:r + 1, :]
        outs[0][r, u] = a.astype(outs[0].dtype)


def _ep_mla_kv(acc, u, extra, outs, *, hb):
    rows = pl.ds(u * TQ, TQ)
    kpe = extra[0][rows, :]
    gain_n = extra[1][...]
    gain_r = extra[2][...]
    ss_pe = jnp.sum(kpe * kpe, axis=-1, keepdims=True)
    kr = _rope_lanes(kpe * gain_r, extra[3][rows, :], extra[4][rows, :])
    inv_n = 1.0 / (MLA_NOPE_DIM + MLA_ROPE_DIM)
    for r in range(hb):
        kn = acc[:, r * DKP:r * DKP + V7X_LANES]
        v = acc[:, r * DKP + V7X_LANES:(r + 1) * DKP]
        rs = lax.rsqrt((jnp.sum(kn * kn, axis=-1, keepdims=True) + ss_pe) * inv_n + EPS)
        outs[0][0, r, rows, :] = jnp.concatenate([kn * rs * gain_n, kr * rs], axis=1).astype(outs[0].dtype)
        outs[1][0, r, u] = v.T.astype(outs[1].dtype)


def _compress_kernel(xa_ref, xb_ref, pea_ref, peb_ref, w1a_ref, w1b_ref, w2_ref, g_ref, o_ref, *, norm_n):
    xa = (xa_ref[0, 0] + pea_ref[...]).astype(BF16)
    xb = (xb_ref[0, 0] + peb_ref[...]).astype(BF16)
    pre = _dot(xa, w1a_ref[...]) + _dot(xb, w1b_ref[...])
    y = _dot(_silu(pre).astype(BF16), w2_ref[...])
    if norm_n:
        ms = jnp.sum(y * y, axis=-1, keepdims=True) * (1.0 / norm_n)
        y = y * lax.rsqrt(ms + EPS) * g_ref[...]
    o_ref[0, 0] = y.astype(o_ref.dtype)


def compress(x_raw, pe, w1, w2, gain, norm_n, name):
    b, g, s, dp = x_raw.shape
    d = pe.shape[1]
    nch = s // CMP_STRIDE
    half = CMP_STRIDE * dp
    chunks = x_raw.reshape(b, g, nch, half)
    xb = jnp.concatenate([chunks[:, :, 1:], jnp.zeros((b, g, 1, half), F32)], axis=2)
    pe_p = jnp.pad(pe, ((0, 0), (0, dp - d)))
    pea = pe_p[:CMP_STRIDE].reshape(1, half)
    peb = pe_p[CMP_STRIDE:].reshape(1, half)
    w1p = jnp.pad(w1.reshape(CMP_BLOCK, d, d), ((0, 0), (0, dp - d), (0, dp - d))).astype(BF16)
    w1a = w1p[:CMP_STRIDE].reshape(half, dp)
    w1b = w1p[CMP_STRIDE:].reshape(half, dp)
    w2p = jnp.pad(w2, ((0, dp - d), (0, dp - d))).astype(BF16)
    gp = jnp.pad(gain, (0, dp - d)).reshape(1, dp).astype(F32)
    full = lambda shape: pl.BlockSpec(shape, lambda bi, gi: (0,) * len(shape))
    return pl.pallas_call(
        functools.partial(_compress_kernel, norm_n=norm_n),
        grid=(b, g),
        in_specs=[
            pl.BlockSpec((1, 1, nch, half), lambda bi, gi: (bi, gi, 0, 0)),
            pl.BlockSpec((1, 1, nch, half), lambda bi, gi: (bi, gi, 0, 0)),
            full((1, half)), full((1, half)), full((half, dp)), full((half, dp)), full((dp, dp)), full((1, dp)),
        ],
        out_specs=pl.BlockSpec((1, 1, nch, dp), lambda bi, gi: (bi, gi, 0, 0)),
        out_shape=jax.ShapeDtypeStruct((b, g, nch, dp), BF16),
        compiler_params=_cparams(("parallel", "arbitrary")),
        name=name,
    )(chunks, xb, pea, peb, w1a, w1b, w2p, gp)


def _attn_init(m_ref, acc_ref):
    m_ref[...] = jnp.full(m_ref.shape, M_INIT, F32)
    acc_ref[...] = jnp.zeros(acc_ref.shape, F32)


def _attn_update(st, vts, add, m_ref, acc_ref):
    ones = jnp.ones((SUM_ROWS, TQ), BF16)
    for j, vt in enumerate(vts):
        sj = st[j * TQ:(j + 1) * TQ]
        if add is not None and add[j] is not None:
            sj = sj + add[j]
        m_prev = m_ref[...]
        m_new = jnp.maximum(m_prev, jnp.max(sj, axis=0, keepdims=True))
        alpha = jnp.exp2(m_prev - m_new)
        p = jnp.exp2(sj - m_new).astype(BF16)
        acc_ref[...] = alpha * acc_ref[...] + _dot(jnp.concatenate([vt, ones], axis=0), p)
        m_ref[...] = m_new


def _attn_result(acc_ref):
    acc = acc_ref[...]
    return acc[:DV] / acc[DV:DV + 1]


def _tile_ds(idx, size):
    return pl.ds(pl.multiple_of(idx * size, size), size)


def _nsa_attn_kernel(qt_ref, kc_ref, vct_ref, ks_ref, vst_ref, kw_ref, vwt_ref, ac_ref, as_ref, aw_ref,
                     cb_ref, t0_ref, t1_ref, ovl_ref, o_ref,
                     m_scr, acc_scr, oc_scr, qa_scr, cmpb_scr, *, heads, n_cmp_pad, n_sel, k_sel):
    nh = NSA_HEADS_PER_ITER
    i = pl.program_id(2)
    tok = i * TQ + lax.broadcasted_iota(jnp.int32, (1, TQ), 1)
    per_tile = TQ // CMP_STRIDE
    band = cb_ref.shape[1]

    valid_c = (tok >= CMP_BLOCK - 1).astype(F32)
    nrow = lax.broadcasted_iota(jnp.int32, (per_tile + n_cmp_pad, TQ), 0)
    base = jnp.where(nrow >= (i + 2) * per_tile, NEG, 0.0)
    nc = cmpb_scr.shape[0]
    for u in range(nc):
        cmpb_scr[u] = base
    kc = kc_ref[0, 0]
    vct = vct_ref[0, 0]

    def cmp_group(gi, psum):
        rs = [gi * nc + u for u in range(nc)]
        for u, r in enumerate(rs):
            cmpb_scr[u, pl.ds(pl.multiple_of(i * per_tile, per_tile), band), :] = cb_ref[r]
        st_all = _dot(kc, jnp.concatenate([qt_ref[0, r, 0] for r in rs], axis=1))
        for u, r in enumerate(rs):
            st = st_all[:, u * TQ:(u + 1) * TQ] + cmpb_scr[u, per_tile:per_tile + n_cmp_pad, :]
            p = jnp.exp2(st - jnp.max(st, axis=0, keepdims=True))
            pn = p * (valid_c / jnp.sum(p, axis=0, keepdims=True))
            oc_scr[r] = _dot(vct, pn.astype(BF16)) * ac_ref[r, 0].astype(F32)
            psum = psum + pn
        return psum

    psum = lax.fori_loop(0, heads // nc, cmp_group, jnp.zeros((n_cmp_pad, TQ), F32))

    ovl = ovl_ref[...]
    p_hi = psum.astype(BF16)
    rem = psum - p_hi.astype(F32)
    p_mid = rem.astype(BF16)
    p_lo = (rem - p_mid.astype(F32)).astype(BF16)
    imp = _dot(ovl, p_hi) + _dot(ovl, p_mid) + _dot(ovl, p_lo)

    n_blk = DKP - NSA_QK_DIM
    blk = lax.broadcasted_iota(jnp.int32, (n_blk, TQ), 0)
    blk_f = blk.astype(F32)
    cur = jnp.right_shift(tok, int(math.log2(SEL_BLOCK)))
    forced = (blk == 0) | (blk == cur) | (blk == cur - 1)
    future = blk * SEL_BLOCK > tok
    val = jnp.where(forced, 1e30, jnp.where(future, -1.0, imp))
    val = jnp.where(blk < n_sel, val, -2.0)

    def pick(_, c):
        v, sel = c
        top = jnp.max(v, axis=0, keepdims=True)
        first = jnp.min(jnp.where(v == top, blk_f, float(n_blk)), axis=0, keepdims=True)
        hit = blk_f == first
        return jnp.where(hit, -3.0, v), jnp.where(hit, 1.0, sel)

    _, sel = lax.fori_loop(0, k_sel, pick, (val, jnp.zeros((n_blk, TQ), F32)), unroll=True)

    mask_rows = ((sel - 1.0) * (-NEG)).astype(BF16)
    for r in range(heads):
        qa_scr[r] = jnp.concatenate([qt_ref[0, r, 0, 0:NSA_QK_DIM, :], mask_rows], axis=0)

    ft = FAR_TILES
    n_far = jnp.maximum(i - 1, 0) // ft
    kj = lax.broadcasted_iota(jnp.int32, (TQ, TQ), 0)
    qi = lax.broadcasted_iota(jnp.int32, (TQ, TQ), 1)
    half_t = TQ // 2
    zb = jnp.zeros((half_t, half_t), F32)
    negb = jnp.full((half_t, half_t), NEG, F32)

    sel_state = (m_scr.at[0], acc_scr.at[0])
    win_state = (m_scr.at[1], acc_scr.at[1])
    _attn_init(*sel_state)
    _attn_init(*win_state)

    def scores(k_rows, rs, masked):
        qs = [qa_scr[r] if masked else qt_ref[0, r, 0] for r in rs]
        return _dot(k_rows, jnp.concatenate(qs, axis=1))

    def update_all(st_all, vts, adds, state, rs):
        for u, r in enumerate(rs):
            _attn_update(st_all[:, u * TQ:(u + 1) * TQ], vts, adds[u], *(s.at[r] for s in state))

    def far(c, _):
        k_rows = ks_ref[0, 0, _tile_ds(c, ft * TQ), :]
        vts = tuple(vst_ref[0, 0, ft * c + j] for j in range(ft))
        for g0 in range(0, heads, nh):
            rs = list(range(g0, g0 + nh))
            update_all(scores(k_rows, rs, True), vts, [None] * nh, sel_state, rs)
        return 0

    lax.fori_loop(0, n_far, far, 0)

    def head_group(gi, _):
        rs = [gi * nh + u for u in range(nh)]

        def near(n_s, n_w):
            t0s = [t0_ref[r] for r in rs]
            t1s = [t1_ref[r] for r in rs]
            prev = [jnp.concatenate([jnp.concatenate([zb, zb], axis=1),
                                     jnp.concatenate([t1, zb], axis=1)], axis=0) for t1 in t1s]
            diag = [jnp.concatenate([jnp.concatenate([t0, t1], axis=1),
                                     jnp.concatenate([negb, t0], axis=1)], axis=0) for t0, t1 in zip(t0s, t1s)]
            w0 = jnp.where(kj > qi, 0.0, NEG)
            st_s = scores(ks_ref[0, 0, pl.ds(pl.multiple_of((i - n_s + 1) * TQ, TQ), n_s * TQ), :], rs, True)
            st_w = scores(kw_ref[0, 0, pl.ds(pl.multiple_of((i - n_w + 1) * TQ, TQ), n_w * TQ), :], rs, False)
            adds_s, adds_w = [], []
            for u in range(nh):
                parts_s = [None] * n_s
                parts_s[-1] = diag[u]
                parts_w = [diag[u]]
                if n_s >= 2:
                    parts_s[-2] = prev[u]
                if n_w >= 2:
                    parts_w.insert(0, prev[u])
                if n_w >= 3:
                    parts_w.insert(0, w0)
                adds_s.append(parts_s)
                adds_w.append(parts_w)
            update_all(st_s, tuple(vst_ref[0, 0, i - n_s + 1 + j] for j in range(n_s)), adds_s, sel_state, rs)
            update_all(st_w, tuple(vwt_ref[0, 0, i - n_w + 1 + j] for j in range(n_w)), adds_w, win_state, rs)

        pl.when(i == 0)(lambda: near(1, 1))
        pl.when(i == 1)(lambda: near(2, 2))
        for k in range(ft):
            pl.when((i >= 2) & ((i - 1) % ft == k))(functools.partial(near, 2 + k, 3))

        for u, r in enumerate(rs):
            ot = (oc_scr[r] + _attn_result(acc_scr.at[0, r]) * as_ref[r, 0].astype(F32)
                  + _attn_result(acc_scr.at[1, r]) * aw_ref[r, 0].astype(F32))
            o_ref[r] = ot.T.astype(o_ref.dtype)
        return 0

    lax.fori_loop(0, heads // nh, head_group, 0)


def _t5_bucket(dist):
    n = jnp.maximum(dist, 0)
    max_exact = REL_BUCKETS // 2
    scaled = (jnp.log(jnp.maximum(n, max_exact).astype(F32) / max_exact)
              / math.log(REL_MAX_DISTANCE / max_exact))
    large = jnp.minimum(max_exact + (scaled * (REL_BUCKETS - max_exact)).astype(jnp.int32), REL_BUCKETS - 1)
    return jnp.where(n < max_exact, n, large)


def _bias_table(rel_bias, dist, shift):
    onehot = (_t5_bucket(dist)[None] == jnp.arange(REL_BUCKETS).reshape((-1,) + (1,) * dist.ndim)).astype(F32)
    vals = jnp.tensordot(rel_bias.T, onehot, axes=1, precision=lax.Precision.HIGHEST)
    return jnp.where(dist >= 0, (vals - shift) * LOG2E, NEG).astype(F32)


def nsa_attention(qt, kc, vct, ks, vst, kw, vwt, gates, rel_bias):
    b, h, nq = qt.shape[:3]
    s = nq * TQ
    g = kc.shape[1]
    heads = h // g
    n_cmp_pad = kc.shape[2]
    n_sel = s // SEL_BLOCK
    k_sel = min(SEL_TOPK, n_sel)
    half_t = TQ // 2
    nh = NSA_HEADS_PER_ITER
    nc = min(NSA_CMP_HEADS_PER_ITER, heads)
    assert heads % nh == 0 and heads % nc == 0
    n_blk = DKP - NSA_QK_DIM
    assert s % (2 * TQ) == 0 and WINDOW == 2 * TQ and n_sel <= n_blk and TQ % SEL_BLOCK == 0
    assert half_t >= REL_MAX_DISTANCE
    assert n_cmp_pad == s // CMP_STRIDE and n_cmp_pad % V7X_LANES == 0

    far = rel_bias[REL_BUCKETS - 1][:, None, None]
    per_tile = TQ // CMP_STRIDE
    band = 2 * per_tile
    assert CMP_STRIDE * (per_tile + 1) - (CMP_BLOCK - 1) >= REL_MAX_DISTANCE and n_cmp_pad % per_tile == 0
    a_rel = jnp.arange(band) - per_tile
    cb = _bias_table(rel_bias, jnp.arange(TQ)[None, :] - CMP_STRIDE * a_rel[:, None] - (CMP_BLOCK - 1), far)
    ij = jnp.arange(half_t)
    t0 = _bias_table(rel_bias, ij[None, :] - ij[:, None], far)
    t1 = _bias_table(rel_bias, half_t + ij[None, :] - ij[:, None], far)

    n_cmp = (s - CMP_BLOCK) // CMP_STRIDE + 1
    cmp_start = CMP_STRIDE * np.arange(n_cmp_pad)
    sel_start = SEL_BLOCK * np.arange(n_blk)
    ovl = ((cmp_start[None, :] < (sel_start + SEL_BLOCK)[:, None])
           & ((cmp_start + CMP_BLOCK)[None, :] > sel_start[:, None])
           & (np.arange(n_cmp_pad) < n_cmp)[None, :] & (np.arange(n_blk) < n_sel)[:, None])
    ovl = jnp.asarray(ovl, BF16)

    once = pl.Buffered(1)
    rows_spec = lambda n, d: pl.BlockSpec((1, 1, n, d), lambda bi, gi, i: (bi, gi, 0, 0), pipeline_mode=once)
    vt_spec = pl.BlockSpec((1, 1, nq, DV, TQ), lambda bi, gi, i: (bi, gi, 0, 0, 0), pipeline_mode=once)
    gate_spec = lambda br: pl.BlockSpec((heads, 1, DV, TQ),
                                        lambda bi, gi, i, br=br: (br * g + gi, bi * nq + i, 0, 0))
    tbl_spec = lambda shape: pl.BlockSpec((heads,) + shape, lambda bi, gi, i: (gi, 0, 0), pipeline_mode=once)
    kern = functools.partial(_nsa_attn_kernel, heads=heads, n_cmp_pad=n_cmp_pad, n_sel=n_sel, k_sel=k_sel)
    return pl.pallas_call(
        kern,
        grid=(b, g, nq),
        in_specs=[
            pl.BlockSpec((1, heads, 1, DKP, TQ), lambda bi, gi, i: (bi, gi, i, 0, 0)),
            rows_spec(n_cmp_pad, DKP), rows_spec(DV, n_cmp_pad),
            rows_spec(s, DKP), vt_spec, rows_spec(s, DKP), vt_spec,
            gate_spec(0), gate_spec(1), gate_spec(2),
            tbl_spec((band, TQ)), tbl_spec((half_t, half_t)), tbl_spec((half_t, half_t)),
            pl.BlockSpec((n_blk, n_cmp_pad), lambda bi, gi, i: (0, 0)),
        ],
        out_specs=pl.BlockSpec((heads, TQ, DV), lambda bi, gi, i: (gi, bi * nq + i, 0)),
        out_shape=jax.ShapeDtypeStruct((h, b * s, DV), BF16),
        scratch_shapes=[
            pltpu.VMEM((2, heads, 1, TQ), F32), pltpu.VMEM((2, heads, DV + SUM_ROWS, TQ), F32),
            pltpu.VMEM((heads, DV, TQ), F32), pltpu.VMEM((heads, DKP, TQ), BF16),
            pltpu.VMEM((nc, TQ // CMP_STRIDE + n_cmp_pad, TQ), F32),
        ],
        compiler_params=_cparams(("parallel", "parallel", "arbitrary")),
        name="nsa_attention",
    )(qt, kc, vct, ks, vst, kw, vwt, gates, gates, gates, cb, t0, t1, ovl)


def _mla_attn_kernel(qt_ref, k_ref, vt_ref, z_ref, o_ref, m_scr, acc_scr, *, heads):
    i = pl.program_id(2)
    kj = lax.broadcasted_iota(jnp.int32, (TQ, TQ), 0)
    qi = lax.broadcasted_iota(jnp.int32, (TQ, TQ), 1)
    causal = jnp.where(kj <= qi, 0.0, NEG)

    ft = FAR_TILES

    def block(first, n_tiles, add):
        rows = pl.ds(pl.multiple_of(first * TQ, TQ), n_tiles * TQ)
        sts = [_dot(k_ref[0, r, rows, :], qt_ref[0, r, 0]) for r in range(heads)]
        for r in range(heads):
            vts = tuple(vt_ref[0, r, first + j] for j in range(n_tiles))
            _attn_update(sts[r], vts, add, m_scr.at[r], acc_scr.at[r])

    _attn_init(m_scr, acc_scr)

    def far(c, _):
        block(ft * c, ft, None)
        return 0

    lax.fori_loop(0, jnp.maximum(i - 1, 0) // ft, far, 0)

    def near(n_tiles):
        block(i - n_tiles + 1, n_tiles, [None] * (n_tiles - 1) + [causal])

    pl.when(i == 0)(functools.partial(near, 1))
    for k in range(ft):
        pl.when((i >= 1) & ((i - 1) % ft == k))(functools.partial(near, 2 + k))
    for r in range(heads):
        ot = _attn_result(acc_scr.at[r]) * z_ref[r, 0].astype(F32)
        o_ref[r] = ot.T.astype(o_ref.dtype)


def mla_attention(qt, k, vt, zg):
    b, h, nq = qt.shape[:3]
    s = nq * TQ
    hb = MLA_HEADS_PER_STEP
    assert s % (2 * TQ) == 0 and h % hb == 0
    return pl.pallas_call(
        functools.partial(_mla_attn_kernel, heads=hb),
        grid=(b, h // hb, nq),
        in_specs=[
            pl.BlockSpec((1, hb, 1, DKP, TQ), lambda bi, hi, i: (bi, hi, i, 0, 0)),
            pl.BlockSpec((1, hb, s, DKP), lambda bi, hi, i: (bi, hi, 0, 0)),
            pl.BlockSpec((1, hb, nq, DV, TQ), lambda bi, hi, i: (bi, hi, 0, 0, 0)),
            pl.BlockSpec((hb, 1, DV, TQ), lambda bi, hi, i: (hi, bi * nq + i, 0, 0)),
        ],
        out_specs=pl.BlockSpec((hb, TQ, DV), lambda bi, hi, i: (hi, bi * nq + i, 0)),
        out_shape=jax.ShapeDtypeStruct((h, b * s, DV), BF16),
        scratch_shapes=[pltpu.VMEM((hb, 1, TQ), F32), pltpu.VMEM((hb, DV + SUM_ROWS, TQ), F32)],
        compiler_params=_cparams(("parallel", "parallel", "arbitrary")),
        name="mla_attention",
    )(qt, k, vt, zg)


def _pad_heads(w, n_heads, d):
    k = w.shape[0]
    return jnp.pad(w.reshape(k, n_heads, d), ((0, 0), (0, 0), (0, DKP - d))).reshape(k, n_heads * DKP).astype(BF16)


def _head_rows_out(b, s, h, tm, hb, width, dtype):
    nt = s // tm
    spec = pl.BlockSpec((1, hb, tm, width), lambda i, j: (i // nt, j, i % nt, 0))
    return jax.ShapeDtypeStruct((b, h, s, width), dtype), spec


def _head_tiles_out(b, s, h, tm, hb, width, dtype):
    nt = s // tm
    spec = pl.BlockSpec((1, hb, tm // TQ, width, TQ), lambda i, j: (i // nt, j, i % nt, 0, 0))
    return jax.ShapeDtypeStruct((b, h, s // TQ, width, TQ), dtype), spec


def nsa_layer(x2, b, s, norm_w, rel_bias, w_in, q_norm, k_norm, pe_k, w1_k, w2_k, pe_v, w1_v, w2_v, w_out):
    t, d_model = x2.shape
    h, g, dk, dv = NSA_HEADS, NSA_KV_GROUPS, NSA_QK_DIM, NSA_V_DIM
    qw, kvw, zw = h * dk, g * (dk + dv), h * dv
    tm = _pick(s, 1024, TQ)
    hn = rmsnorm_rows(x2, norm_w, d_model, BF16, "nsa_prenorm")

    w_all = w_in.astype(BF16)
    hb = _pick(h, 4)
    q_shape, q_spec = _head_tiles_out(b, s, h, tm, hb, DKP, BF16)
    qt = matmul(hn, w_all, functools.partial(_ep_q_heads, hb=hb, n_true=dk, scale=LOG2E * dk ** -0.5, rope=False),
                tm=tm, tn=hb * dk, out_shapes=q_shape, out_specs=q_spec, cols=(0, qw),
                extra=(jnp.broadcast_to(q_norm[:, None], (dk, TQ)),),
                extra_specs=(pl.BlockSpec((dk, TQ), lambda i, j: (0, 0)),), slabs=True, name="nsa_q_proj")

    def kv_proj(br, normed, block_onehot=False):
        w_br = w_all[:, qw + br * kvw: qw + (br + 1) * kvw]
        w_kv = jnp.concatenate([_pad_heads(w_br[:, :g * dk], g, dk), w_br[:, g * dk:]], axis=1)
        k_out = _head_rows_out(b, s, g, tm, g, DKP, BF16 if normed else F32)
        v_out = _head_tiles_out(b, s, g, tm, g, dv, BF16) if normed else _head_rows_out(b, s, g, tm, g, dv, F32)
        gain = jnp.pad(k_norm[br], (0, DKP - dk)).reshape(1, DKP)
        ep = functools.partial(_ep_nsa_kv, groups=g, n_true=dk, normed=normed,
                               seq_tiles=s // tm if block_onehot else 0)
        return matmul(hn, w_kv, ep,
                      tm=tm, tn=g * (DKP + dv), out_shapes=(k_out[0], v_out[0]), out_specs=(k_out[1], v_out[1]),
                      extra=(gain,), extra_specs=(pl.BlockSpec((1, DKP), lambda i, j: (0, 0)),),
                      slabs=True, name="nsa_kv_proj_%d" % br)

    kc_raw, vc_raw = kv_proj(0, False)
    ks, vst = kv_proj(1, True, block_onehot=True)
    kw, vwt = kv_proj(2, True)
    kc = compress(kc_raw, pe_k, w1_k, w2_k, k_norm[0], dk, "nsa_compress_k")
    vc = compress(vc_raw, pe_v, w1_v, w2_v, jnp.ones((dv,), F32), 0, "nsa_compress_v")
    vct = jnp.swapaxes(vc, -1, -2)

    z0 = qw + 3 * kvw
    w_g = w_all[:, z0 + 3 * zw:]
    sg = matmul(hn, w_g, _ep_sigmoid, tm=tm, tn=3 * h,
                out_shapes=jax.ShapeDtypeStruct((t, 3 * h), F32),
                out_specs=pl.BlockSpec((tm, 3 * h), lambda i, j: (i, j)), name="nsa_gate_proj")
    hbz = max(c for c in range(1, 9) if (3 * h) % c == 0 and z0 % (c * dv) == 0)
    n_zt = 3 * h // hbz
    sub = 8
    sg_tiles = jnp.pad(sg.reshape(t // TQ, TQ, n_zt, hbz).transpose(2, 0, 3, 1),
                       ((0, 0), (0, 0), (0, sub - hbz), (0, 0)))
    gates = matmul(hn, w_all, functools.partial(_ep_gate_heads, hb=hbz, gated=True), tm=tm, tn=hbz * dv,
                   out_shapes=jax.ShapeDtypeStruct((3 * h, t // TQ, dv, TQ), BF16),
                   out_specs=pl.BlockSpec((hbz, tm // TQ, dv, TQ), lambda i, j: (j, i, 0, 0)),
                   cols=(z0, 3 * zw), extra=(sg_tiles,),
                   extra_specs=(pl.BlockSpec((1, tm // TQ, sub, TQ), lambda i, j: (j, i, 0, 0)),),
                   slabs=True, name="nsa_z_proj")

    o = nsa_attention(qt, kc, vct, ks, vst, kw, vwt, gates, rel_bias)
    tmo = _pick(t, 512, 8)
    tno = _pick(d_model, 512, V7X_LANES)
    return matmul(o, w_out.astype(BF16), _ep_residual, tm=tmo, tn=tno,
                  out_shapes=jax.ShapeDtypeStruct((t, d_model), F32),
                  out_specs=pl.BlockSpec((tmo, tno), lambda i, j: (i, j)),
                  extra=(x2,), extra_specs=(pl.BlockSpec((tmo, tno), lambda i, j: (i, j)),), name="nsa_out_proj")


def mla_layer(x2, b, s, positions, norm_w, w_in, q_a_norm, w_q_b, kv_a_norm, w_kv_b, q_norm, k_norm, w_out):
    t, d_model = x2.shape
    h = MLA_HEADS
    dqk = MLA_NOPE_DIM + MLA_ROPE_DIM
    tm = _pick(s, 1024, TQ)
    hn = rmsnorm_rows(x2, norm_w, d_model, BF16, "mla_prenorm")

    half = MLA_ROPE_DIM // 2
    inv_freq = ROPE_BASE ** (-jnp.arange(half, dtype=F32) / half)
    ang = positions.astype(F32).reshape(t, 1) * inv_freq[None, :]
    zeros = jnp.zeros((t, V7X_LANES - MLA_ROPE_DIM), F32)
    cos2 = jnp.concatenate([jnp.cos(ang), jnp.cos(ang), zeros], axis=1)
    sin2 = jnp.concatenate([-jnp.sin(ang), jnp.sin(ang), zeros], axis=1)
    rope_spec = pl.BlockSpec((tm, V7X_LANES), lambda i, j: (i, 0))
    cos2_t = jnp.swapaxes(cos2[:, :MLA_ROPE_DIM].reshape(t // TQ, TQ, MLA_ROPE_DIM), 1, 2)
    sin2_t = jnp.swapaxes(sin2[:, :MLA_ROPE_DIM].reshape(t // TQ, TQ, MLA_ROPE_DIM), 1, 2)
    rope_t_spec = pl.BlockSpec((tm // TQ, MLA_ROPE_DIM, TQ), lambda i, j: (i, 0, 0))
    row_spec = lambda w: pl.BlockSpec((1, w), lambda i, j: (0, 0))

    c0, c1, c2 = MLA_Q_LORA, MLA_Q_LORA + MLA_KV_LORA, MLA_Q_LORA + MLA_KV_LORA + MLA_ROPE_DIM
    w_lat = jnp.concatenate([w_in[:, :c1], jnp.pad(w_in[:, c1:c2], ((0, 0), (0, DKP - MLA_ROPE_DIM)))],
                            axis=1).astype(BF16)
    lat_w = c1 + DKP
    tml = _pick(s, 512, 8)
    row_out = lambda w, dt: (jax.ShapeDtypeStruct((t, w), dt), pl.BlockSpec((tml, w), lambda i, j: (i, 0)))
    lat_outs = (row_out(c0, BF16), row_out(c1 - c0, BF16), row_out(V7X_LANES, F32))
    cq, ckv, kpe = matmul(hn, w_lat, functools.partial(_ep_mla_latent, c0=c0, c1=c1), tm=tml, tn=lat_w,
                          out_shapes=tuple(o[0] for o in lat_outs), out_specs=tuple(o[1] for o in lat_outs),
                          extra=(q_a_norm.reshape(1, c0), kv_a_norm.reshape(1, c1 - c0)),
                          extra_specs=(row_spec(c0), row_spec(c1 - c0)), w_once=True, name="mla_latent_proj")

    hbz = _pick(h, 8)
    zg = matmul(hn, w_in[:, c2:].astype(BF16), functools.partial(_ep_gate_heads, hb=hbz, gated=False),
                tm=tm, tn=hbz * DV, out_shapes=jax.ShapeDtypeStruct((h, t // TQ, DV, TQ), BF16),
                out_specs=pl.BlockSpec((hbz, tm // TQ, DV, TQ), lambda i, j: (j, i, 0, 0)),
                slabs=True, name="mla_z_proj")

    hb = _pick(h, 4)
    q_shape, q_spec = _head_tiles_out(b, s, h, tm, hb, DKP, BF16)
    qt = matmul(cq, w_q_b.astype(BF16),
                functools.partial(_ep_q_heads, hb=hb, n_true=dqk, scale=LOG2E * dqk ** -0.5, rope=True),
                tm=tm, tn=hb * dqk, out_shapes=q_shape, out_specs=q_spec,
                extra=(jnp.broadcast_to(q_norm[:, None], (dqk, TQ)), cos2_t, sin2_t),
                extra_specs=(pl.BlockSpec((dqk, TQ), lambda i, j: (0, 0)), rope_t_spec, rope_t_spec),
                slabs=True, name="mla_q_proj")

    k_shape, k_spec = _head_rows_out(b, s, h, tm, hb, DKP, BF16)
    v_shape, v_spec = _head_tiles_out(b, s, h, tm, hb, DV, BF16)
    gain_n = k_norm[:MLA_NOPE_DIM].reshape(1, MLA_NOPE_DIM)
    gain_r = jnp.pad(k_norm[MLA_NOPE_DIM:], (0, V7X_LANES - MLA_ROPE_DIM)).reshape(1, V7X_LANES)
    k, vt = matmul(ckv, w_kv_b.astype(BF16), functools.partial(_ep_mla_kv, hb=hb),
                   tm=tm, tn=hb * DKP, out_shapes=(k_shape, v_shape), out_specs=(k_spec, v_spec),
                   extra=(kpe, gain_n, gain_r, cos2, sin2),
                   extra_specs=(rope_spec, row_spec(MLA_NOPE_DIM), row_spec(V7X_LANES), rope_spec, rope_spec),
                   slabs=True, name="mla_kv_proj")

    o = mla_attention(qt, k, vt, zg)
    tmo = _pick(t, 512, 8)
    tno = _pick(d_model, 512, V7X_LANES)
    return matmul(o, w_out.astype(BF16), _ep_residual, tm=tmo, tn=tno,
                  out_shapes=jax.ShapeDtypeStruct((t, d_model), F32),
                  out_specs=pl.BlockSpec((tmo, tno), lambda i, j: (i, j)),
                  extra=(x2,), extra_specs=(pl.BlockSpec((tmo, tno), lambda i, j: (i, j)),), name="mla_out_proj")


def kernel(x, positions, norm_w, rel_bias, nsa_w_in, nsa_q_norm, nsa_k_norm, nsa_cmp_pe_k, nsa_cmp_w1_k,
           nsa_cmp_w2_k, nsa_cmp_pe_v, nsa_cmp_w1_v, nsa_cmp_w2_v, nsa_w_out, mla_w_in, mla_q_a_norm,
           mla_w_q_b, mla_kv_a_norm, mla_w_kv_b, mla_q_norm, mla_k_norm, mla_w_out):
    b, s, d_model = x.shape
    x2 = x.reshape(b * s, d_model)
    depth = norm_w.shape[0]
    for layer in range(depth):
        j = layer // 2
        if layer % 2 == 0:
            x2 = nsa_layer(x2, b, s, norm_w[layer], rel_bias, nsa_w_in[j], nsa_q_norm[j], nsa_k_norm[j],
                           nsa_cmp_pe_k[j], nsa_cmp_w1_k[j], nsa_cmp_w2_k[j], nsa_cmp_pe_v[j], nsa_cmp_w1_v[j],
                           nsa_cmp_w2_v[j], nsa_w_out[j])
        else:
            x2 = mla_layer(x2, b, s, positions, norm_w[layer], mla_w_in[j], mla_q_a_norm[j], mla_w_q_b[j],
                           mla_kv_a_norm[j], mla_w_kv_b[j], mla_q_norm[j], mla_k_norm[j], mla_w_out[j])
    return x2.reshape(b, s, d_model)
```

```python
import functools
import math

import numpy as np
import jax
import jax.numpy as jnp
from jax import lax
from jax.experimental import pallas as pl
from jax.experimental.pallas import tpu as pltpu

F32 = jnp.float32
BF16 = jnp.bfloat16

EPS = 1e-6
REL_BUCKETS = 32
REL_MAX_DISTANCE = 128
NSA_HEADS = 64
NSA_KV_GROUPS = 4
NSA_QK_DIM = 192
NSA_V_DIM = 128
CMP_BLOCK = 32
CMP_STRIDE = 16
SEL_BLOCK = 64
SEL_TOPK = 16
WINDOW = 512
MLA_HEADS = 64
MLA_Q_LORA = 1536
MLA_KV_LORA = 512
MLA_NOPE_DIM = 128
MLA_ROPE_DIM = 64
MLA_V_DIM = 128
ROPE_BASE = 10000.0

V7X_LANES = 128
V7X_VMEM_BYTES = 64 * 1024 * 1024
VMEM_LIMIT_BYTES = V7X_VMEM_BYTES - 8 * 1024 * 1024

DKP = 2 * V7X_LANES
DV = V7X_LANES
TQ = 256
NEG = -1e30
M_INIT = -1e20
SUM_ROWS = 16
LOG2E = math.log2(math.e)
MLA_HEADS_PER_STEP = 8
MLA_HEADS_PER_ITER = 4
NSA_HEADS_PER_ITER = 4
NSA_CMP_HEADS_PER_ITER = 16
FAR_TILES = 4


def _cparams(sem):
    return pltpu.CompilerParams(dimension_semantics=sem, vmem_limit_bytes=VMEM_LIMIT_BYTES)


def _dot(a, b):
    return jnp.dot(a, b, preferred_element_type=F32)


def _sigmoid(x):
    return 1.0 / (1.0 + jnp.exp(-x))


def _silu(x):
    return x * _sigmoid(x)


def _pick(n, target, unit=1):
    best = unit
    c = unit
    while c <= min(n, max(target, unit)):
        if n % c == 0:
            best = c
        c += unit
    return best


def _rmsnorm_kernel(x_ref, g_ref, o_ref, *, inv_n):
    x = x_ref[...].astype(F32)
    ms = jnp.sum(x * x, axis=-1, keepdims=True) * inv_n
    o_ref[...] = (x * lax.rsqrt(ms + EPS) * g_ref[...]).astype(o_ref.dtype)


def rmsnorm_rows(x, gain, n_true, out_dtype, name):
    m, d = x.shape
    tm = _pick(m, max(8, (2 * 1024 * 1024) // (4 * d)), 8)
    return pl.pallas_call(
        functools.partial(_rmsnorm_kernel, inv_n=1.0 / n_true),
        grid=(m // tm,),
        in_specs=[pl.BlockSpec((tm, d), lambda i: (i, 0)), pl.BlockSpec((1, d), lambda i: (0, 0))],
        out_specs=pl.BlockSpec((tm, d), lambda i: (i, 0)),
        out_shape=jax.ShapeDtypeStruct((m, d), out_dtype),
        compiler_params=_cparams(("parallel",)),
        name=name,
    )(x, gain.reshape(1, d).astype(F32))


def _matmul_kernel(*refs, epilogue, n_in, slabs):
    a_ref, w_ref = refs[0], refs[1]
    extra, outs = refs[2:n_in], refs[n_in:]
    if len(a_ref.shape) == 3:
        a = jnp.concatenate([a_ref[h] for h in range(a_ref.shape[0])], axis=1)
    else:
        a = a_ref[...]
    w = w_ref[...]
    if not slabs:
        epilogue(_dot(a, w), extra, outs)
        return
    n = a.shape[0] // TQ
    acc = _dot(a[:TQ], w)
    for u in range(n):
        nxt = _dot(a[(u + 1) * TQ:(u + 2) * TQ], w) if u + 1 < n else None
        epilogue(acc, u, extra, outs)
        acc = nxt


def matmul(a, w, epilogue, *, tm, tn, out_shapes, out_specs, extra=(), extra_specs=(), slabs=False,
           w_once=False, cols=None, name):
    col0, n = cols if cols else (0, w.shape[1])
    assert col0 % tn == 0
    jb = col0 // tn
    if a.ndim == 3:
        m, k = a.shape[1], a.shape[0] * a.shape[2]
        a_spec = pl.BlockSpec((a.shape[0], tm, a.shape[2]), lambda i, j: (0, i, 0))
    else:
        m, k = a.shape
        a_spec = pl.BlockSpec((tm, k), lambda i, j: (i, 0))
    assert m % tm == 0 and n % tn == 0 and k == w.shape[0], (m, tm, n, tn, k)
    assert not slabs or tm % TQ == 0
    assert not w_once or tn == n
    w_spec = pl.BlockSpec((k, tn), lambda i, j: (0, jb + j), pipeline_mode=pl.Buffered(1) if w_once else None)
    in_specs = [a_spec, w_spec]
    in_specs += list(extra_specs)
    return pl.pallas_call(
        functools.partial(_matmul_kernel, epilogue=epilogue, n_in=2 + len(extra), slabs=slabs),
        grid=(m // tm, n // tn),
        in_specs=in_specs,
        out_specs=out_specs,
        out_shape=out_shapes,
        compiler_params=_cparams(("parallel", "arbitrary")),
        name=name,
    )(a, w, *extra)


def _ep_store(acc, extra, outs):
    outs[0][...] = acc.astype(outs[0].dtype)


def _ep_sigmoid(acc, extra, outs):
    outs[0][...] = _sigmoid(acc)


def _ep_residual(acc, extra, outs):
    outs[0][...] = extra[0][...] + acc


def _ep_mla_latent(acc, extra, outs, *, c0, c1):
    def normed(x, gain):
        ms = jnp.mean(x * x, axis=-1, keepdims=True)
        return x * lax.rsqrt(ms + EPS) * gain

    outs[0][...] = normed(acc[:, :c0], extra[0][...]).astype(outs[0].dtype)
    outs[1][...] = normed(acc[:, c0:c1], extra[1][...]).astype(outs[1].dtype)
    outs[2][...] = acc[:, c1:c1 + V7X_LANES]


def _ep_nsa_kv(acc, u, extra, outs, *, groups, n_true, normed, seq_tiles=0):
    rows = pl.ds(u * TQ, TQ)
    gain = extra[0][...]
    if seq_tiles:
        tm = outs[0].shape[2]
        pos = (pl.program_id(0) % seq_tiles) * tm + u * TQ + lax.broadcasted_iota(jnp.int32, (TQ, 1), 0)
        slot = n_true + jnp.right_shift(pos, int(math.log2(SEL_BLOCK)))
        block_lane = lax.broadcasted_iota(jnp.int32, (TQ, DKP), 1) == slot
    for gi in range(groups):
        k = acc[:, gi * DKP:(gi + 1) * DKP]
        v = acc[:, groups * DKP + gi * DV:groups * DKP + (gi + 1) * DV]
        if normed:
            ms = jnp.sum(k * k, axis=-1, keepdims=True) * (1.0 / n_true)
            kn = k * lax.rsqrt(ms + EPS) * gain
            if seq_tiles:
                kn = jnp.where(block_lane, 1.0, kn)
            outs[0][0, gi, rows, :] = kn.astype(outs[0].dtype)
            outs[1][0, gi, u] = v.T.astype(outs[1].dtype)
        else:
            outs[0][0, gi, rows, :] = k
            outs[1][0, gi, rows, :] = v


def _rope_lanes(x, cos2, sin2):
    half = MLA_ROPE_DIM // 2
    lane = lax.broadcasted_iota(jnp.int32, x.shape, 1)
    up = pltpu.roll(x, V7X_LANES - half, axis=1)
    down = pltpu.roll(x, half, axis=1)
    rot = jnp.where(lane < half, up, down)
    return x * cos2 + rot * sin2


def _rope_rows(x, cos2, sin2):
    half = MLA_ROPE_DIM // 2
    rot = jnp.concatenate([x[half:], x[:half]], axis=0)
    return x * cos2 + rot * sin2


def _ep_q_heads(acc, u, extra, outs, *, hb, n_true, scale, rope):
    acc_t = acc.T
    gain = extra[0][...]
    zeros = jnp.zeros((DKP - n_true, TQ), F32)
    for r in range(hb):
        y = acc_t[r * n_true:(r + 1) * n_true]
        ms = jnp.sum(y * y, axis=0, keepdims=True) * (1.0 / n_true)
        yn = y * lax.rsqrt(ms + EPS) * gain
        if rope:
            nope = n_true - MLA_ROPE_DIM
            yn = jnp.concatenate([yn[:nope], _rope_rows(yn[nope:], extra[1][u], extra[2][u])], axis=0)
        outs[0][0, r, u] = (jnp.concatenate([yn, zeros], axis=0) * scale).astype(outs[0].dtype)


def _ep_gate_heads(acc, u, extra, outs, *, hb, gated):
    acc_t = acc.T
    for r in range(hb):
        a = _silu(acc_t[r * DV:(r + 1) * DV])
        if gated:
            a = a * extra[0][0, u, r:r + 1, :]
        outs[0][r, u] = a.astype(outs[0].dtype)


def _ep_mla_kv(acc, u, extra, outs, *, hb):
    rows = pl.ds(u * TQ, TQ)
    kpe = extra[0][rows, :]
    gain_n = extra[1][...]
    gain_r = extra[2][...]
    ss_pe = jnp.sum(kpe * kpe, axis=-1, keepdims=True)
    kr = _rope_lanes(kpe * gain_r, extra[3][rows, :], extra[4][rows, :])
    inv_n = 1.0 / (MLA_NOPE_DIM + MLA_ROPE_DIM)
    for r in range(hb):
        kn = acc[:, r * DKP:r * DKP + V7X_LANES]
        v = acc[:, r * DKP + V7X_LANES:(r + 1) * DKP]
        rs = lax.rsqrt((jnp.sum(kn * kn, axis=-1, keepdims=True) + ss_pe) * inv_n + EPS)
        outs[0][0, r, rows, :] = jnp.concatenate([kn * rs * gain_n, kr * rs], axis=1).astype(outs[0].dtype)
        outs[1][0, r, u] = v.T.astype(outs[1].dtype)


def _compress_kernel(xa_ref, xb_ref, pea_ref, peb_ref, w1a_ref, w1b_ref, w2_ref, g_ref, o_ref, *, norm_n):
    xa = (xa_ref[0, 0] + pea_ref[...]).astype(BF16)
    xb = (xb_ref[0, 0] + peb_ref[...]).astype(BF16)
    pre = _dot(xa, w1a_ref[...]) + _dot(xb, w1b_ref[...])
    y = _dot(_silu(pre).astype(BF16), w2_ref[...])
    if norm_n:
        ms = jnp.sum(y * y, axis=-1, keepdims=True) * (1.0 / norm_n)
        y = y * lax.rsqrt(ms + EPS) * g_ref[...]
    o_ref[0, 0] = y.astype(o_ref.dtype)


def compress(x_raw, pe, w1, w2, gain, norm_n, name):
    b, g, s, dp = x_raw.shape
    d = pe.shape[1]
    nch = s // CMP_STRIDE
    half = CMP_STRIDE * dp
    chunks = x_raw.reshape(b, g, nch, half)
    xb = jnp.concatenate([chunks[:, :, 1:], jnp.zeros((b, g, 1, half), F32)], axis=2)
    pe_p = jnp.pad(pe, ((0, 0), (0, dp - d)))
    pea = pe_p[:CMP_STRIDE].reshape(1, half)
    peb = pe_p[CMP_STRIDE:].reshape(1, half)
    w1p = jnp.pad(w1.reshape(CMP_BLOCK, d, d), ((0, 0), (0, dp - d), (0, dp - d))).astype(BF16)
    w1a = w1p[:CMP_STRIDE].reshape(half, dp)
    w1b = w1p[CMP_STRIDE:].reshape(half, dp)
    w2p = jnp.pad(w2, ((0, dp - d), (0, dp - d))).astype(BF16)
    gp = jnp.pad(gain, (0, dp - d)).reshape(1, dp).astype(F32)
    full = lambda shape: pl.BlockSpec(shape, lambda bi, gi: (0,) * len(shape))
    return pl.pallas_call(
        functools.partial(_compress_kernel, norm_n=norm_n),
        grid=(b, g),
        in_specs=[
            pl.BlockSpec((1, 1, nch, half), lambda bi, gi: (bi, gi, 0, 0)),
            pl.BlockSpec((1, 1, nch, half), lambda bi, gi: (bi, gi, 0, 0)),
            full((1, half)), full((1, half)), full((half, dp)), full((half, dp)), full((dp, dp)), full((1, dp)),
        ],
        out_specs=pl.BlockSpec((1, 1, nch, dp), lambda bi, gi: (bi, gi, 0, 0)),
        out_shape=jax.ShapeDtypeStruct((b, g, nch, dp), BF16),
        compiler_params=_cparams(("parallel", "arbitrary")),
        name=name,
    )(chunks, xb, pea, peb, w1a, w1b, w2p, gp)


def _attn_init(m_ref, acc_ref):
    m_ref[...] = jnp.full(m_ref.shape, M_INIT, F32)
    acc_ref[...] = jnp.zeros(acc_ref.shape, F32)


def _attn_update(st, vts, add, m_ref, acc_ref):
    ones = jnp.ones((SUM_ROWS, TQ), BF16)
    for j, vt in enumerate(vts):
        sj = st[j * TQ:(j + 1) * TQ]
        if add is not None and add[j] is not None:
            sj = sj + add[j]
        m_prev = m_ref[...]
        m_new = jnp.maximum(m_prev, jnp.max(sj, axis=0, keepdims=True))
        alpha = jnp.exp2(m_prev - m_new)
        p = jnp.exp2(sj - m_new).astype(BF16)
        acc_ref[...] = alpha * acc_ref[...] + _dot(jnp.concatenate([vt, ones], axis=0), p)
        m_ref[...] = m_new


def _attn_result(acc_ref):
    acc = acc_ref[...]
    return acc[:DV] / acc[DV:DV + 1]


def _tile_ds(idx, size):
    return pl.ds(pl.multiple_of(idx * size, size), size)


def _nsa_attn_kernel(qt_ref, kc_ref, vct_ref, ks_ref, vst_ref, kw_ref, vwt_ref, ac_ref, as_ref, aw_ref,
                     cb_ref, t0_ref, t1_ref, ovl_ref, o_ref,
                     m_scr, acc_scr, oc_scr, qa_scr, cmpb_scr, *, heads, n_cmp_pad, n_sel, k_sel):
    nh = NSA_HEADS_PER_ITER
    i = pl.program_id(2)
    tok = i * TQ + lax.broadcasted_iota(jnp.int32, (1, TQ), 1)
    per_tile = TQ // CMP_STRIDE
    band = cb_ref.shape[1]

    valid_c = (tok >= CMP_BLOCK - 1).astype(F32)
    nrow = lax.broadcasted_iota(jnp.int32, (per_tile + n_cmp_pad, TQ), 0)
    base = jnp.where(nrow >= (i + 2) * per_tile, NEG, 0.0)
    nc = cmpb_scr.shape[0]
    for u in range(nc):
        cmpb_scr[u] = base
    kc = kc_ref[0, 0]
    vct = vct_ref[0, 0]

    def cmp_group(gi, psum):
        rs = [gi * nc + u for u in range(nc)]
        for u, r in enumerate(rs):
            cmpb_scr[u, pl.ds(pl.multiple_of(i * per_tile, per_tile), band), :] = cb_ref[r]
        st_all = _dot(kc, jnp.concatenate([qt_ref[0, r, 0] for r in rs], axis=1))
        for u, r in enumerate(rs):
            st = st_all[:, u * TQ:(u + 1) * TQ] + cmpb_scr[u, per_tile:per_tile + n_cmp_pad, :]
            p = jnp.exp2(st - jnp.max(st, axis=0, keepdims=True))
            pn = p * (valid_c / jnp.sum(p, axis=0, keepdims=True))
            oc_scr[r] = _dot(vct, pn.astype(BF16)) * ac_ref[r, 0].astype(F32)
            psum = psum + pn
        return psum

    psum = lax.fori_loop(0, heads // nc, cmp_group, jnp.zeros((n_cmp_pad, TQ), F32))

    ovl = ovl_ref[...]
    p_hi = psum.astype(BF16)
    rem = psum - p_hi.astype(F32)
    p_mid = rem.astype(BF16)
    p_lo = (rem - p_mid.astype(F32)).astype(BF16)
    imp = _dot(ovl, p_hi) + _dot(ovl, p_mid) + _dot(ovl, p_lo)

    n_blk = DKP - NSA_QK_DIM
    blk = lax.broadcasted_iota(jnp.int32, (n_blk, TQ), 0)
    blk_f = blk.astype(F32)
    cur = jnp.right_shift(tok, int(math.log2(SEL_BLOCK)))
    forced = (blk == 0) | (blk == cur) | (blk == cur - 1)
    future = blk * SEL_BLOCK > tok
    val = jnp.where(forced, 1e30, jnp.where(future, -1.0, imp))
    val = jnp.where(blk < n_sel, val, -2.0)

    def pick(_, c):
        v, sel = c
        top = jnp.max(v, axis=0, keepdims=True)
        first = jnp.min(jnp.where(v == top, blk_f, float(n_blk)), axis=0, keepdims=True)
        hit = blk_f == first
        return jnp.where(hit, -3.0, v), jnp.where(hit, 1.0, sel)

    _, sel = lax.fori_loop(0, k_sel, pick, (val, jnp.zeros((n_blk, TQ), F32)), unroll=True)

    mask_rows = ((sel - 1.0) * (-NEG)).astype(BF16)
    for r in range(heads):
        qa_scr[r] = jnp.concatenate([qt_ref[0, r, 0, 0:NSA_QK_DIM, :], mask_rows], axis=0)

    ft = FAR_TILES
    n_far = jnp.maximum(i - 1, 0) // ft
    kj = lax.broadcasted_iota(jnp.int32, (TQ, TQ), 0)
    qi = lax.broadcasted_iota(jnp.int32, (TQ, TQ), 1)
    half_t = TQ // 2
    zb = jnp.zeros((half_t, half_t), F32)
    negb = jnp.full((half_t, half_t), NEG, F32)

    sel_state = (m_scr.at[0], acc_scr.at[0])
    win_state = (m_scr.at[1], acc_scr.at[1])
    _attn_init(*sel_state)
    _attn_init(*win_state)

    def scores(k_rows, rs, masked):
        qs = [qa_scr[r] if masked else qt_ref[0, r, 0] for r in rs]
        return _dot(k_rows, jnp.concatenate(qs, axis=1))

    def update_all(st_all, vts, adds, state, rs):
        for u, r in enumerate(rs):
            _attn_update(st_all[:, u * TQ:(u + 1) * TQ], vts, adds[u], *(s.at[r] for s in state))

    def far(c, _):
        k_rows = ks_ref[0, 0, _tile_ds(c, ft * TQ), :]
        vts = tuple(vst_ref[0, 0, ft * c + j] for j in range(ft))
        for g0 in range(0, heads, nh):
            rs = list(range(g0, g0 + nh))
            update_all(scores(k_rows, rs, True), vts, [None] * nh, sel_state, rs)
        return 0

    lax.fori_loop(0, n_far, far, 0)

    def head_group(gi, _):
        rs = [gi * nh + u for u in range(nh)]

        def near(n_s, n_w):
            t0s = [t0_ref[r] for r in rs]
            t1s = [t1_ref[r] for r in rs]
            prev = [jnp.concatenate([jnp.concatenate([zb, zb], axis=1),
                                     jnp.concatenate([t1, zb], axis=1)], axis=0) for t1 in t1s]
            diag = [jnp.concatenate([jnp.concatenate([t0, t1], axis=1),
                                     jnp.concatenate([negb, t0], axis=1)], axis=0) for t0, t1 in zip(t0s, t1s)]
            w0 = jnp.where(kj > qi, 0.0, NEG)
            st_s = scores(ks_ref[0, 0, pl.ds(pl.multiple_of((i - n_s + 1) * TQ, TQ), n_s * TQ), :], rs, True)
            st_w = scores(kw_ref[0, 0, pl.ds(pl.multiple_of((i - n_w + 1) * TQ, TQ), n_w * TQ), :], rs, False)
            adds_s, adds_w = [], []
            for u in range(nh):
                parts_s = [None] * n_s
                parts_s[-1] = diag[u]
                parts_w = [diag[u]]
                if n_s >= 2:
                    parts_s[-2] = prev[u]
                if n_w >= 2:
                    parts_w.insert(0, prev[u])
                if n_w >= 3:
                    parts_w.insert(0, w0)
                adds_s.append(parts_s)
                adds_w.append(parts_w)
            update_all(st_s, tuple(vst_ref[0, 0, i - n_s + 1 + j] for j in range(n_s)), adds_s, sel_state, rs)
            update_all(st_w, tuple(vwt_ref[0, 0, i - n_w + 1 + j] for j in range(n_w)), adds_w, win_state, rs)

        pl.when(i == 0)(lambda: near(1, 1))
        pl.when(i == 1)(lambda: near(2, 2))
        for k in range(ft):
            pl.when((i >= 2) & ((i - 1) % ft == k))(functools.partial(near, 2 + k, 3))

        for u, r in enumerate(rs):
            ot = (oc_scr[r] + _attn_result(acc_scr.at[0, r]) * as_ref[r, 0].astype(F32)
                  + _attn_result(acc_scr.at[1, r]) * aw_ref[r, 0].astype(F32))
            o_ref[r] = ot.T.astype(o_ref.dtype)
        return 0

    lax.fori_loop(0, heads // nh, head_group, 0)


def _t5_bucket(dist):
    n = jnp.maximum(dist, 0)
    max_exact = REL_BUCKETS // 2
    scaled = (jnp.log(jnp.maximum(n, max_exact).astype(F32) / max_exact)
              / math.log(REL_MAX_DISTANCE / max_exact))
    large = jnp.minimum(max_exact + (scaled * (REL_BUCKETS - max_exact)).astype(jnp.int32), REL_BUCKETS - 1)
    return jnp.where(n < max_exact, n, large)


def _bias_table(rel_bias, dist, shift):
    onehot = (_t5_bucket(dist)[None] == jnp.arange(REL_BUCKETS).reshape((-1,) + (1,) * dist.ndim)).astype(F32)
    vals = jnp.tensordot(rel_bias.T, onehot, axes=1, precision=lax.Precision.HIGHEST)
    return jnp.where(dist >= 0, (vals - shift) * LOG2E, NEG).astype(F32)


def nsa_attention(qt, kc, vct, ks, vst, kw, vwt, gates, rel_bias):
    b, h, nq = qt.shape[:3]
    s = nq * TQ
    g = kc.shape[1]
    heads = h // g
    n_cmp_pad = kc.shape[2]
    n_sel = s // SEL_BLOCK
    k_sel = min(SEL_TOPK, n_sel)
    half_t = TQ // 2
    nh = NSA_HEADS_PER_ITER
    nc = min(NSA_CMP_HEADS_PER_ITER, heads)
    assert heads % nh == 0 and heads % nc == 0
    n_blk = DKP - NSA_QK_DIM
    assert s % (2 * TQ) == 0 and WINDOW == 2 * TQ and n_sel <= n_blk and TQ % SEL_BLOCK == 0
    assert half_t >= REL_MAX_DISTANCE
    assert n_cmp_pad == s // CMP_STRIDE and n_cmp_pad % V7X_LANES == 0

    far = rel_bias[REL_BUCKETS - 1][:, None, None]
    per_tile = TQ // CMP_STRIDE
    band = 2 * per_tile
    assert CMP_STRIDE * (per_tile + 1) - (CMP_BLOCK - 1) >= REL_MAX_DISTANCE and n_cmp_pad % per_tile == 0
    a_rel = jnp.arange(band) - per_tile
    cb = _bias_table(rel_bias, jnp.arange(TQ)[None, :] - CMP_STRIDE * a_rel[:, None] - (CMP_BLOCK - 1), far)
    ij = jnp.arange(half_t)
    t0 = _bias_table(rel_bias, ij[None, :] - ij[:, None], far)
    t1 = _bias_table(rel_bias, half_t + ij[None, :] - ij[:, None], far)

    n_cmp = (s - CMP_BLOCK) // CMP_STRIDE + 1
    cmp_start = CMP_STRIDE * np.arange(n_cmp_pad)
    sel_start = SEL_BLOCK * np.arange(n_blk)
    ovl = ((cmp_start[None, :] < (sel_start + SEL_BLOCK)[:, None])
           & ((cmp_start + CMP_BLOCK)[None, :] > sel_start[:, None])
           & (np.arange(n_cmp_pad) < n_cmp)[None, :] & (np.arange(n_blk) < n_sel)[:, None])
    ovl = jnp.asarray(ovl, BF16)

    once = pl.Buffered(1)
    rows_spec = lambda n, d: pl.BlockSpec((1, 1, n, d), lambda bi, gi, i: (bi, gi, 0, 0), pipeline_mode=once)
    vt_spec = pl.BlockSpec((1, 1, nq, DV, TQ), lambda bi, gi, i: (bi, gi, 0, 0, 0), pipeline_mode=once)
    gate_spec = lambda br: pl.BlockSpec((heads, 1, DV, TQ),
                                        lambda bi, gi, i, br=br: (br * g + gi, bi * nq + i, 0, 0))
    tbl_spec = lambda shape: pl.BlockSpec((heads,) + shape, lambda bi, gi, i: (gi, 0, 0), pipeline_mode=once)
    kern = functools.partial(_nsa_attn_kernel, heads=heads, n_cmp_pad=n_cmp_pad, n_sel=n_sel, k_sel=k_sel)
    return pl.pallas_call(
        kern,
        grid=(b, g, nq),
        in_specs=[
            pl.BlockSpec((1, heads, 1, DKP, TQ), lambda bi, gi, i: (bi, gi, i, 0, 0)),
            rows_spec(n_cmp_pad, DKP), rows_spec(DV, n_cmp_pad),
            rows_spec(s, DKP), vt_spec, rows_spec(s, DKP), vt_spec,
            gate_spec(0), gate_spec(1), gate_spec(2),
            tbl_spec((band, TQ)), tbl_spec((half_t, half_t)), tbl_spec((half_t, half_t)),
            pl.BlockSpec((n_blk, n_cmp_pad), lambda bi, gi, i: (0, 0)),
        ],
        out_specs=pl.BlockSpec((heads, TQ, DV), lambda bi, gi, i: (gi, bi * nq + i, 0)),
        out_shape=jax.ShapeDtypeStruct((h, b * s, DV), BF16),
        scratch_shapes=[
            pltpu.VMEM((2, heads, 1, TQ), F32), pltpu.VMEM((2, heads, DV + SUM_ROWS, TQ), F32),
            pltpu.VMEM((heads, DV, TQ), F32), pltpu.VMEM((heads, DKP, TQ), BF16),
            pltpu.VMEM((nc, TQ // CMP_STRIDE + n_cmp_pad, TQ), F32),
        ],
        compiler_params=_cparams(("parallel", "parallel", "arbitrary")),
        name="nsa_attention",
    )(qt, kc, vct, ks, vst, kw, vwt, gates, gates, gates, cb, t0, t1, ovl)


def _mla_attn_kernel(qt_ref, k_ref, vt_ref, z_ref, o_ref, m_scr, acc_scr, *, heads):
    i = pl.program_id(2)
    kj = lax.broadcasted_iota(jnp.int32, (TQ, TQ), 0)
    qi = lax.broadcasted_iota(jnp.int32, (TQ, TQ), 1)
    causal = jnp.where(kj <= qi, 0.0, NEG)

    ft = FAR_TILES

    def block(first, n_tiles, add):
        rows = pl.ds(pl.multiple_of(first * TQ, TQ), n_tiles * TQ)
        for g0 in range(0, heads, MLA_HEADS_PER_ITER):
            rs = range(g0, min(g0 + MLA_HEADS_PER_ITER, heads))
            sts = [_dot(k_ref[0, r, rows, :], qt_ref[0, r, 0]) for r in rs]
            for st, r in zip(sts, rs):
                vts = tuple(vt_ref[0, r, first + j] for j in range(n_tiles))
                _attn_update(st, vts, add, m_scr.at[r], acc_scr.at[r])

    _attn_init(m_scr, acc_scr)

    def far(c, _):
        block(ft * c, ft, None)
        return 0

    lax.fori_loop(0, jnp.maximum(i - 1, 0) // ft, far, 0)

    def near(n_tiles):
        block(i - n_tiles + 1, n_tiles, [None] * (n_tiles - 1) + [causal])

    pl.when(i == 0)(functools.partial(near, 1))
    for k in range(ft):
        pl.when((i >= 1) & ((i - 1) % ft == k))(functools.partial(near, 2 + k))
    for r in range(heads):
        ot = _attn_result(acc_scr.at[r]) * z_ref[r, 0].astype(F32)
        o_ref[r] = ot.T.astype(o_ref.dtype)


def mla_attention(qt, k, vt, zg):
    b, h, nq = qt.shape[:3]
    s = nq * TQ
    hb = min(MLA_HEADS_PER_STEP, h)
    assert s % (2 * TQ) == 0 and h % hb == 0
    return pl.pallas_call(
        functools.partial(_mla_attn_kernel, heads=hb),
        grid=(b, h // hb, nq),
        in_specs=[
            pl.BlockSpec((1, hb, 1, DKP, TQ), lambda bi, hi, i: (bi, hi, i, 0, 0)),
            pl.BlockSpec((1, hb, s, DKP), lambda bi, hi, i: (bi, hi, 0, 0), pipeline_mode=pl.Buffered(1)),
            pl.BlockSpec((1, hb, nq, DV, TQ), lambda bi, hi, i: (bi, hi, 0, 0, 0), pipeline_mode=pl.Buffered(1)),
            pl.BlockSpec((hb, 1, DV, TQ), lambda bi, hi, i: (hi, bi * nq + i, 0, 0)),
        ],
        out_specs=pl.BlockSpec((hb, TQ, DV), lambda bi, hi, i: (hi, bi * nq + i, 0)),
        out_shape=jax.ShapeDtypeStruct((h, b * s, DV), BF16),
        scratch_shapes=[pltpu.VMEM((hb, 1, TQ), F32), pltpu.VMEM((hb, DV + SUM_ROWS, TQ), F32)],
        compiler_params=_cparams(("parallel", "parallel", "arbitrary")),
        name="mla_attention",
    )(qt, k, vt, zg)


def _pad_heads(w, n_heads, d):
    k = w.shape[0]
    return jnp.pad(w.reshape(k, n_heads, d), ((0, 0), (0, 0), (0, DKP - d))).reshape(k, n_heads * DKP).astype(BF16)


def _head_rows_out(b, s, h, tm, hb, width, dtype):
    nt = s // tm
    spec = pl.BlockSpec((1, hb, tm, width), lambda i, j: (i // nt, j, i % nt, 0))
    return jax.ShapeDtypeStruct((b, h, s, width), dtype), spec


def _head_tiles_out(b, s, h, tm, hb, width, dtype):
    nt = s // tm
    spec = pl.BlockSpec((1, hb, tm // TQ, width, TQ), lambda i, j: (i // nt, j, i % nt, 0, 0))
    return jax.ShapeDtypeStruct((b, h, s // TQ, width, TQ), dtype), spec


def nsa_layer(x2, b, s, norm_w, rel_bias, w_in, q_norm, k_norm, pe_k, w1_k, w2_k, pe_v, w1_v, w2_v, w_out):
    t, d_model = x2.shape
    h, g, dk, dv = NSA_HEADS, NSA_KV_GROUPS, NSA_QK_DIM, NSA_V_DIM
    qw, kvw, zw = h * dk, g * (dk + dv), h * dv
    tm = _pick(s, 1024, TQ)
    hn = rmsnorm_rows(x2, norm_w, d_model, BF16, "nsa_prenorm")

    w_all = w_in.astype(BF16)
    hb = _pick(h, 4)
    q_shape, q_spec = _head_tiles_out(b, s, h, tm, hb, DKP, BF16)
    qt = matmul(hn, w_all, functools.partial(_ep_q_heads, hb=hb, n_true=dk, scale=LOG2E * dk ** -0.5, rope=False),
                tm=tm, tn=hb * dk, out_shapes=q_shape, out_specs=q_spec, cols=(0, qw),
                extra=(jnp.broadcast_to(q_norm[:, None], (dk, TQ)),),
                extra_specs=(pl.BlockSpec((dk, TQ), lambda i, j: (0, 0)),), slabs=True, name="nsa_q_proj")

    def kv_proj(br, normed, block_onehot=False):
        w_br = w_all[:, qw + br * kvw: qw + (br + 1) * kvw]
        w_kv = jnp.concatenate([_pad_heads(w_br[:, :g * dk], g, dk), w_br[:, g * dk:]], axis=1)
        k_out = _head_rows_out(b, s, g, tm, g, DKP, BF16 if normed else F32)
        v_out = _head_tiles_out(b, s, g, tm, g, dv, BF16) if normed else _head_rows_out(b, s, g, tm, g, dv, F32)
        gain = jnp.pad(k_norm[br], (0, DKP - dk)).reshape(1, DKP)
        ep = functools.partial(_ep_nsa_kv, groups=g, n_true=dk, normed=normed,
                               seq_tiles=s // tm if block_onehot else 0)
        return matmul(hn, w_kv, ep,
                      tm=tm, tn=g * (DKP + dv), out_shapes=(k_out[0], v_out[0]), out_specs=(k_out[1], v_out[1]),
                      extra=(gain,), extra_specs=(pl.BlockSpec((1, DKP), lambda i, j: (0, 0)),),
                      slabs=True, name="nsa_kv_proj_%d" % br)

    kc_raw, vc_raw = kv_proj(0, False)
    ks, vst = kv_proj(1, True, block_onehot=True)
    kw, vwt = kv_proj(2, True)
    kc = compress(kc_raw, pe_k, w1_k, w2_k, k_norm[0], dk, "nsa_compress_k")
    vc = compress(vc_raw, pe_v, w1_v, w2_v, jnp.ones((dv,), F32), 0, "nsa_compress_v")
    vct = jnp.swapaxes(vc, -1, -2)

    z0 = qw + 3 * kvw
    w_g = w_all[:, z0 + 3 * zw:]
    sg = matmul(hn, w_g, _ep_sigmoid, tm=tm, tn=3 * h,
                out_shapes=jax.ShapeDtypeStruct((t, 3 * h), F32),
                out_specs=pl.BlockSpec((tm, 3 * h), lambda i, j: (i, j)), name="nsa_gate_proj")
    hbz = max(c for c in range(1, 9) if (3 * h) % c == 0 and z0 % (c * dv) == 0)
    n_zt = 3 * h // hbz
    sub = 8
    sg_tiles = jnp.pad(sg.reshape(t // TQ, TQ, n_zt, hbz).transpose(2, 0, 3, 1),
                       ((0, 0), (0, 0), (0, sub - hbz), (0, 0)))
    gates = matmul(hn, w_all, functools.partial(_ep_gate_heads, hb=hbz, gated=True), tm=tm, tn=hbz * dv,
                   out_shapes=jax.ShapeDtypeStruct((3 * h, t // TQ, dv, TQ), BF16),
                   out_specs=pl.BlockSpec((hbz, tm // TQ, dv, TQ), lambda i, j: (j, i, 0, 0)),
                   cols=(z0, 3 * zw), extra=(sg_tiles,),
                   extra_specs=(pl.BlockSpec((1, tm // TQ, sub, TQ), lambda i, j: (j, i, 0, 0)),),
                   slabs=True, name="nsa_z_proj")

    o = nsa_attention(qt, kc, vct, ks, vst, kw, vwt, gates, rel_bias)
    tmo = _pick(t, 512, 8)
    tno = _pick(d_model, 512, V7X_LANES)
    return matmul(o, w_out.astype(BF16), _ep_residual, tm=tmo, tn=tno,
                  out_shapes=jax.ShapeDtypeStruct((t, d_model), F32),
                  out_specs=pl.BlockSpec((tmo, tno), lambda i, j: (i, j)),
                  extra=(x2,), extra_specs=(pl.BlockSpec((tmo, tno), lambda i, j: (i, j)),), name="nsa_out_proj")


def mla_layer(x2, b, s, positions, norm_w, w_in, q_a_norm, w_q_b, kv_a_norm, w_kv_b, q_norm, k_norm, w_out):
    t, d_model = x2.shape
    h = MLA_HEADS
    dqk = MLA_NOPE_DIM + MLA_ROPE_DIM
    tm = _pick(s, 1024, TQ)
    hn = rmsnorm_rows(x2, norm_w, d_model, BF16, "mla_prenorm")

    half = MLA_ROPE_DIM // 2
    inv_freq = ROPE_BASE ** (-jnp.arange(half, dtype=F32) / half)
    ang = positions.astype(F32).reshape(t, 1) * inv_freq[None, :]
    zeros = jnp.zeros((t, V7X_LANES - MLA_ROPE_DIM), F32)
    cos2 = jnp.concatenate([jnp.cos(ang), jnp.cos(ang), zeros], axis=1)
    sin2 = jnp.concatenate([-jnp.sin(ang), jnp.sin(ang), zeros], axis=1)
    rope_spec = pl.BlockSpec((tm, V7X_LANES), lambda i, j: (i, 0))
    cos2_t = jnp.swapaxes(cos2[:, :MLA_ROPE_DIM].reshape(t // TQ, TQ, MLA_ROPE_DIM), 1, 2)
    sin2_t = jnp.swapaxes(sin2[:, :MLA_ROPE_DIM].reshape(t // TQ, TQ, MLA_ROPE_DIM), 1, 2)
    rope_t_spec = pl.BlockSpec((tm // TQ, MLA_ROPE_DIM, TQ), lambda i, j: (i, 0, 0))
    row_spec = lambda w: pl.BlockSpec((1, w), lambda i, j: (0, 0))

    c0, c1, c2 = MLA_Q_LORA, MLA_Q_LORA + MLA_KV_LORA, MLA_Q_LORA + MLA_KV_LORA + MLA_ROPE_DIM
    w_lat = jnp.concatenate([w_in[:, :c1], jnp.pad(w_in[:, c1:c2], ((0, 0), (0, DKP - MLA_ROPE_DIM)))],
                            axis=1).astype(BF16)
    lat_w = c1 + DKP
    tml = _pick(s, 512, 8)
    row_out = lambda w, dt: (jax.ShapeDtypeStruct((t, w), dt), pl.BlockSpec((tml, w), lambda i, j: (i, 0)))
    lat_outs = (row_out(c0, BF16), row_out(c1 - c0, BF16), row_out(V7X_LANES, F32))
    cq, ckv, kpe = matmul(hn, w_lat, functools.partial(_ep_mla_latent, c0=c0, c1=c1), tm=tml, tn=lat_w,
                          out_shapes=tuple(o[0] for o in lat_outs), out_specs=tuple(o[1] for o in lat_outs),
                          extra=(q_a_norm.reshape(1, c0), kv_a_norm.reshape(1, c1 - c0)),
                          extra_specs=(row_spec(c0), row_spec(c1 - c0)), w_once=True, name="mla_latent_proj")

    hbz = _pick(h, 8)
    zg = matmul(hn, w_in[:, c2:].astype(BF16), functools.partial(_ep_gate_heads, hb=hbz, gated=False),
                tm=tm, tn=hbz * DV, out_shapes=jax.ShapeDtypeStruct((h, t // TQ, DV, TQ), BF16),
                out_specs=pl.BlockSpec((hbz, tm // TQ, DV, TQ), lambda i, j: (j, i, 0, 0)),
                slabs=True, name="mla_z_proj")

    hb = _pick(h, 4)
    q_shape, q_spec = _head_tiles_out(b, s, h, tm, hb, DKP, BF16)
    qt = matmul(cq, w_q_b.astype(BF16),
                functools.partial(_ep_q_heads, hb=hb, n_true=dqk, scale=LOG2E * dqk ** -0.5, rope=True),
                tm=tm, tn=hb * dqk, out_shapes=q_shape, out_specs=q_spec,
                extra=(jnp.broadcast_to(q_norm[:, None], (dqk, TQ)), cos2_t, sin2_t),
                extra_specs=(pl.BlockSpec((dqk, TQ), lambda i, j: (0, 0)), rope_t_spec, rope_t_spec),
                slabs=True, name="mla_q_proj")

    k_shape, k_spec = _head_rows_out(b, s, h, tm, hb, DKP, BF16)
    v_shape, v_spec = _head_tiles_out(b, s, h, tm, hb, DV, BF16)
    gain_n = k_norm[:MLA_NOPE_DIM].reshape(1, MLA_NOPE_DIM)
    gain_r = jnp.pad(k_norm[MLA_NOPE_DIM:], (0, V7X_LANES - MLA_ROPE_DIM)).reshape(1, V7X_LANES)
    k, vt = matmul(ckv, w_kv_b.astype(BF16), functools.partial(_ep_mla_kv, hb=hb),
                   tm=tm, tn=hb * DKP, out_shapes=(k_shape, v_shape), out_specs=(k_spec, v_spec),
                   extra=(kpe, gain_n, gain_r, cos2, sin2),
                   extra_specs=(rope_spec, row_spec(MLA_NOPE_DIM), row_spec(V7X_LANES), rope_spec, rope_spec),
                   slabs=True, name="mla_kv_proj")

    o = mla_attention(qt, k, vt, zg)
    tmo = _pick(t, 512, 8)
    tno = _pick(d_model, 512, V7X_LANES)
    return matmul(o, w_out.astype(BF16), _ep_residual, tm=tmo, tn=tno,
                  out_shapes=jax.ShapeDtypeStruct((t, d_model), F32),
                  out_specs=pl.BlockSpec((tmo, tno), lambda i, j: (i, j)),
                  extra=(x2,), extra_specs=(pl.BlockSpec((tmo, tno), lambda i, j: (i, j)),), name="mla_out_proj")


def kernel(x, positions, norm_w, rel_bias, nsa_w_in, nsa_q_norm, nsa_k_norm, nsa_cmp_pe_k, nsa_cmp_w1_k,
           nsa_cmp_w2_k, nsa_cmp_pe_v, nsa_cmp_w1_v, nsa_cmp_w2_v, nsa_w_out, mla_w_in, mla_q_a_norm,
           mla_w_q_b, mla_kv_a_norm, mla_w_kv_b, mla_q_norm, mla_k_norm, mla_w_out):
    b, s, d_model = x.shape
    x2 = x.reshape(b * s, d_model)
    depth = norm_w.shape[0]
    for layer in range(depth):
        j = layer // 2
        if layer % 2 == 0:
            x2 = nsa_layer(x2, b, s, norm_w[layer], rel_bias, nsa_w_in[j], nsa_q_norm[j], nsa_k_norm[j],
                           nsa_cmp_pe_k[j], nsa_cmp_w1_k[j], nsa_cmp_w2_k[j], nsa_cmp_pe_v[j], nsa_cmp_w1_v[j],
                           nsa_cmp_w2_v[j], nsa_w_out[j])
        else:
            x2 = mla_layer(x2, b, s, positions, norm_w[layer], mla_w_in[j], mla_q_a_norm[j], mla_w_q_b[j],
                           mla_kv_a_norm[j], mla_w_kv_b[j], mla_q_norm[j], mla_k_norm[j], mla_w_out[j])
    return x2.reshape(b, s, d_model)
```

```python
import functools
import math

import numpy as np
import jax
import jax.numpy as jnp
from jax import lax
from jax.experimental import pallas as pl
from jax.experimental.pallas import tpu as pltpu

F32 = jnp.float32
BF16 = jnp.bfloat16

EPS = 1e-6
REL_BUCKETS = 32
REL_MAX_DISTANCE = 128
NSA_HEADS = 64
NSA_KV_GROUPS = 4
NSA_QK_DIM = 192
NSA_V_DIM = 128
CMP_BLOCK = 32
CMP_STRIDE = 16
SEL_BLOCK = 64
SEL_TOPK = 16
WINDOW = 512
MLA_HEADS = 64
MLA_Q_LORA = 1536
MLA_KV_LORA = 512
MLA_NOPE_DIM = 128
MLA_ROPE_DIM = 64
MLA_V_DIM = 128
ROPE_BASE = 10000.0

V7X_LANES = 128
V7X_VMEM_BYTES = 64 * 1024 * 1024
VMEM_LIMIT_BYTES = V7X_VMEM_BYTES - 8 * 1024 * 1024

DKP = 2 * V7X_LANES
DV = V7X_LANES
TQ = 256
NEG = -1e30
M_INIT = -1e20
SUM_ROWS = 16
LOG2E = math.log2(math.e)
MLA_HEADS_PER_STEP = 4
NSA_HEADS_PER_ITER = 4
NSA_CMP_HEADS_PER_ITER = 16
FAR_TILES = 4


def _cparams(sem):
    return pltpu.CompilerParams(dimension_semantics=sem, vmem_limit_bytes=VMEM_LIMIT_BYTES)


def _dot(a, b):
    return jnp.dot(a, b, preferred_element_type=F32)


def _sigmoid(x):
    return 1.0 / (1.0 + jnp.exp(-x))


def _silu(x):
    return x * _sigmoid(x)


def _pick(n, target, unit=1):
    best = unit
    c = unit
    while c <= min(n, max(target, unit)):
        if n % c == 0:
            best = c
        c += unit
    return best


def _rmsnorm_kernel(x_ref, g_ref, o_ref, *, inv_n):
    x = x_ref[...].astype(F32)
    ms = jnp.sum(x * x, axis=-1, keepdims=True) * inv_n
    o_ref[...] = (x * lax.rsqrt(ms + EPS) * g_ref[...]).astype(o_ref.dtype)


def rmsnorm_rows(x, gain, n_true, out_dtype, name):
    m, d = x.shape
    tm = _pick(m, max(8, (2 * 1024 * 1024) // (4 * d)), 8)
    return pl.pallas_call(
        functools.partial(_rmsnorm_kernel, inv_n=1.0 / n_true),
        grid=(m // tm,),
        in_specs=[pl.BlockSpec((tm, d), lambda i: (i, 0)), pl.BlockSpec((1, d), lambda i: (0, 0))],
        out_specs=pl.BlockSpec((tm, d), lambda i: (i, 0)),
        out_shape=jax.ShapeDtypeStruct((m, d), out_dtype),
        compiler_params=_cparams(("parallel",)),
        name=name,
    )(x, gain.reshape(1, d).astype(F32))


def _matmul_kernel(*refs, epilogue, n_in, slabs):
    a_ref, w_ref = refs[0], refs[1]
    extra, outs = refs[2:n_in], refs[n_in:]
    if len(a_ref.shape) == 3:
        a = jnp.concatenate([a_ref[h] for h in range(a_ref.shape[0])], axis=1)
    else:
        a = a_ref[...]
    w = w_ref[...]
    if not slabs:
        epilogue(_dot(a, w), extra, outs)
        return
    n = a.shape[0] // TQ
    acc = _dot(a[:TQ], w)
    for u in range(n):
        nxt = _dot(a[(u + 1) * TQ:(u + 2) * TQ], w) if u + 1 < n else None
        epilogue(acc, u, extra, outs)
        acc = nxt


def matmul(a, w, epilogue, *, tm, tn, out_shapes, out_specs, extra=(), extra_specs=(), slabs=False,
           w_once=False, cols=None, name):
    col0, n = cols if cols else (0, w.shape[1])
    assert col0 % tn == 0
    jb = col0 // tn
    if a.ndim == 3:
        m, k = a.shape[1], a.shape[0] * a.shape[2]
        a_spec = pl.BlockSpec((a.shape[0], tm, a.shape[2]), lambda i, j: (0, i, 0))
    else:
        m, k = a.shape
        a_spec = pl.BlockSpec((tm, k), lambda i, j: (i, 0))
    assert m % tm == 0 and n % tn == 0 and k == w.shape[0], (m, tm, n, tn, k)
    assert not slabs or tm % TQ == 0
    assert not w_once or tn == n
    w_spec = pl.BlockSpec((k, tn), lambda i, j: (0, jb + j), pipeline_mode=pl.Buffered(1) if w_once else None)
    in_specs = [a_spec, w_spec]
    in_specs += list(extra_specs)
    return pl.pallas_call(
        functools.partial(_matmul_kernel, epilogue=epilogue, n_in=2 + len(extra), slabs=slabs),
        grid=(m // tm, n // tn),
        in_specs=in_specs,
        out_specs=out_specs,
        out_shape=out_shapes,
        compiler_params=_cparams(("parallel", "arbitrary")),
        name=name,
    )(a, w, *extra)


def _ep_store(acc, extra, outs):
    outs[0][...] = acc.astype(outs[0].dtype)


def _ep_sigmoid(acc, extra, outs):
    outs[0][...] = _sigmoid(acc)


def _ep_residual(acc, extra, outs):
    outs[0][...] = extra[0][...] + acc


def _ep_mla_latent(acc, extra, outs, *, c0, c1):
    def normed(x, gain):
        ms = jnp.mean(x * x, axis=-1, keepdims=True)
        return x * lax.rsqrt(ms + EPS) * gain

    outs[0][...] = normed(acc[:, :c0], extra[0][...]).astype(outs[0].dtype)
    outs[1][...] = normed(acc[:, c0:c1], extra[1][...]).astype(outs[1].dtype)
    outs[2][...] = acc[:, c1:c1 + V7X_LANES]


def _ep_nsa_kv(acc, u, extra, outs, *, groups, n_true, normed, seq_tiles=0):
    rows = pl.ds(u * TQ, TQ)
    gain = extra[0][...]
    if seq_tiles:
        tm = outs[0].shape[2]
        pos = (pl.program_id(0) % seq_tiles) * tm + u * TQ + lax.broadcasted_iota(jnp.int32, (TQ, 1), 0)
        slot = n_true + jnp.right_shift(pos, int(math.log2(SEL_BLOCK)))
        block_lane = lax.broadcasted_iota(jnp.int32, (TQ, DKP), 1) == slot
    for gi in range(groups):
        k = acc[:, gi * DKP:(gi + 1) * DKP]
        v = acc[:, groups * DKP + gi * DV:groups * DKP + (gi + 1) * DV]
        if normed:
            ms = jnp.sum(k * k, axis=-1, keepdims=True) * (1.0 / n_true)
            kn = k * lax.rsqrt(ms + EPS) * gain
            if seq_tiles:
                kn = jnp.where(block_lane, 1.0, kn)
            outs[0][0, gi, rows, :] = kn.astype(outs[0].dtype)
            outs[1][0, gi, u] = v.T.astype(outs[1].dtype)
        else:
            outs[0][0, gi, rows, :] = k
            outs[1][0, gi, rows, :] = v


def _rope_lanes(x, cos2, sin2):
    half = MLA_ROPE_DIM // 2
    lane = lax.broadcasted_iota(jnp.int32, x.shape, 1)
    up = pltpu.roll(x, V7X_LANES - half, axis=1)
    down = pltpu.roll(x, half, axis=1)
    rot = jnp.where(lane < half, up, down)
    return x * cos2 + rot * sin2


def _rope_rows(x, cos2, sin2):
    half = MLA_ROPE_DIM // 2
    rot = jnp.concatenate([x[half:], x[:half]], axis=0)
    return x * cos2 + rot * sin2


def _ep_q_heads(acc, u, extra, outs, *, hb, n_true, scale, rope):
    acc_t = acc.T
    gain = extra[0][...]
    zeros = jnp.zeros((DKP - n_true, TQ), F32)
    for r in range(hb):
        y = acc_t[r * n_true:(r + 1) * n_true]
        ms = jnp.sum(y * y, axis=0, keepdims=True) * (1.0 / n_true)
        yn = y * lax.rsqrt(ms + EPS) * gain
        if rope:
            nope = n_true - MLA_ROPE_DIM
            yn = jnp.concatenate([yn[:nope], _rope_rows(yn[nope:], extra[1][u], extra[2][u])], axis=0)
        outs[0][0, r, u] = (jnp.concatenate([yn, zeros], axis=0) * scale).astype(outs[0].dtype)


def _ep_gate_heads(acc, u, extra, outs, *, hb, gated):
    acc_t = acc.T
    for r in range(hb):
        a = _silu(acc_t[r * DV:(r + 1) * DV])
        if gated:
            a = a * extra[0][0, u, r:r + 1, :]
        outs[0][r, u] = a.astype(outs[0].dtype)


def _ep_mla_kv(acc, u, extra, outs, *, hb):
    rows = pl.ds(u * TQ, TQ)
    kpe = extra[0][rows, :]
    gain_n = extra[1][...]
    gain_r = extra[2][...]
    ss_pe = jnp.sum(kpe * kpe, axis=-1, keepdims=True)
    kr = _rope_lanes(kpe * gain_r, extra[3][rows, :], extra[4][rows, :])
    inv_n = 1.0 / (MLA_NOPE_DIM + MLA_ROPE_DIM)
    for r in range(hb):
        kn = acc[:, r * DKP:r * DKP + V7X_LANES]
        v = acc[:, r * DKP + V7X_LANES:(r + 1) * DKP]
        rs = lax.rsqrt((jnp.sum(kn * kn, axis=-1, keepdims=True) + ss_pe) * inv_n + EPS)
        outs[0][0, r, rows, :] = jnp.concatenate([kn * rs * gain_n, kr * rs], axis=1).astype(outs[0].dtype)
        outs[1][0, r, u] = v.T.astype(outs[1].dtype)


def _compress_kernel(xa_ref, xb_ref, pea_ref, peb_ref, w1a_ref, w1b_ref, w2_ref, g_ref, o_ref, *, norm_n):
    xa = (xa_ref[0, 0] + pea_ref[...]).astype(BF16)
    xb = (xb_ref[0, 0] + peb_ref[...]).astype(BF16)
    pre = _dot(xa, w1a_ref[...]) + _dot(xb, w1b_ref[...])
    y = _dot(_silu(pre).astype(BF16), w2_ref[...])
    if norm_n:
        ms = jnp.sum(y * y, axis=-1, keepdims=True) * (1.0 / norm_n)
        y = y * lax.rsqrt(ms + EPS) * g_ref[...]
    o_ref[0, 0] = y.astype(o_ref.dtype)


def compress(x_raw, pe, w1, w2, gain, norm_n, name):
    b, g, s, dp = x_raw.shape
    d = pe.shape[1]
    nch = s // CMP_STRIDE
    half = CMP_STRIDE * dp
    chunks = x_raw.reshape(b, g, nch, half)
    xb = jnp.concatenate([chunks[:, :, 1:], jnp.zeros((b, g, 1, half), F32)], axis=2)
    pe_p = jnp.pad(pe, ((0, 0), (0, dp - d)))
    pea = pe_p[:CMP_STRIDE].reshape(1, half)
    peb = pe_p[CMP_STRIDE:].reshape(1, half)
    w1p = jnp.pad(w1.reshape(CMP_BLOCK, d, d), ((0, 0), (0, dp - d), (0, dp - d))).astype(BF16)
    w1a = w1p[:CMP_STRIDE].reshape(half, dp)
    w1b = w1p[CMP_STRIDE:].reshape(half, dp)
    w2p = jnp.pad(w2, ((0, dp - d), (0, dp - d))).astype(BF16)
    gp = jnp.pad(gain, (0, dp - d)).reshape(1, dp).astype(F32)
    full = lambda shape: pl.BlockSpec(shape, lambda bi, gi: (0,) * len(shape))
    return pl.pallas_call(
        functools.partial(_compress_kernel, norm_n=norm_n),
        grid=(b, g),
        in_specs=[
            pl.BlockSpec((1, 1, nch, half), lambda bi, gi: (bi, gi, 0, 0)),
            pl.BlockSpec((1, 1, nch, half), lambda bi, gi: (bi, gi, 0, 0)),
            full((1, half)), full((1, half)), full((half, dp)), full((half, dp)), full((dp, dp)), full((1, dp)),
        ],
        out_specs=pl.BlockSpec((1, 1, nch, dp), lambda bi, gi: (bi, gi, 0, 0)),
        out_shape=jax.ShapeDtypeStruct((b, g, nch, dp), BF16),
        compiler_params=_cparams(("parallel", "arbitrary")),
        name=name,
    )(chunks, xb, pea, peb, w1a, w1b, w2p, gp)


def _attn_init(m_ref, acc_ref):
    m_ref[...] = jnp.full(m_ref.shape, M_INIT, F32)
    acc_ref[...] = jnp.zeros(acc_ref.shape, F32)


def _attn_update(st, vts, add, m_ref, acc_ref):
    ones = jnp.ones((SUM_ROWS, TQ), BF16)
    for j, vt in enumerate(vts):
        sj = st[j * TQ:(j + 1) * TQ]
        if add is not None and add[j] is not None:
            sj = sj + add[j]
        m_prev = m_ref[...]
        m_parts, p_parts = [], []
        for h0 in range(0, TQ, V7X_LANES):
            sh = sj[:, h0:h0 + V7X_LANES]
            mh = jnp.maximum(m_prev[:, h0:h0 + V7X_LANES], jnp.max(sh, axis=0, keepdims=True))
            m_parts.append(mh)
            p_parts.append(jnp.exp2(sh - mh).astype(BF16))
        m_new = jnp.concatenate(m_parts, axis=1)
        alpha = jnp.exp2(m_prev - m_new)
        p = jnp.concatenate(p_parts, axis=1)
        acc_ref[...] = alpha * acc_ref[...] + _dot(jnp.concatenate([vt, ones], axis=0), p)
        m_ref[...] = m_new


def _attn_result(acc_ref):
    acc = acc_ref[...]
    return acc[:DV] / acc[DV:DV + 1]


def _tile_ds(idx, size):
    return pl.ds(pl.multiple_of(idx * size, size), size)


def _nsa_attn_kernel(qt_ref, kc_ref, vct_ref, ks_ref, vst_ref, kw_ref, vwt_ref, ac_ref, as_ref, aw_ref,
                     cb_ref, t0_ref, t1_ref, ovl_ref, o_ref,
                     m_scr, acc_scr, oc_scr, qa_scr, cmpb_scr, *, heads, n_cmp_pad, n_sel, k_sel):
    nh = NSA_HEADS_PER_ITER
    i = pl.program_id(2)
    tok = i * TQ + lax.broadcasted_iota(jnp.int32, (1, TQ), 1)
    per_tile = TQ // CMP_STRIDE
    band = cb_ref.shape[1]

    valid_c = (tok >= CMP_BLOCK - 1).astype(F32)
    nrow = lax.broadcasted_iota(jnp.int32, (per_tile + n_cmp_pad, TQ), 0)
    base = jnp.where(nrow >= (i + 2) * per_tile, NEG, 0.0)
    nc = cmpb_scr.shape[0]
    for u in range(nc):
        cmpb_scr[u] = base
    kc = kc_ref[0, 0]
    vct = vct_ref[0, 0]

    def cmp_group(gi, psum):
        rs = [gi * nc + u for u in range(nc)]
        for u, r in enumerate(rs):
            cmpb_scr[u, pl.ds(pl.multiple_of(i * per_tile, per_tile), band), :] = cb_ref[r]
        st_all = _dot(kc, jnp.concatenate([qt_ref[0, r, 0] for r in rs], axis=1))
        for u, r in enumerate(rs):
            st = st_all[:, u * TQ:(u + 1) * TQ] + cmpb_scr[u, per_tile:per_tile + n_cmp_pad, :]
            p = jnp.exp2(st - jnp.max(st, axis=0, keepdims=True))
            pn = p * (valid_c / jnp.sum(p, axis=0, keepdims=True))
            oc_scr[r] = _dot(vct, pn.astype(BF16)) * ac_ref[r, 0].astype(F32)
            psum = psum + pn
        return psum

    psum = lax.fori_loop(0, heads // nc, cmp_group, jnp.zeros((n_cmp_pad, TQ), F32))

    ovl = ovl_ref[...]
    p_hi = psum.astype(BF16)
    rem = psum - p_hi.astype(F32)
    p_mid = rem.astype(BF16)
    p_lo = (rem - p_mid.astype(F32)).astype(BF16)
    imp = _dot(ovl, p_hi) + _dot(ovl, p_mid) + _dot(ovl, p_lo)

    n_blk = DKP - NSA_QK_DIM
    blk = lax.broadcasted_iota(jnp.int32, (n_blk, TQ), 0)
    blk_f = blk.astype(F32)
    cur = jnp.right_shift(tok, int(math.log2(SEL_BLOCK)))
    forced = (blk == 0) | (blk == cur) | (blk == cur - 1)
    future = blk * SEL_BLOCK > tok
    val = jnp.where(forced, 1e30, jnp.where(future, -1.0, imp))
    val = jnp.where(blk < n_sel, val, -2.0)

    def pick(_, c):
        v, sel = c
        top = jnp.max(v, axis=0, keepdims=True)
        first = jnp.min(jnp.where(v == top, blk_f, float(n_blk)), axis=0, keepdims=True)
        hit = blk_f == first
        return jnp.where(hit, -3.0, v), jnp.where(hit, 1.0, sel)

    _, sel = lax.fori_loop(0, k_sel, pick, (val, jnp.zeros((n_blk, TQ), F32)), unroll=True)

    mask_rows = ((sel - 1.0) * (-NEG)).astype(BF16)
    for r in range(heads):
        qa_scr[r] = jnp.concatenate([qt_ref[0, r, 0, 0:NSA_QK_DIM, :], mask_rows], axis=0)

    ft = FAR_TILES
    n_far = jnp.maximum(i - 1, 0) // ft
    kj = lax.broadcasted_iota(jnp.int32, (TQ, TQ), 0)
    qi = lax.broadcasted_iota(jnp.int32, (TQ, TQ), 1)
    half_t = TQ // 2
    zb = jnp.zeros((half_t, half_t), F32)
    negb = jnp.full((half_t, half_t), NEG, F32)

    sel_state = (m_scr.at[0], acc_scr.at[0])
    win_state = (m_scr.at[1], acc_scr.at[1])
    _attn_init(*sel_state)
    _attn_init(*win_state)

    def scores(k_rows, rs, masked):
        qs = [qa_scr[r] if masked else qt_ref[0, r, 0] for r in rs]
        return _dot(k_rows, jnp.concatenate(qs, axis=1))

    def update_all(st_all, vts, adds, state, rs):
        for u, r in enumerate(rs):
            _attn_update(st_all[:, u * TQ:(u + 1) * TQ], vts, adds[u], *(s.at[r] for s in state))

    def far(c, _):
        k_rows = ks_ref[0, 0, _tile_ds(c, ft * TQ), :]
        vts = tuple(vst_ref[0, 0, ft * c + j] for j in range(ft))
        for g0 in range(0, heads, nh):
            rs = list(range(g0, g0 + nh))
            update_all(scores(k_rows, rs, True), vts, [None] * nh, sel_state, rs)
        return 0

    lax.fori_loop(0, n_far, far, 0)

    def head_group(gi, _):
        rs = [gi * nh + u for u in range(nh)]

        def near(n_s, n_w):
            t0s = [t0_ref[r] for r in rs]
            t1s = [t1_ref[r] for r in rs]
            prev = [jnp.concatenate([jnp.concatenate([zb, zb], axis=1),
                                     jnp.concatenate([t1, zb], axis=1)], axis=0) for t1 in t1s]
            diag = [jnp.concatenate([jnp.concatenate([t0, t1], axis=1),
                                     jnp.concatenate([negb, t0], axis=1)], axis=0) for t0, t1 in zip(t0s, t1s)]
            w0 = jnp.where(kj > qi, 0.0, NEG)
            st_s = scores(ks_ref[0, 0, pl.ds(pl.multiple_of((i - n_s + 1) * TQ, TQ), n_s * TQ), :], rs, True)
            st_w = scores(kw_ref[0, 0, pl.ds(pl.multiple_of((i - n_w + 1) * TQ, TQ), n_w * TQ), :], rs, False)
            adds_s, adds_w = [], []
            for u in range(nh):
                parts_s = [None] * n_s
                parts_s[-1] = diag[u]
                parts_w = [diag[u]]
                if n_s >= 2:
                    parts_s[-2] = prev[u]
                if n_w >= 2:
                    parts_w.insert(0, prev[u])
                if n_w >= 3:
                    parts_w.insert(0, w0)
                adds_s.append(parts_s)
                adds_w.append(parts_w)
            update_all(st_s, tuple(vst_ref[0, 0, i - n_s + 1 + j] for j in range(n_s)), adds_s, sel_state, rs)
            update_all(st_w, tuple(vwt_ref[0, 0, i - n_w + 1 + j] for j in range(n_w)), adds_w, win_state, rs)

        pl.when(i == 0)(lambda: near(1, 1))
        pl.when(i == 1)(lambda: near(2, 2))
        for k in range(ft):
            pl.when((i >= 2) & ((i - 1) % ft == k))(functools.partial(near, 2 + k, 3))

        for u, r in enumerate(rs):
            ot = (oc_scr[r] + _attn_result(acc_scr.at[0, r]) * as_ref[r, 0].astype(F32)
                  + _attn_result(acc_scr.at[1, r]) * aw_ref[r, 0].astype(F32))
            o_ref[r] = ot.T.astype(o_ref.dtype)
        return 0

    lax.fori_loop(0, heads // nh, head_group, 0)


def _t5_bucket(dist):
    n = jnp.maximum(dist, 0)
    max_exact = REL_BUCKETS // 2
    scaled = (jnp.log(jnp.maximum(n, max_exact).astype(F32) / max_exact)
              / math.log(REL_MAX_DISTANCE / max_exact))
    large = jnp.minimum(max_exact + (scaled * (REL_BUCKETS - max_exact)).astype(jnp.int32), REL_BUCKETS - 1)
    return jnp.where(n < max_exact, n, large)


def _bias_table(rel_bias, dist, shift):
    onehot = (_t5_bucket(dist)[None] == jnp.arange(REL_BUCKETS).reshape((-1,) + (1,) * dist.ndim)).astype(F32)
    vals = jnp.tensordot(rel_bias.T, onehot, axes=1, precision=lax.Precision.HIGHEST)
    return jnp.where(dist >= 0, (vals - shift) * LOG2E, NEG).astype(F32)


def nsa_attention(qt, kc, vct, ks, vst, kw, vwt, gates, rel_bias):
    b, h, nq = qt.shape[:3]
    s = nq * TQ
    g = kc.shape[1]
    heads = h // g
    n_cmp_pad = kc.shape[2]
    n_sel = s // SEL_BLOCK
    k_sel = min(SEL_TOPK, n_sel)
    half_t = TQ // 2
    nh = NSA_HEADS_PER_ITER
    nc = min(NSA_CMP_HEADS_PER_ITER, heads)
    assert heads % nh == 0 and heads % nc == 0
    n_blk = DKP - NSA_QK_DIM
    assert s % (2 * TQ) == 0 and WINDOW == 2 * TQ and n_sel <= n_blk and TQ % SEL_BLOCK == 0
    assert half_t >= REL_MAX_DISTANCE
    assert n_cmp_pad == s // CMP_STRIDE and n_cmp_pad % V7X_LANES == 0

    far = rel_bias[REL_BUCKETS - 1][:, None, None]
    per_tile = TQ // CMP_STRIDE
    band = 2 * per_tile
    assert CMP_STRIDE * (per_tile + 1) - (CMP_BLOCK - 1) >= REL_MAX_DISTANCE and n_cmp_pad % per_tile == 0
    a_rel = jnp.arange(band) - per_tile
    cb = _bias_table(rel_bias, jnp.arange(TQ)[None, :] - CMP_STRIDE * a_rel[:, None] - (CMP_BLOCK - 1), far)
    ij = jnp.arange(half_t)
    t0 = _bias_table(rel_bias, ij[None, :] - ij[:, None], far)
    t1 = _bias_table(rel_bias, half_t + ij[None, :] - ij[:, None], far)

    n_cmp = (s - CMP_BLOCK) // CMP_STRIDE + 1
    cmp_start = CMP_STRIDE * np.arange(n_cmp_pad)
    sel_start = SEL_BLOCK * np.arange(n_blk)
    ovl = ((cmp_start[None, :] < (sel_start + SEL_BLOCK)[:, None])
           & ((cmp_start + CMP_BLOCK)[None, :] > sel_start[:, None])
           & (np.arange(n_cmp_pad) < n_cmp)[None, :] & (np.arange(n_blk) < n_sel)[:, None])
    ovl = jnp.asarray(ovl, BF16)

    once = pl.Buffered(1)
    rows_spec = lambda n, d: pl.BlockSpec((1, 1, n, d), lambda bi, gi, i: (bi, gi, 0, 0), pipeline_mode=once)
    vt_spec = pl.BlockSpec((1, 1, nq, DV, TQ), lambda bi, gi, i: (bi, gi, 0, 0, 0), pipeline_mode=once)
    gate_spec = lambda br: pl.BlockSpec((heads, 1, DV, TQ),
                                        lambda bi, gi, i, br=br: (br * g + gi, bi * nq + i, 0, 0))
    tbl_spec = lambda shape: pl.BlockSpec((heads,) + shape, lambda bi, gi, i: (gi, 0, 0), pipeline_mode=once)
    kern = functools.partial(_nsa_attn_kernel, heads=heads, n_cmp_pad=n_cmp_pad, n_sel=n_sel, k_sel=k_sel)
    return pl.pallas_call(
        kern,
        grid=(b, g, nq),
        in_specs=[
            pl.BlockSpec((1, heads, 1, DKP, TQ), lambda bi, gi, i: (bi, gi, i, 0, 0)),
            rows_spec(n_cmp_pad, DKP), rows_spec(DV, n_cmp_pad),
            rows_spec(s, DKP), vt_spec, rows_spec(s, DKP), vt_spec,
            gate_spec(0), gate_spec(1), gate_spec(2),
            tbl_spec((band, TQ)), tbl_spec((half_t, half_t)), tbl_spec((half_t, half_t)),
            pl.BlockSpec((n_blk, n_cmp_pad), lambda bi, gi, i: (0, 0)),
        ],
        out_specs=pl.BlockSpec((heads, TQ, DV), lambda bi, gi, i: (gi, bi * nq + i, 0)),
        out_shape=jax.ShapeDtypeStruct((h, b * s, DV), BF16),
        scratch_shapes=[
            pltpu.VMEM((2, heads, 1, TQ), F32), pltpu.VMEM((2, heads, DV + SUM_ROWS, TQ), F32),
            pltpu.VMEM((heads, DV, TQ), F32), pltpu.VMEM((heads, DKP, TQ), BF16),
            pltpu.VMEM((nc, TQ // CMP_STRIDE + n_cmp_pad, TQ), F32),
        ],
        compiler_params=_cparams(("parallel", "parallel", "arbitrary")),
        name="nsa_attention",
    )(qt, kc, vct, ks, vst, kw, vwt, gates, gates, gates, cb, t0, t1, ovl)


def _mla_attn_kernel(qt_ref, k_ref, vt_ref, z_ref, o_ref, m_scr, acc_scr, *, heads):
    i = pl.program_id(2)
    kj = lax.broadcasted_iota(jnp.int32, (TQ, TQ), 0)
    qi = lax.broadcasted_iota(jnp.int32, (TQ, TQ), 1)
    causal = jnp.where(kj <= qi, 0.0, NEG)

    ft = FAR_TILES

    def block(first, n_tiles, add):
        rows = pl.ds(pl.multiple_of(first * TQ, TQ), n_tiles * TQ)
        sts = [_dot(k_ref[0, r, rows, :], qt_ref[0, r, 0]) for r in range(heads)]
        for r in range(heads):
            vts = tuple(vt_ref[0, r, first + j] for j in range(n_tiles))
            _attn_update(sts[r], vts, add, m_scr.at[r], acc_scr.at[r])

    _attn_init(m_scr, acc_scr)

    def far(c, _):
        block(ft * c, ft, None)
        return 0

    lax.fori_loop(0, jnp.maximum(i - 1, 0) // ft, far, 0)

    def near(n_tiles):
        block(i - n_tiles + 1, n_tiles, [None] * (n_tiles - 1) + [causal])

    pl.when(i == 0)(functools.partial(near, 1))
    for k in range(ft):
        pl.when((i >= 1) & ((i - 1) % ft == k))(functools.partial(near, 2 + k))
    for r in range(heads):
        ot = _attn_result(acc_scr.at[r]) * z_ref[r, 0].astype(F32)
        o_ref[r] = ot.T.astype(o_ref.dtype)


def mla_attention(qt, k, vt, zg):
    b, h, nq = qt.shape[:3]
    s = nq * TQ
    hb = MLA_HEADS_PER_STEP
    assert s % (2 * TQ) == 0 and h % hb == 0
    return pl.pallas_call(
        functools.partial(_mla_attn_kernel, heads=hb),
        grid=(b, h // hb, nq),
        in_specs=[
            pl.BlockSpec((1, hb, 1, DKP, TQ), lambda bi, hi, i: (bi, hi, i, 0, 0)),
            pl.BlockSpec((1, hb, s, DKP), lambda bi, hi, i: (bi, hi, 0, 0)),
            pl.BlockSpec((1, hb, nq, DV, TQ), lambda bi, hi, i: (bi, hi, 0, 0, 0)),
            pl.BlockSpec((hb, 1, DV, TQ), lambda bi, hi, i: (hi, bi * nq + i, 0, 0)),
        ],
        out_specs=pl.BlockSpec((hb, TQ, DV), lambda bi, hi, i: (hi, bi * nq + i, 0)),
        out_shape=jax.ShapeDtypeStruct((h, b * s, DV), BF16),
        scratch_shapes=[pltpu.VMEM((hb, 1, TQ), F32), pltpu.VMEM((hb, DV + SUM_ROWS, TQ), F32)],
        compiler_params=_cparams(("parallel", "parallel", "arbitrary")),
        name="mla_attention",
    )(qt, k, vt, zg)


def _pad_heads(w, n_heads, d):
    k = w.shape[0]
    return jnp.pad(w.reshape(k, n_heads, d), ((0, 0), (0, 0), (0, DKP - d))).reshape(k, n_heads * DKP).astype(BF16)


def _head_rows_out(b, s, h, tm, hb, width, dtype):
    nt = s // tm
    spec = pl.BlockSpec((1, hb, tm, width), lambda i, j: (i // nt, j, i % nt, 0))
    return jax.ShapeDtypeStruct((b, h, s, width), dtype), spec


def _head_tiles_out(b, s, h, tm, hb, width, dtype):
    nt = s // tm
    spec = pl.BlockSpec((1, hb, tm // TQ, width, TQ), lambda i, j: (i // nt, j, i % nt, 0, 0))
    return jax.ShapeDtypeStruct((b, h, s // TQ, width, TQ), dtype), spec


def nsa_layer(x2, b, s, norm_w, rel_bias, w_in, q_norm, k_norm, pe_k, w1_k, w2_k, pe_v, w1_v, w2_v, w_out):
    t, d_model = x2.shape
    h, g, dk, dv = NSA_HEADS, NSA_KV_GROUPS, NSA_QK_DIM, NSA_V_DIM
    qw, kvw, zw = h * dk, g * (dk + dv), h * dv
    tm = _pick(s, 1024, TQ)
    hn = rmsnorm_rows(x2, norm_w, d_model, BF16, "nsa_prenorm")

    w_all = w_in.astype(BF16)
    hb = _pick(h, 4)
    q_shape, q_spec = _head_tiles_out(b, s, h, tm, hb, DKP, BF16)
    qt = matmul(hn, w_all, functools.partial(_ep_q_heads, hb=hb, n_true=dk, scale=LOG2E * dk ** -0.5, rope=False),
                tm=tm, tn=hb * dk, out_shapes=q_shape, out_specs=q_spec, cols=(0, qw),
                extra=(jnp.broadcast_to(q_norm[:, None], (dk, TQ)),),
                extra_specs=(pl.BlockSpec((dk, TQ), lambda i, j: (0, 0)),), slabs=True, name="nsa_q_proj")

    def kv_proj(br, normed, block_onehot=False):
        w_br = w_all[:, qw + br * kvw: qw + (br + 1) * kvw]
        w_kv = jnp.concatenate([_pad_heads(w_br[:, :g * dk], g, dk), w_br[:, g * dk:]], axis=1)
        k_out = _head_rows_out(b, s, g, tm, g, DKP, BF16 if normed else F32)
        v_out = _head_tiles_out(b, s, g, tm, g, dv, BF16) if normed else _head_rows_out(b, s, g, tm, g, dv, F32)
        gain = jnp.pad(k_norm[br], (0, DKP - dk)).reshape(1, DKP)
        ep = functools.partial(_ep_nsa_kv, groups=g, n_true=dk, normed=normed,
                               seq_tiles=s // tm if block_onehot else 0)
        return matmul(hn, w_kv, ep,
                      tm=tm, tn=g * (DKP + dv), out_shapes=(k_out[0], v_out[0]), out_specs=(k_out[1], v_out[1]),
                      extra=(gain,), extra_specs=(pl.BlockSpec((1, DKP), lambda i, j: (0, 0)),),
                      slabs=True, name="nsa_kv_proj_%d" % br)

    kc_raw, vc_raw = kv_proj(0, False)
    ks, vst = kv_proj(1, True, block_onehot=True)
    kw, vwt = kv_proj(2, True)
    kc = compress(kc_raw, pe_k, w1_k, w2_k, k_norm[0], dk, "nsa_compress_k")
    vc = compress(vc_raw, pe_v, w1_v, w2_v, jnp.ones((dv,), F32), 0, "nsa_compress_v")
    vct = jnp.swapaxes(vc, -1, -2)

    z0 = qw + 3 * kvw
    w_g = w_all[:, z0 + 3 * zw:]
    sg = matmul(hn, w_g, _ep_sigmoid, tm=tm, tn=3 * h,
                out_shapes=jax.ShapeDtypeStruct((t, 3 * h), F32),
                out_specs=pl.BlockSpec((tm, 3 * h), lambda i, j: (i, j)), name="nsa_gate_proj")
    hbz = max(c for c in range(1, 9) if (3 * h) % c == 0 and z0 % (c * dv) == 0)
    n_zt = 3 * h // hbz
    sub = 8
    sg_tiles = jnp.pad(sg.reshape(t // TQ, TQ, n_zt, hbz).transpose(2, 0, 3, 1),
                       ((0, 0), (0, 0), (0, sub - hbz), (0, 0)))
    gates = matmul(hn, w_all, functools.partial(_ep_gate_heads, hb=hbz, gated=True), tm=tm, tn=hbz * dv,
                   out_shapes=jax.ShapeDtypeStruct((3 * h, t // TQ, dv, TQ), BF16),
                   out_specs=pl.BlockSpec((hbz, tm // TQ, dv, TQ), lambda i, j: (j, i, 0, 0)),
                   cols=(z0, 3 * zw), extra=(sg_tiles,),
                   extra_specs=(pl.BlockSpec((1, tm // TQ, sub, TQ), lambda i, j: (j, i, 0, 0)),),
                   slabs=True, name="nsa_z_proj")

    o = nsa_attention(qt, kc, vct, ks, vst, kw, vwt, gates, rel_bias)
    tmo = _pick(t, 512, 8)
    tno = _pick(d_model, 512, V7X_LANES)
    return matmul(o, w_out.astype(BF16), _ep_residual, tm=tmo, tn=tno,
                  out_shapes=jax.ShapeDtypeStruct((t, d_model), F32),
                  out_specs=pl.BlockSpec((tmo, tno), lambda i, j: (i, j)),
                  extra=(x2,), extra_specs=(pl.BlockSpec((tmo, tno), lambda i, j: (i, j)),), name="nsa_out_proj")


def mla_layer(x2, b, s, positions, norm_w, w_in, q_a_norm, w_q_b, kv_a_norm, w_kv_b, q_norm, k_norm, w_out):
    t, d_model = x2.shape
    h = MLA_HEADS
    dqk = MLA_NOPE_DIM + MLA_ROPE_DIM
    tm = _pick(s, 1024, TQ)
    hn = rmsnorm_rows(x2, norm_w, d_model, BF16, "mla_prenorm")

    half = MLA_ROPE_DIM // 2
    inv_freq = ROPE_BASE ** (-jnp.arange(half, dtype=F32) / half)
    ang = positions.astype(F32).reshape(t, 1) * inv_freq[None, :]
    zeros = jnp.zeros((t, V7X_LANES - MLA_ROPE_DIM), F32)
    cos2 = jnp.concatenate([jnp.cos(ang), jnp.cos(ang), zeros], axis=1)
    sin2 = jnp.concatenate([-jnp.sin(ang), jnp.sin(ang), zeros], axis=1)
    rope_spec = pl.BlockSpec((tm, V7X_LANES), lambda i, j: (i, 0))
    cos2_t = jnp.swapaxes(cos2[:, :MLA_ROPE_DIM].reshape(t // TQ, TQ, MLA_ROPE_DIM), 1, 2)
    sin2_t = jnp.swapaxes(sin2[:, :MLA_ROPE_DIM].reshape(t // TQ, TQ, MLA_ROPE_DIM), 1, 2)
    rope_t_spec = pl.BlockSpec((tm // TQ, MLA_ROPE_DIM, TQ), lambda i, j: (i, 0, 0))
    row_spec = lambda w: pl.BlockSpec((1, w), lambda i, j: (0, 0))

    c0, c1, c2 = MLA_Q_LORA, MLA_Q_LORA + MLA_KV_LORA, MLA_Q_LORA + MLA_KV_LORA + MLA_ROPE_DIM
    w_lat = jnp.concatenate([w_in[:, :c1], jnp.pad(w_in[:, c1:c2], ((0, 0), (0, DKP - MLA_ROPE_DIM)))],
                            axis=1).astype(BF16)
    lat_w = c1 + DKP
    tml = _pick(s, 512, 8)
    row_out = lambda w, dt: (jax.ShapeDtypeStruct((t, w), dt), pl.BlockSpec((tml, w), lambda i, j: (i, 0)))
    lat_outs = (row_out(c0, BF16), row_out(c1 - c0, BF16), row_out(V7X_LANES, F32))
    cq, ckv, kpe = matmul(hn, w_lat, functools.partial(_ep_mla_latent, c0=c0, c1=c1), tm=tml, tn=lat_w,
                          out_shapes=tuple(o[0] for o in lat_outs), out_specs=tuple(o[1] for o in lat_outs),
                          extra=(q_a_norm.reshape(1, c0), kv_a_norm.reshape(1, c1 - c0)),
                          extra_specs=(row_spec(c0), row_spec(c1 - c0)), w_once=True, name="mla_latent_proj")

    hbz = _pick(h, 8)
    zg = matmul(hn, w_in[:, c2:].astype(BF16), functools.partial(_ep_gate_heads, hb=hbz, gated=False),
                tm=tm, tn=hbz * DV, out_shapes=jax.ShapeDtypeStruct((h, t // TQ, DV, TQ), BF16),
                out_specs=pl.BlockSpec((hbz, tm // TQ, DV, TQ), lambda i, j: (j, i, 0, 0)),
                slabs=True, name="mla_z_proj")

    hb = _pick(h, 4)
    q_shape, q_spec = _head_tiles_out(b, s, h, tm, hb, DKP, BF16)
    qt = matmul(cq, w_q_b.astype(BF16),
                functools.partial(_ep_q_heads, hb=hb, n_true=dqk, scale=LOG2E * dqk ** -0.5, rope=True),
                tm=tm, tn=hb * dqk, out_shapes=q_shape, out_specs=q_spec,
                extra=(jnp.broadcast_to(q_norm[:, None], (dqk, TQ)), cos2_t, sin2_t),
                extra_specs=(pl.BlockSpec((dqk, TQ), lambda i, j: (0, 0)), rope_t_spec, rope_t_spec),
                slabs=True, name="mla_q_proj")

    k_shape, k_spec = _head_rows_out(b, s, h, tm, hb, DKP, BF16)
    v_shape, v_spec = _head_tiles_out(b, s, h, tm, hb, DV, BF16)
    gain_n = k_norm[:MLA_NOPE_DIM].reshape(1, MLA_NOPE_DIM)
    gain_r = jnp.pad(k_norm[MLA_NOPE_DIM:], (0, V7X_LANES - MLA_ROPE_DIM)).reshape(1, V7X_LANES)
    k, vt = matmul(ckv, w_kv_b.astype(BF16), functools.partial(_ep_mla_kv, hb=hb),
                   tm=tm, tn=hb * DKP, out_shapes=(k_shape, v_shape), out_specs=(k_spec, v_spec),
                   extra=(kpe, gain_n, gain_r, cos2, sin2),
                   extra_specs=(rope_spec, row_spec(MLA_NOPE_DIM), row_spec(V7X_LANES), rope_spec, rope_spec),
                   slabs=True, name="mla_kv_proj")

    o = mla_attention(qt, k, vt, zg)
    tmo = _pick(t, 512, 8)
    tno = _pick(d_model, 512, V7X_LANES)
    return matmul(o, w_out.astype(BF16), _ep_residual, tm=tmo, tn=tno,
                  out_shapes=jax.ShapeDtypeStruct((t, d_model), F32),
                  out_specs=pl.BlockSpec((tmo, tno), lambda i, j: (i, j)),
                  extra=(x2,), extra_specs=(pl.BlockSpec((tmo, tno), lambda i, j: (i, j)),), name="mla_out_proj")


def kernel(x, positions, norm_w, rel_bias, nsa_w_in, nsa_q_norm, nsa_k_norm, nsa_cmp_pe_k, nsa_cmp_w1_k,
           nsa_cmp_w2_k, nsa_cmp_pe_v, nsa_cmp_w1_v, nsa_cmp_w2_v, nsa_w_out, mla_w_in, mla_q_a_norm,
           mla_w_q_b, mla_kv_a_norm, mla_w_kv_b, mla_q_norm, mla_k_norm, mla_w_out):
    b, s, d_model = x.shape
    x2 = x.reshape(b * s, d_model)
    depth = norm_w.shape[0]
    for layer in range(depth):
        j = layer // 2
        if layer % 2 == 0:
            x2 = nsa_layer(x2, b, s, norm_w[layer], rel_bias, nsa_w_in[j], nsa_q_norm[j], nsa_k_norm[j],
                           nsa_cmp_pe_k[j], nsa_cmp_w1_k[j], nsa_cmp_w2_k[j], nsa_cmp_pe_v[j], nsa_cmp_w1_v[j],
                           nsa_cmp_w2_v[j], nsa_w_out[j])
        else:
            x2 = mla_layer(x2, b, s, positions, norm_w[layer], mla_w_in[j], mla_q_a_norm[j], mla_w_q_b[j],
                           mla_kv_a_norm[j], mla_w_kv_b[j], mla_q_norm[j], mla_k_norm[j], mla_w_out[j])
    return x2.reshape(b, s, d_model)
```
